```python
import math
import jax, jax.numpy as jnp
from jax import lax
import numpy as np

D_MODEL = 1024
BATCH = 8
SEQ = 2048
DEPTH = 4
DEC_BATCH = 128
DEC_SEQ = 4
PAST_LEN = 16384
PAGE_SIZE = 128

N_MIXERS = 3
N_RWKV = len(range(0, DEPTH, N_MIXERS))
N_HGRN = len(range(1, DEPTH, N_MIXERS))
N_GLA = len(range(2, DEPTH, N_MIXERS))

RW_HEAD = 64
RW_HEADS = D_MODEL // RW_HEAD
RW_DECAY_LORA = max(32, int(round(1.8 * D_MODEL ** 0.5 / 32)) * 32)
RW_AAA_LORA = max(32, int(round(1.8 * D_MODEL ** 0.5 / 32)) * 32)
RW_MV_LORA = max(32, int(round(1.3 * D_MODEL ** 0.5 / 32)) * 32)
RW_GATE_LORA = max(32, int(round(0.6 * D_MODEL ** 0.8 / 32)) * 32)
RW_LNX_EPS = 64e-5

HG_HEAD = 128
HG_HEADS = D_MODEL // HG_HEAD

GL_HEADS = 4
GL_DK = (D_MODEL // 2) // GL_HEADS
GL_DV = D_MODEL // GL_HEADS
GL_GATE_RANK = 16
GL_GATE_NORM = 16.0

CHUNK = 32

D_FF = ((8 * D_MODEL // 3 + 127) // 128) * 128
CONV_W = 3

LN_EPS = 1e-5
RMS_EPS = 1e-5
DN_ALPHA = (2 * DEPTH) ** 0.25
DN_BETA = (8 * DEPTH) ** -0.25

kernel_name = "rwkv7_hgrn2_gla_convffn_deepnorm_step"


def layer_norm(x, w, b):
    xf = x.astype(jnp.float32)
    mu = jnp.mean(xf, -1, keepdims=True)
    xc = xf - mu
    var = jnp.mean(xc * xc, -1, keepdims=True)
    return (xc * lax.rsqrt(var + LN_EPS) * w.astype(jnp.float32) + b.astype(jnp.float32)).astype(x.dtype)


def head_rms_gate(o, gain, gate):
    B, T, H, V = o.shape
    o = o * lax.rsqrt(jnp.mean(o * o, -1, keepdims=True) + RMS_EPS) * gain.astype(jnp.float32)
    return (o.reshape(B, T, H * V) * jax.nn.silu(gate.astype(jnp.float32))).astype(gate.dtype)


def chunked_gla(q, k, v, log_g, s0):
    B, T, H, K = q.shape
    V = v.shape[-1]
    C = min(CHUNK, T)
    n = -(-T // C)
    pad = n * C - T

    def blocks(a):
        a = jnp.pad(a.astype(jnp.float32), ((0, 0), (0, pad), (0, 0), (0, 0)))
        return a.reshape(B, n, C, H, a.shape[-1]).transpose(1, 0, 3, 2, 4)

    qc, kc, vc, gc = blocks(q), blocks(k), blocks(v), blocks(log_g)
    mask = jnp.tril(jnp.ones((C, C), dtype=bool))
    ref = (C - 1) // 2

    def step(S, inp):
        qb, kb, vb, gb = inp
        b = jnp.cumsum(gb, axis=-2)
        b_ref = b[:, :, ref:ref + 1]
        b_last = b[:, :, -1:]
        qi = qb * jnp.exp(b - b_ref)
        ki = kb * jnp.exp(b_ref - b)
        A = jnp.where(mask, jnp.einsum('bhck,bhdk->bhcd', qi, ki), 0.0)
        o = jnp.einsum('bhcd,bhdv->bhcv', A, vb) + jnp.einsum('bhck,bhkv->bhcv', qb * jnp.exp(b), S)
        S = jnp.exp(b_last[:, :, 0])[..., None] * S + jnp.einsum('bhck,bhcv->bhkv', kb * jnp.exp(b_last - b), vb)
        return S, o

    S, o = lax.scan(step, s0.astype(jnp.float32), (qc, kc, vc, gc))
    o = o.transpose(1, 0, 3, 2, 4).reshape(B, n * C, H, V)[:, :T]
    return o, S


def rwkv7_time_mix(x, shift_prev, s0, v_first, vmix, mix, wr, wk, wv, wo, w0, w1, w2, a0, a1, a2,
                   g1, g2, k_k, k_a, r_k, lnx_w, lnx_b):
    B, T, D = x.shape
    H, N = RW_HEADS, RW_HEAD
    x_prev = jnp.concatenate([shift_prev[:, None, :].astype(x.dtype), x[:, :-1]], axis=1)
    xx = x_prev - x
    xr, xw, xk, xv, xa, xg = (x + xx * mix[j] for j in range(6))
    r = xr @ wr
    k = xk @ wk
    v = xv @ wv
    w_log = -jnp.exp(-jax.nn.softplus(-(w0 + jnp.tanh(xw @ w1) @ w2).astype(jnp.float32)) - 0.5)
    a = jax.nn.sigmoid(a0 + (xa @ a1) @ a2)
    if vmix is None:
        v_first = v
    else:
        v0, v1, v2 = vmix
        v = v + (v_first - v) * jax.nn.sigmoid(v0 + (xv @ v1) @ v2)
    g = jax.nn.sigmoid(xg @ g1) @ g2

    def hs(t):
        return t.astype(jnp.float32).reshape(B, T, H, N)

    r, k, v, a, w_log = hs(r), hs(k), hs(v), hs(a), hs(w_log)
    kk = k * k_k.astype(jnp.float32).reshape(H, N)
    kk = kk / jnp.maximum(jnp.sqrt(jnp.sum(kk * kk, -1, keepdims=True)), 1e-12)
    k = k * (1.0 + (a - 1.0) * k_a.astype(jnp.float32).reshape(H, N))
    decay = jnp.exp(w_log)

    def step(S, inp):
        r_t, d_t, k_t, v_t, kk_t, a_t = inp
        sa = jnp.einsum('bhvk,bhk->bhv', S, -kk_t)
        S = S * d_t[:, :, None, :] + sa[..., None] * (kk_t * a_t)[:, :, None, :] + v_t[..., None] * k_t[:, :, None, :]
        return S, jnp.einsum('bhvk,bhk->bhv', S, r_t)

    tm = lambda t: jnp.swapaxes(t, 0, 1)
    S, y = lax.scan(step, s0.astype(jnp.float32), (tm(r), tm(decay), tm(k), tm(v), tm(kk), tm(a)))
    y = tm(y)
    mu = jnp.mean(y, -1, keepdims=True)
    yc = y - mu
    y = yc * lax.rsqrt(jnp.mean(yc * yc, -1, keepdims=True) + RW_LNX_EPS)
    y = y.reshape(B, T, D) * lnx_w.astype(jnp.float32) + lnx_b.astype(jnp.float32)
    y = y + (jnp.sum(r * k * r_k.astype(jnp.float32), -1, keepdims=True) * v).reshape(B, T, D)
    out = (y * g.astype(jnp.float32)).astype(x.dtype) @ wo
    return out, x[:, -1], S, v_first


def hgrn2_mix(x, s0, lb, wq, wf, wi, wg, wo, norm_w):
    B, T, D = x.shape
    H, K = HG_HEADS, HG_HEAD
    q = jax.nn.silu((x @ wq).astype(jnp.float32)) * K ** -0.5
    f = (x @ wf).astype(jnp.float32)
    lb = lb.astype(jnp.float32)
    log_g = jnp.logaddexp(jnp.log(lb), jnp.log1p(-lb) + jax.nn.log_sigmoid(f))
    k = (1.0 - lb) * jax.nn.sigmoid(-f)
    i = (x @ wi).astype(jnp.float32)
    hs = lambda t: t.reshape(B, T, H, K)
    o, S = chunked_gla(hs(q), hs(k), hs(i), hs(log_g), s0)
    o = head_rms_gate(o, norm_w, x @ wg)
    return o @ wo, S


def gla_mix(x, s0, wq, wk, wv, wg, gk1, gk2, gk_b, wo, norm_w):
    B, T, D = x.shape
    q = (x @ wq).astype(jnp.float32) * GL_DK ** -0.5
    k = x @ wk
    v = x @ wv
    log_g = jax.nn.log_sigmoid(((x @ gk1) @ gk2 + gk_b).astype(jnp.float32)) / GL_GATE_NORM
    hk = lambda t: t.reshape(B, T, GL_HEADS, GL_DK)
    o, S = chunked_gla(hk(q), hk(k), v.reshape(B, T, GL_HEADS, GL_DV), hk(log_g), s0)
    o = head_rms_gate(o, norm_w, x @ wg)
    return o @ wo, S


def conv_ffn(x, buf, wu, wg, conv_w, conv_b, wd):
    T = x.shape[1]
    u = x @ wu
    z = jnp.concatenate([buf.astype(x.dtype), x @ wg], axis=1)
    zc = conv_b + sum(conv_w[j] * z[:, j:j + T] for j in range(CONV_W))
    return (jax.nn.silu(zc) * u) @ wd, z[:, T:]


def run_trunk(x, st_rw, st_shift, st_hg, st_gl, st_conv, rw_w, rw_vmix, hg_w, hg_lb_param, gl_w, ffn_w, ln_w):
    ln1_w, ln1_b, ln2_w, ln2_b = ln_w
    lb_soft = jax.nn.softmax(hg_lb_param.astype(jnp.float32), axis=0)
    lower_bounds = jnp.cumsum(lb_soft, axis=0) - lb_soft[0]
    new_rw, new_shift, new_hg, new_gl, new_conv = [], [], [], [], []
    v_first = None
    for i in range(DEPTH):
        j = i // N_MIXERS
        kind = i % N_MIXERS
        if kind == 0:
            vm = None if j == 0 else (rw_vmix[0][j - 1], rw_vmix[1][j - 1], rw_vmix[2][j - 1])
            h, sh, S, v_first = rwkv7_time_mix(x, st_shift[j], st_rw[j], v_first, vm, *[p[j] for p in rw_w])
            new_rw.append(S)
            new_shift.append(sh.astype(st_shift.dtype))
        elif kind == 1:
            h, S = hgrn2_mix(x, st_hg[j], lower_bounds[i], *[p[j] for p in hg_w])
            new_hg.append(S)
        else:
            h, S = gla_mix(x, st_gl[j], *[p[j] for p in gl_w])
            new_gl.append(S)
        x = layer_norm(DN_ALPHA * x + h, ln1_w[i], ln1_b[i])
        f, buf = conv_ffn(x, st_conv[i], *[p[i] for p in ffn_w])
        new_conv.append(buf.astype(st_conv.dtype))
        x = layer_norm(DN_ALPHA * x + f, ln2_w[i], ln2_b[i])
    return (x, jnp.stack(new_rw).astype(st_rw.dtype), jnp.stack(new_shift), jnp.stack(new_hg).astype(st_hg.dtype),
            jnp.stack(new_gl).astype(st_gl.dtype), jnp.stack(new_conv))


def setup_inputs(seed: int = 0) -> dict:
    key = jax.random.key(seed)
    ks = iter(jax.random.split(key, 64))

    def nrm(shape, scale=1.0):
        return jax.random.normal(next(ks), shape, jnp.float32) * scale

    def uni(shape, lo, hi):
        return jax.random.uniform(next(ks), shape, jnp.float32, lo, hi)

    D, F = D_MODEL, D_FF
    NR, NH, NG = N_RWKV, N_HGRN, N_GLA
    sd = D ** -0.5
    return {
        "x_prompt": nrm((BATCH, SEQ, D)),
        "x_sample": nrm((DEC_BATCH, DEC_SEQ, D)),
        "state_rwkv": nrm((NR, DEC_BATCH, RW_HEADS, RW_HEAD, RW_HEAD), 0.5),
        "state_rwkv_shift": nrm((NR, DEC_BATCH, D)),
        "state_hgrn": nrm((NH, DEC_BATCH, HG_HEADS, HG_HEAD, HG_HEAD), 0.5),
        "state_gla": nrm((NG, DEC_BATCH, GL_HEADS, GL_DK, GL_DV), 0.5),
        "state_ffn_conv": nrm((DEPTH, DEC_BATCH, CONV_W - 1, F)),
        "rw_mix": uni((NR, 6, D), 0.0, 1.0),
        "rw_wr": nrm((NR, D, D), sd),
        "rw_wk": nrm((NR, D, D), sd),
        "rw_wv": nrm((NR, D, D), sd),
        "rw_wo": nrm((NR, D, D), sd * DN_BETA),
        "rw_w0": uni((NR, D), -6.0, -1.0),
        "rw_w1": nrm((NR, D, RW_DECAY_LORA), sd),
        "rw_w2": nrm((NR, RW_DECAY_LORA, D), 0.1 * RW_DECAY_LORA ** -0.5),
        "rw_a0": nrm((NR, D), 0.1),
        "rw_a1": nrm((NR, D, RW_AAA_LORA), sd),
        "rw_a2": nrm((NR, RW_AAA_LORA, D), 0.1 * RW_AAA_LORA ** -0.5),
        "rw_g1": nrm((NR, D, RW_GATE_LORA), sd),
        "rw_g2": nrm((NR, RW_GATE_LORA, D), RW_GATE_LORA ** -0.5),
        "rw_k_k": 0.85 + nrm((NR, D), 0.05),
        "rw_k_a": 1.0 + nrm((NR, D), 0.05),
        "rw_r_k": nrm((NR, RW_HEADS, RW_HEAD), 0.1),
        "rw_lnx_w": 1.0 + nrm((NR, D), 0.02),
        "rw_lnx_b": nrm((NR, D), 0.02),
        "rw_v0": 1.0 + nrm((NR - 1, D), 0.1),
        "rw_v1": nrm((NR - 1, D, RW_MV_LORA), sd),
        "rw_v2": nrm((NR - 1, RW_MV_LORA, D), 0.1 * RW_MV_LORA ** -0.5),
        "hg_wq": nrm((NH, D, D), sd),
        "hg_wf": nrm((NH, D, D), sd),
        "hg_wi": nrm((NH, D, D), sd),
        "hg_wg": nrm((NH, D, D), sd),
        "hg_wo": nrm((NH, D, D), sd * DN_BETA),
        "hg_norm_w": 1.0 + nrm((NH, HG_HEAD), 0.02),
        "hg_lb_param": nrm((DEPTH, D), 0.1),
        "gl_wq": nrm((NG, D, GL_HEADS * GL_DK), sd),
        "gl_wk": nrm((NG, D, GL_HEADS * GL_DK), sd),
        "gl_wv": nrm((NG, D, D), sd),
        "gl_wg": nrm((NG, D, D), sd),
        "gl_gk1": nrm((NG, D, GL_GATE_RANK), sd),
        "gl_gk2": nrm((NG, GL_GATE_RANK, GL_HEADS * GL_DK), GL_GATE_RANK ** -0.5),
        "gl_gk_b": nrm((NG, GL_HEADS * GL_DK), 0.1),
        "gl_wo": nrm((NG, D, D), sd * DN_BETA),
        "gl_norm_w": 1.0 + nrm((NG, GL_DV), 0.02),
        "ffn_wu": nrm((DEPTH, D, F), sd),
        "ffn_wg": nrm((DEPTH, D, F), sd),
        "ffn_conv_w": nrm((DEPTH, CONV_W, F), CONV_W ** -0.5),
        "ffn_conv_b": nrm((DEPTH, F), 0.02),
        "ffn_wd": nrm((DEPTH, F, D), F ** -0.5 * DN_BETA),
        "ln1_w": 1.0 + nrm((DEPTH, D), 0.02),
        "ln1_b": nrm((DEPTH, D), 0.02),
        "ln2_w": 1.0 + nrm((DEPTH, D), 0.02),
        "ln2_b": nrm((DEPTH, D), 0.02),
    }


def reference(x_prompt, x_sample, state_rwkv, state_rwkv_shift, state_hgrn, state_gla, state_ffn_conv,
              rw_mix, rw_wr, rw_wk, rw_wv, rw_wo, rw_w0, rw_w1, rw_w2, rw_a0, rw_a1, rw_a2, rw_g1, rw_g2,
              rw_k_k, rw_k_a, rw_r_k, rw_lnx_w, rw_lnx_b, rw_v0, rw_v1, rw_v2,
              hg_wq, hg_wf, hg_wi, hg_wg, hg_wo, hg_norm_w, hg_lb_param,
              gl_wq, gl_wk, gl_wv, gl_wg, gl_gk1, gl_gk2, gl_gk_b, gl_wo, gl_norm_w,
              ffn_wu, ffn_wg, ffn_conv_w, ffn_conv_b, ffn_wd, ln1_w, ln1_b, ln2_w, ln2_b):
    rw_w = (rw_mix, rw_wr, rw_wk, rw_wv, rw_wo, rw_w0, rw_w1, rw_w2, rw_a0, rw_a1, rw_a2, rw_g1, rw_g2,
            rw_k_k, rw_k_a, rw_r_k, rw_lnx_w, rw_lnx_b)
    rw_vmix = (rw_v0, rw_v1, rw_v2)
    hg_w = (hg_wq, hg_wf, hg_wi, hg_wg, hg_wo, hg_norm_w)
    gl_w = (gl_wq, gl_wk, gl_wv, gl_wg, gl_gk1, gl_gk2, gl_gk_b, gl_wo, gl_norm_w)
    ffn_w = (ffn_wu, ffn_wg, ffn_conv_w, ffn_conv_b, ffn_wd)
    ln_w = (ln1_w, ln1_b, ln2_w, ln2_b)

    z_rw = jnp.zeros((N_RWKV, BATCH) + state_rwkv.shape[2:], state_rwkv.dtype)
    z_shift = jnp.zeros((N_RWKV, BATCH) + state_rwkv_shift.shape[2:], state_rwkv_shift.dtype)
    z_hg = jnp.zeros((N_HGRN, BATCH) + state_hgrn.shape[2:], state_hgrn.dtype)
    z_gl = jnp.zeros((N_GLA, BATCH) + state_gla.shape[2:], state_gla.dtype)
    z_conv = jnp.zeros((DEPTH, BATCH) + state_ffn_conv.shape[2:], state_ffn_conv.dtype)

    y_prompt, p_rwkv, p_rwkv_shift, p_hgrn, p_gla, p_ffn_conv = run_trunk(
        x_prompt, z_rw, z_shift, z_hg, z_gl, z_conv, rw_w, rw_vmix, hg_w, hg_lb_param, gl_w, ffn_w, ln_w)
    y_sample, s_rwkv, s_rwkv_shift, s_hgrn, s_gla, s_ffn_conv = run_trunk(
        x_sample, state_rwkv, state_rwkv_shift, state_hgrn, state_gla, state_ffn_conv,
        rw_w, rw_vmix, hg_w, hg_lb_param, gl_w, ffn_w, ln_w)
    return (y_prompt, y_sample, p_rwkv, p_rwkv_shift, p_hgrn, p_gla, p_ffn_conv,
            s_rwkv, s_rwkv_shift, s_hgrn, s_gla, s_ffn_conv)
```

```python
import functools
import math

import jax
import jax.numpy as jnp
from jax import lax
from jax.experimental import pallas as pl
from jax.experimental.pallas import tpu as pltpu

D_MODEL = 1024
DEPTH = 4
N_MIXERS = 3
RW_HEAD = 64
RW_HEADS = D_MODEL // RW_HEAD
RW_LNX_EPS = 64e-5
HG_HEAD = 128
HG_HEADS = D_MODEL // HG_HEAD
GL_HEADS = 4
GL_DK = (D_MODEL // 2) // GL_HEADS
GL_DV = D_MODEL // GL_HEADS
GL_GATE_NORM = 16.0
CONV_W = 3
LN_EPS = 1e-5
RMS_EPS = 1e-5
DN_ALPHA = (2 * DEPTH) ** 0.25

LANES = 128
RW_GROUP = 256
RW_GROUP_HEADS = RW_GROUP // RW_HEAD
CHUNK = 32
MIN_CHUNK = 8
VMEM_LIMIT = 56 * 1024 * 1024

BF16 = jnp.bfloat16
F32 = jnp.float32


def _dot(a, b):
    return jnp.dot(a, b, preferred_element_type=F32)


def _dot_nt(a, b):
    return lax.dot_general(a, b, (((1,), (1,)), ((), ())), preferred_element_type=F32)


def _dot_tn(a, b):
    return lax.dot_general(a, b, (((0,), (0,)), ((), ())), preferred_element_type=F32)


def _split3(x):
    hi = x.astype(BF16)
    r1 = x - hi.astype(F32)
    mid = r1.astype(BF16)
    lo = (r1 - mid.astype(F32)).astype(BF16)
    return hi, mid, lo


def _dot_exact_rhs(sel, x):
    hi, mid, lo = _split3(x)
    return _dot(sel, hi) + _dot(sel, mid) + _dot(sel, lo)


def _dot_exact_lhs(x, sel):
    hi, mid, lo = _split3(x)
    return _dot(hi, sel) + _dot(mid, sel) + _dot(lo, sel)


def _iota(shape, dim):
    return lax.broadcasted_iota(jnp.int32, shape, dim)


def _chunk_tri(n, c):
    row, col = _iota((n, n), 0), _iota((n, n), 1)
    sh = int(math.log2(c))
    keep = (col <= row) & ((row >> sh) == (col >> sh))
    return jnp.where(keep, 1.0, 0.0).astype(BF16)


def _mm_body(x_ref, w_ref, o_ref):
    o_ref[...] = _dot(x_ref[...].astype(BF16), w_ref[...])


def _mm(x, w, tm=512):
    m, k = x.shape
    n = w.shape[1]
    tm = min(tm, m)
    assert m % tm == 0
    return pl.pallas_call(
        _mm_body,
        grid=(m // tm,),
        in_specs=[pl.BlockSpec((tm, k), lambda i: (i, 0)), pl.BlockSpec((k, n), lambda i: (0, 0))],
        out_specs=pl.BlockSpec((tm, n), lambda i: (i, 0)),
        out_shape=jax.ShapeDtypeStruct((m, n), F32),
        compiler_params=pltpu.CompilerParams(dimension_semantics=("arbitrary",), vmem_limit_bytes=VMEM_LIMIT),
        name="mm",
    )(x, w)


def _rwkv_body(r_ref, wl_ref, k_ref, v_ref, a_ref, b_ref, s0_ref, y_ref, st_ref, s_scr, l_scr,
               *, chunk, tile, zero_init):
    c = chunk
    gh = RW_GROUP_HEADS
    n_groups = D_MODEL // RW_GROUP
    t_idx = pl.program_id(1)
    log2c = int(math.log2(c))

    bd_r, bd_c = _iota((RW_GROUP, RW_GROUP), 0), _iota((RW_GROUP, RW_GROUP), 1)
    bd256 = (bd_r >> 6) == (bd_c >> 6)
    hm = (_iota((gh * c, RW_GROUP), 0) >> log2c) == (_iota((gh * c, RW_GROUP), 1) >> 6)
    hm_b = jnp.where(hm, 1.0, 0.0).astype(BF16)
    bdm = (_iota((gh * c, gh * c), 0) >> log2c) == (_iota((gh * c, gh * c), 1) >> log2c)
    bdm_b = jnp.where(bdm, 1.0, 0.0).astype(BF16)
    trow = _iota((c, gh * c), 0)
    tcol = _iota((c, gh * c), 1) & (c - 1)
    strict = tcol < trow
    incl = tcol <= trow
    e_r, e_c = _iota((RW_HEAD, RW_GROUP), 0), _iota((RW_HEAD, RW_GROUP), 1)
    expand = jnp.where((e_c & (RW_HEAD - 1)) == e_r, 1.0, 0.0).astype(BF16)
    x_r, x_c = _iota((RW_GROUP, RW_HEAD), 0), _iota((RW_GROUP, RW_HEAD), 1)
    extract = jnp.where((x_r & (RW_HEAD - 1)) == x_c, 1.0, 0.0).astype(BF16)

    @pl.when(t_idx == 0)
    def _init():
        for g in range(n_groups):
            if zero_init:
                s_scr[g] = jnp.zeros((RW_GROUP, RW_GROUP), F32)
            else:
                rows = s0_ref[g * RW_GROUP:(g + 1) * RW_GROUP, :]
                tiled = _dot_exact_lhs(rows, expand)
                s_scr[g] = jnp.where(bd256, tiled, 0.0)

    l_scr[...] = _dot_exact_rhs(_chunk_tri(tile, c), wl_ref[...])

    def stack(x_b):
        return jnp.concatenate([x_b] * gh, axis=0) * hm_b

    def chunk_step(ci, carry):
        rows = pl.ds(pl.multiple_of(ci * c, c), c)
        for g in range(n_groups):
            lanes = slice(g * RW_GROUP, (g + 1) * RW_GROUP)
            r, wl, k, v = r_ref[rows, lanes], wl_ref[rows, lanes], k_ref[rows, lanes], v_ref[rows, lanes]
            a, b = a_ref[rows, lanes], b_ref[rows, lanes]
            lc = l_scr[rows, lanes]
            lend = lc[c - 1:c, :]
            w_inv = jnp.exp(-lc)
            w_end = jnp.exp(lend - lc)
            at = a * jnp.exp(lc - wl)
            rt = r * jnp.exp(lc)
            ar = jnp.concatenate([at, rt], axis=0).astype(BF16)
            bts = stack((b * w_inv).astype(BF16))
            kts = stack((k * w_inv).astype(BF16))
            vs = stack(v.astype(BF16))
            sb = _dot_nt(ar, bts)
            sk = _dot_nt(ar, kts)
            mab = jnp.where(strict, sb[:c], 0.0)
            arb = jnp.where(incl, sb[c:], 0.0)
            mak = jnp.where(strict, sk[:c], 0.0)
            ark = jnp.where(incl, sk[c:], 0.0)
            sg = s_scr[g]
            x2 = _dot_nt(ar, sg.astype(BF16))
            u = x2[:c] + _dot(mak.astype(BF16), vs)
            p = mab
            for i in range(log2c):
                p_b = p.astype(BF16)
                u = u + _dot(p_b, stack(u.astype(BF16)))
                if i < log2c - 1:
                    p = _dot(p_b, jnp.concatenate([p_b] * gh, axis=0) * bdm_b)
            y = x2[c:] + _dot(arb.astype(BF16), stack(u.astype(BF16))) + _dot(ark.astype(BF16), vs)
            y_ref[rows, lanes] = y
            uv = jnp.concatenate([u, v], axis=0).astype(BF16)
            bk = jnp.concatenate([b * w_end, k * w_end], axis=0).astype(BF16)
            s_scr[g] = sg * jnp.exp(lend) + jnp.where(bd256, _dot_tn(uv, bk), 0.0)
        return carry

    lax.fori_loop(0, tile // c, chunk_step, 0)

    @pl.when(t_idx == pl.num_programs(1) - 1)
    def _final():
        for g in range(n_groups):
            sg = jnp.where(bd256, s_scr[g], 0.0)
            st_ref[g * RW_GROUP:(g + 1) * RW_GROUP, :] = _dot_exact_lhs(sg, extract)


def _rwkv_scan(r, wl, k, v, a, b, s0, zero_init):
    bsz, t, d = r.shape
    c = CHUNK if t >= CHUNK else MIN_CHUNK
    tp = -(-t // c) * c
    if tp != t:
        pad = lambda z: jnp.pad(z, ((0, 0), (0, tp - t), (0, 0)))
        r, wl, k, v, a, b = map(pad, (r, wl, k, v, a, b))
    tile = min(tp, 256)
    assert tp % tile == 0
    s0r = s0.reshape(bsz, d, RW_HEAD)
    seq = pl.BlockSpec((None, tile, d), lambda i, j: (i, j, 0))
    st = pl.BlockSpec((None, d, RW_HEAD), lambda i, j: (i, 0, 0))
    y, s_t = pl.pallas_call(
        functools.partial(_rwkv_body, chunk=c, tile=tile, zero_init=zero_init),
        grid=(bsz, tp // tile),
        in_specs=[seq] * 6 + [st],
        out_specs=[seq, st],
        out_shape=[jax.ShapeDtypeStruct((bsz, tp, d), F32), jax.ShapeDtypeStruct((bsz, d, RW_HEAD), F32)],
        scratch_shapes=[pltpu.VMEM((d // RW_GROUP, RW_GROUP, RW_GROUP), F32), pltpu.VMEM((tile, d), F32)],
        compiler_params=pltpu.CompilerParams(dimension_semantics=("arbitrary", "arbitrary"),
                                             vmem_limit_bytes=VMEM_LIMIT),
        name="rwkv_scan",
    )(r, wl, k, v, a, b, s0r)
    return y[:, :t], s_t.reshape(bsz, RW_HEADS, RW_HEAD, RW_HEAD)


def _gla_body(q_ref, k_ref, g_ref, v_ref, s0_ref, o_ref, st_ref, s_scr, l_scr,
              *, chunk, tile, heads, dk, dv, zero_init):
    c = chunk
    ref_row = (c - 1) // 2
    t_idx = pl.program_id(1)
    incl = _iota((c, c), 1) <= _iota((c, c), 0)

    @pl.when(t_idx == 0)
    def _init():
        if zero_init:
            s_scr[...] = jnp.zeros(s_scr.shape, F32)
        else:
            s_scr[...] = s0_ref[...]

    l_scr[...] = _dot_exact_rhs(_chunk_tri(tile, c), g_ref[...])

    def chunk_step(ci, carry):
        rows = pl.ds(pl.multiple_of(ci * c, c), c)
        for h in range(heads):
            kl = slice(h * dk, (h + 1) * dk)
            vl = slice(h * dv, (h + 1) * dv)
            q, k, v = q_ref[rows, kl], k_ref[rows, kl], v_ref[rows, vl]
            bc = l_scr[rows, kl]
            b_ref = bc[ref_row:ref_row + 1, :]
            b_last = bc[c - 1:c, :]
            qi = (q * jnp.exp(bc - b_ref)).astype(BF16)
            ki = (k * jnp.exp(b_ref - bc)).astype(BF16)
            att = jnp.where(incl, _dot_nt(qi, ki), 0.0)
            st = s_scr[h]
            v_b = v.astype(BF16)
            o = _dot(att.astype(BF16), v_b) + _dot_nt((q * jnp.exp(bc)).astype(BF16), st.astype(BF16))
            o_ref[rows, vl] = o
            kd = (k * jnp.exp(b_last - bc)).astype(BF16)
            s_scr[h] = st * jnp.exp(b_last) + _dot_tn(v_b, kd)
        return carry

    lax.fori_loop(0, tile // c, chunk_step, 0)

    @pl.when(t_idx == pl.num_programs(1) - 1)
    def _final():
        st_ref[...] = s_scr[...]


def _gla_scan(q, k, g, v, s0, heads, dk, dv, zero_init):
    bsz, t, _ = q.shape
    c = CHUNK if t >= CHUNK else MIN_CHUNK
    tp = -(-t // c) * c
    if tp != t:
        pad = lambda z: jnp.pad(z, ((0, 0), (0, tp - t), (0, 0)))
        q, k, g, v = map(pad, (q, k, g, v))
    tile = min(tp, 256)
    assert tp % tile == 0
    s0t = jnp.swapaxes(s0, 2, 3)
    kseq = pl.BlockSpec((None, tile, heads * dk), lambda i, j: (i, j, 0))
    vseq = pl.BlockSpec((None, tile, heads * dv), lambda i, j: (i, j, 0))
    st = pl.BlockSpec((None, heads, dv, dk), lambda i, j: (i, 0, 0, 0))
    o, s_t = pl.pallas_call(
        functools.partial(_gla_body, chunk=c, tile=tile, heads=heads, dk=dk, dv=dv, zero_init=zero_init),
        grid=(bsz, tp // tile),
        in_specs=[kseq, kseq, kseq, vseq, st],
        out_specs=[vseq, st],
        out_shape=[jax.ShapeDtypeStruct((bsz, tp, heads * dv), F32),
                   jax.ShapeDtypeStruct((bsz, heads, dv, dk), F32)],
        scratch_shapes=[pltpu.VMEM((heads, dv, dk), F32), pltpu.VMEM((tile, heads * dk), F32)],
        compiler_params=pltpu.CompilerParams(dimension_semantics=("arbitrary", "arbitrary"),
                                             vmem_limit_bytes=VMEM_LIMIT),
        name="gla_scan",
    )(q, k, g, v, s0t)
    return o[:, :t], jnp.swapaxes(s_t, 2, 3)


def _layer_norm(x, w, b):
    mu = jnp.mean(x, -1, keepdims=True)
    xc = x - mu
    var = jnp.mean(xc * xc, -1, keepdims=True)
    return xc * lax.rsqrt(var + LN_EPS) * w + b


def _head_rms_gate(o, gain, gate, heads, dv):
    bsz, t, _ = o.shape
    o = o.reshape(bsz, t, heads, dv)
    o = o * lax.rsqrt(jnp.mean(o * o, -1, keepdims=True) + RMS_EPS) * gain
    return o.reshape(bsz, t, heads * dv) * jax.nn.silu(gate)


def _mm3(x, w):
    bsz, t, kdim = x.shape
    return _mm(x.reshape(bsz * t, kdim), w).reshape(bsz, t, w.shape[1])


def _rwkv_layer(x, shift_prev, s0, v_first, vmix, p, zero_init):
    bsz, t, d = x.shape
    h, n = RW_HEADS, RW_HEAD
    x_prev = jnp.concatenate([shift_prev[:, None, :], x[:, :-1]], axis=1)
    xx = x_prev - x
    xr, xw, xk, xv, xa, xg = (x + xx * p["mix"][j] for j in range(6))
    r = _mm3(xr, p["wr"])
    k = _mm3(xk, p["wk"])
    v = _mm3(xv, p["wv"])
    w_log = -jnp.exp(-jax.nn.softplus(-(p["w0"] + _mm3(jnp.tanh(_mm3(xw, p["w1"])), p["w2"]))) - 0.5)
    a = jax.nn.sigmoid(p["a0"] + _mm3(_mm3(xa, p["a1"]), p["a2"]))
    if vmix is None:
        v_first = v
    else:
        v0, v1, v2 = vmix
        v = v + (v_first - v) * jax.nn.sigmoid(v0 + _mm3(_mm3(xv, v1), v2))
    g = _mm3(jax.nn.sigmoid(_mm3(xg, p["g1"])), p["g2"])

    kk = (k * p["k_k"]).reshape(bsz, t, h, n)
    kk = kk / jnp.maximum(jnp.sqrt(jnp.sum(kk * kk, -1, keepdims=True)), 1e-12)
    kk = kk.reshape(bsz, t, d)
    k = k * (1.0 + (a - 1.0) * p["k_a"])

    y, s_t = _rwkv_scan(r, w_log, k, v, -kk, kk * a, s0, zero_init)

    y = y.reshape(bsz, t, h, n)
    mu = jnp.mean(y, -1, keepdims=True)
    yc = y - mu
    y = yc * lax.rsqrt(jnp.mean(yc * yc, -1, keepdims=True) + RW_LNX_EPS)
    y = y.reshape(bsz, t, d) * p["lnx_w"] + p["lnx_b"]
    rk = jnp.sum((r * k).reshape(bsz, t, h, n) * p["r_k"], -1, keepdims=True)
    y = y + (rk * v.reshape(bsz, t, h, n)).reshape(bsz, t, d)
    out = _mm3(y * g, p["wo"])
    return out, x[:, -1], s_t, v_first


def _hgrn_layer(x, s0, lb, p, zero_init):
    kdim = HG_HEAD
    q = jax.nn.silu(_mm3(x, p["wq"])) * kdim ** -0.5
    f = _mm3(x, p["wf"])
    log_g = jnp.logaddexp(jnp.log(lb), jnp.log1p(-lb) + jax.nn.log_sigmoid(f))
    k = (1.0 - lb) * jax.nn.sigmoid(-f)
    i = _mm3(x, p["wi"])
    o, s_t = _gla_scan(q, k, log_g, i, s0, HG_HEADS, HG_HEAD, HG_HEAD, zero_init)
    o = _head_rms_gate(o, p["norm_w"], _mm3(x, p["wg"]), HG_HEADS, HG_HEAD)
    return _mm3(o, p["wo"]), s_t


def _gla_layer(x, s0, p, zero_init):
    q = _mm3(x, p["wq"]) * GL_DK ** -0.5
    k = _mm3(x, p["wk"])
    v = _mm3(x, p["wv"])
    log_g = jax.nn.log_sigmoid(_mm3(_mm3(x, p["gk1"]), p["gk2"]) + p["gk_b"]) / GL_GATE_NORM
    o, s_t = _gla_scan(q, k, log_g, v, s0, GL_HEADS, GL_DK, GL_DV, zero_init)
    o = _head_rms_gate(o, p["norm_w"], _mm3(x, p["wg"]), GL_HEADS, GL_DV)
    return _mm3(o, p["wo"]), s_t


def _conv_ffn(x, buf, p):
    t = x.shape[1]
    u = _mm3(x, p["wu"])
    z = jnp.concatenate([buf, _mm3(x, p["wg"])], axis=1)
    zc = p["conv_b"] + sum(p["conv_w"][j] * z[:, j:j + t] for j in range(CONV_W))
    return _mm3(jax.nn.silu(zc) * u, p["wd"]), z[:, t:]


def _run_trunk(x, st_rw, st_shift, st_hg, st_gl, st_conv, rw, rw_vmix, hg, lower_bounds, gl, ffn, ln, zero_init):
    new_rw, new_shift, new_hg, new_gl, new_conv = [], [], [], [], []
    v_first = None
    for i in range(DEPTH):
        j = i // N_MIXERS
        kind = i % N_MIXERS
        if kind == 0:
            vm = None if j == 0 else tuple(q[j - 1] for q in rw_vmix)
            hdn, sh, s_t, v_first = _rwkv_layer(x, st_shift[j], st_rw[j], v_first, vm,
                                                {n: q[j] for n, q in rw.items()}, zero_init)
            new_rw.append(s_t)
            new_shift.append(sh)
        elif kind == 1:
            hdn, s_t = _hgrn_layer(x, st_hg[j], lower_bounds[i], {n: q[j] for n, q in hg.items()}, zero_init)
            new_hg.append(s_t)
        else:
            hdn, s_t = _gla_layer(x, st_gl[j], {n: q[j] for n, q in gl.items()}, zero_init)
            new_gl.append(s_t)
        x = _layer_norm(DN_ALPHA * x + hdn, ln["ln1_w"][i], ln["ln1_b"][i])
        f, buf = _conv_ffn(x, st_conv[i], {n: q[i] for n, q in ffn.items()})
        new_conv.append(buf)
        x = _layer_norm(DN_ALPHA * x + f, ln["ln2_w"][i], ln["ln2_b"][i])
    return (x, jnp.stack(new_rw), jnp.stack(new_shift), jnp.stack(new_hg), jnp.stack(new_gl), jnp.stack(new_conv))


def kernel(x_prompt, x_sample, state_rwkv, state_rwkv_shift, state_hgrn, state_gla, state_ffn_conv, rw_mix, rw_wr, rw_wk, rw_wv, rw_wo, rw_w0, rw_w1, rw_w2, rw_a0, rw_a1, rw_a2, rw_g1, rw_g2, rw_k_k, rw_k_a, rw_r_k, rw_lnx_w, rw_lnx_b, rw_v0, rw_v1, rw_v2, hg_wq, hg_wf, hg_wi, hg_wg, hg_wo, hg_norm_w, hg_lb_param, gl_wq, gl_wk, gl_wv, gl_wg, gl_gk1, gl_gk2, gl_gk_b, gl_wo, gl_norm_w, ffn_wu, ffn_wg, ffn_conv_w, ffn_conv_b, ffn_wd, ln1_w, ln1_b, ln2_w, ln2_b):
    bf = lambda w: w.astype(BF16)
    rw = dict(mix=rw_mix, wr=bf(rw_wr), wk=bf(rw_wk), wv=bf(rw_wv), wo=bf(rw_wo), w0=rw_w0, w1=bf(rw_w1),
              w2=bf(rw_w2), a0=rw_a0, a1=bf(rw_a1), a2=bf(rw_a2), g1=bf(rw_g1), g2=bf(rw_g2), k_k=rw_k_k,
              k_a=rw_k_a, r_k=rw_r_k, lnx_w=rw_lnx_w, lnx_b=rw_lnx_b)
    rw_vmix = (rw_v0, bf(rw_v1), bf(rw_v2))
    hg = dict(wq=bf(hg_wq), wf=bf(hg_wf), wi=bf(hg_wi), wg=bf(hg_wg), wo=bf(hg_wo), norm_w=hg_norm_w)
    gl = dict(wq=bf(gl_wq), wk=bf(gl_wk), wv=bf(gl_wv), wg=bf(gl_wg), gk1=bf(gl_gk1), gk2=bf(gl_gk2),
              gk_b=gl_gk_b, wo=bf(gl_wo), norm_w=gl_norm_w)
    ffn = dict(wu=bf(ffn_wu), wg=bf(ffn_wg), conv_w=ffn_conv_w, conv_b=ffn_conv_b, wd=bf(ffn_wd))
    ln = dict(ln1_w=ln1_w, ln1_b=ln1_b, ln2_w=ln2_w, ln2_b=ln2_b)
    lb_soft = jax.nn.softmax(hg_lb_param, axis=0)
    lower_bounds = jnp.cumsum(lb_soft, axis=0) - lb_soft[0]

    nb = x_prompt.shape[0]
    zeros = lambda s: jnp.zeros((s.shape[0], nb) + s.shape[2:], s.dtype)
    p_out = _run_trunk(x_prompt, zeros(state_rwkv), zeros(state_rwkv_shift), zeros(state_hgrn), zeros(state_gla),
                       zeros(state_ffn_conv), rw, rw_vmix, hg, lower_bounds, gl, ffn, ln, True)
    s_out = _run_trunk(x_sample, state_rwkv, state_rwkv_shift, state_hgrn, state_gla, state_ffn_conv,
                       rw, rw_vmix, hg, lower_bounds, gl, ffn, ln, False)
    return (p_out[0], s_out[0]) + tuple(p_out[1:]) + tuple(s_out[1:])
```

```python
import functools
import math

import jax
import jax.numpy as jnp
from jax import lax
from jax.experimental import pallas as pl
from jax.experimental.pallas import tpu as pltpu

D_MODEL = 1024
DEPTH = 4
N_MIXERS = 3
RW_HEAD = 64
RW_HEADS = D_MODEL // RW_HEAD
RW_LNX_EPS = 64e-5
HG_HEAD = 128
HG_HEADS = D_MODEL // HG_HEAD
GL_HEADS = 4
GL_DK = (D_MODEL // 2) // GL_HEADS
GL_DV = D_MODEL // GL_HEADS
GL_GATE_NORM = 16.0
CONV_W = 3
LN_EPS = 1e-5
RMS_EPS = 1e-5
DN_ALPHA = (2 * DEPTH) ** 0.25

LANES = 128
RW_GROUP = 256
RW_GROUP_HEADS = RW_GROUP // RW_HEAD
CHUNK = 32
MIN_CHUNK = 8
RW_BATCH_PER_STEP = 2
VMEM_LIMIT = 56 * 1024 * 1024

BF16 = jnp.bfloat16
F32 = jnp.float32


def _dot(a, b):
    return jnp.dot(a, b, preferred_element_type=F32)


def _dot_nt(a, b):
    return lax.dot_general(a, b, (((1,), (1,)), ((), ())), preferred_element_type=F32)


def _dot_tn(a, b):
    return lax.dot_general(a, b, (((0,), (0,)), ((), ())), preferred_element_type=F32)


def _split3(x):
    hi = x.astype(BF16)
    r1 = x - hi.astype(F32)
    mid = r1.astype(BF16)
    lo = (r1 - mid.astype(F32)).astype(BF16)
    return hi, mid, lo


def _dot_exact_rhs(sel, x):
    hi, mid, lo = _split3(x)
    return _dot(sel, hi) + _dot(sel, mid) + _dot(sel, lo)


def _dot_exact_lhs(x, sel):
    hi, mid, lo = _split3(x)
    return _dot(hi, sel) + _dot(mid, sel) + _dot(lo, sel)


def _iota(shape, dim):
    return lax.broadcasted_iota(jnp.int32, shape, dim)


def _chunk_tri(n, c):
    row, col = _iota((n, n), 0), _iota((n, n), 1)
    sh = int(math.log2(c))
    keep = (col <= row) & ((row >> sh) == (col >> sh))
    return jnp.where(keep, 1.0, 0.0).astype(BF16)


def _mm_body(x_ref, w_ref, o_ref):
    o_ref[...] = _dot(x_ref[...].astype(BF16), w_ref[...])


def _mm(x, w, tm=512):
    m, k = x.shape
    n = w.shape[1]
    tm = min(tm, m)
    assert m % tm == 0
    return pl.pallas_call(
        _mm_body,
        grid=(m // tm,),
        in_specs=[pl.BlockSpec((tm, k), lambda i: (i, 0)), pl.BlockSpec((k, n), lambda i: (0, 0))],
        out_specs=pl.BlockSpec((tm, n), lambda i: (i, 0)),
        out_shape=jax.ShapeDtypeStruct((m, n), F32),
        compiler_params=pltpu.CompilerParams(dimension_semantics=("arbitrary",), vmem_limit_bytes=VMEM_LIMIT),
        name="mm",
    )(x, w)


def _rwkv_body(r_ref, wl_ref, k_ref, v_ref, a_ref, b_ref, s0_ref, y_ref, st_ref, s_scr, l_scr,
               *, chunk, tile, nb, zero_init):
    c = chunk
    gh = RW_GROUP_HEADS
    n_groups = D_MODEL // RW_GROUP
    t_idx = pl.program_id(1)
    log2c = int(math.log2(c))

    bd_r, bd_c = _iota((RW_GROUP, RW_GROUP), 0), _iota((RW_GROUP, RW_GROUP), 1)
    bd256 = (bd_r >> 6) == (bd_c >> 6)
    hm = (_iota((gh * c, RW_GROUP), 0) >> log2c) == (_iota((gh * c, RW_GROUP), 1) >> 6)
    hm_b = jnp.where(hm, 1.0, 0.0).astype(BF16)
    bdm = (_iota((gh * c, gh * c), 0) >> log2c) == (_iota((gh * c, gh * c), 1) >> log2c)
    bdm_b = jnp.where(bdm, 1.0, 0.0).astype(BF16)
    trow = _iota((c, gh * c), 0)
    tcol = _iota((c, gh * c), 1) & (c - 1)
    strict = tcol < trow
    incl = tcol <= trow
    e_r, e_c = _iota((RW_HEAD, RW_GROUP), 0), _iota((RW_HEAD, RW_GROUP), 1)
    expand = jnp.where((e_c & (RW_HEAD - 1)) == e_r, 1.0, 0.0).astype(BF16)
    x_r, x_c = _iota((RW_GROUP, RW_HEAD), 0), _iota((RW_GROUP, RW_HEAD), 1)
    extract = jnp.where((x_r & (RW_HEAD - 1)) == x_c, 1.0, 0.0).astype(BF16)

    chains = [(n, g) for n in range(nb) for g in range(n_groups)]

    @pl.when(t_idx == 0)
    def _init():
        for i, (n, g) in enumerate(chains):
            if zero_init:
                s_scr[i] = jnp.zeros((RW_GROUP, RW_GROUP), F32)
            else:
                rows = s0_ref[n, g * RW_GROUP:(g + 1) * RW_GROUP, :]
                tiled = _dot_exact_lhs(rows, expand)
                s_scr[i] = jnp.where(bd256, tiled, 0.0)

    tri = _chunk_tri(tile, c)
    for n in range(nb):
        l_scr[n] = _dot_exact_rhs(tri, wl_ref[n])

    def stack(x_b):
        return jnp.concatenate([x_b] * gh, axis=0) * hm_b

    def chunk_step(ci, carry):
        rows = pl.ds(pl.multiple_of(ci * c, c), c)
        ld = []
        for n, g in chains:
            lanes = slice(g * RW_GROUP, (g + 1) * RW_GROUP)
            ld.append(dict(r=r_ref[n, rows, lanes], wl=wl_ref[n, rows, lanes], k=k_ref[n, rows, lanes],
                           v=v_ref[n, rows, lanes], a=a_ref[n, rows, lanes], b=b_ref[n, rows, lanes],
                           lc=l_scr[n, rows, lanes]))
        for i, x in enumerate(ld):
            lc = x["lc"]
            x["lend"] = lc[c - 1:c, :]
            w_inv = jnp.exp(-lc)
            at = x["a"] * jnp.exp(lc - x["wl"])
            rt = x["r"] * jnp.exp(lc)
            x["ar"] = jnp.concatenate([at, rt], axis=0).astype(BF16)
            x["bts"] = stack((x["b"] * w_inv).astype(BF16))
            x["kts"] = stack((x["k"] * w_inv).astype(BF16))
            x["vs"] = stack(x["v"].astype(BF16))
            x["sg"] = s_scr[i]
        for x in ld:
            x["sb"] = _dot_nt(x["ar"], x["bts"])
            x["sk"] = _dot_nt(x["ar"], x["kts"])
            x["x2"] = _dot_nt(x["ar"], x["sg"].astype(BF16))
        for x in ld:
            x["p"] = jnp.where(strict, x["sb"][:c], 0.0).astype(BF16)
            x["arb"] = jnp.where(incl, x["sb"][c:], 0.0).astype(BF16)
            mak = jnp.where(strict, x["sk"][:c], 0.0).astype(BF16)
            x["ark"] = jnp.where(incl, x["sk"][c:], 0.0).astype(BF16)
            x["u"] = x["x2"][:c] + _dot(mak, x["vs"])
        for it in range(log2c):
            for x in ld:
                x["u"] = x["u"] + _dot(x["p"], stack(x["u"].astype(BF16)))
            if it < log2c - 1:
                for x in ld:
                    x["p"] = _dot(x["p"], jnp.concatenate([x["p"]] * gh, axis=0) * bdm_b).astype(BF16)
        for x, (n, g) in zip(ld, chains):
            y = x["x2"][c:] + _dot(x["arb"], stack(x["u"].astype(BF16))) + _dot(x["ark"], x["vs"])
            y_ref[n, rows, g * RW_GROUP:(g + 1) * RW_GROUP] = y
        for i, x in enumerate(ld):
            w_end = jnp.exp(x["lend"] - x["lc"])
            uv = jnp.concatenate([x["u"], x["v"]], axis=0).astype(BF16)
            bk = jnp.concatenate([x["b"] * w_end, x["k"] * w_end], axis=0).astype(BF16)
            s_scr[i] = x["sg"] * jnp.exp(x["lend"]) + jnp.where(bd256, _dot_tn(uv, bk), 0.0)
        return carry

    lax.fori_loop(0, tile // c, chunk_step, 0)

    @pl.when(t_idx == pl.num_programs(1) - 1)
    def _final():
        for i, (n, g) in enumerate(chains):
            sg = jnp.where(bd256, s_scr[i], 0.0)
            st_ref[n, g * RW_GROUP:(g + 1) * RW_GROUP, :] = _dot_exact_lhs(sg, extract)


def _rwkv_scan(r, wl, k, v, a, b, s0, zero_init):
    bsz, t, d = r.shape
    c = CHUNK if t >= CHUNK else MIN_CHUNK
    tp = -(-t // c) * c
    if tp != t:
        pad = lambda z: jnp.pad(z, ((0, 0), (0, tp - t), (0, 0)))
        r, wl, k, v, a, b = map(pad, (r, wl, k, v, a, b))
    tile = min(tp, 256)
    nb = RW_BATCH_PER_STEP
    assert tp % tile == 0 and bsz % nb == 0
    s0r = s0.reshape(bsz, d, RW_HEAD)
    seq = pl.BlockSpec((nb, tile, d), lambda i, j: (i, j, 0))
    st = pl.BlockSpec((nb, d, RW_HEAD), lambda i, j: (i, 0, 0))
    y, s_t = pl.pallas_call(
        functools.partial(_rwkv_body, chunk=c, tile=tile, nb=nb, zero_init=zero_init),
        grid=(bsz // nb, tp // tile),
        in_specs=[seq] * 6 + [st],
        out_specs=[seq, st],
        out_shape=[jax.ShapeDtypeStruct((bsz, tp, d), F32), jax.ShapeDtypeStruct((bsz, d, RW_HEAD), F32)],
        scratch_shapes=[pltpu.VMEM((nb * (d // RW_GROUP), RW_GROUP, RW_GROUP), F32),
                        pltpu.VMEM((nb, tile, d), F32)],
        compiler_params=pltpu.CompilerParams(dimension_semantics=("arbitrary", "arbitrary"),
                                             vmem_limit_bytes=VMEM_LIMIT),
        name="rwkv_scan",
    )(r, wl, k, v, a, b, s0r)
    return y[:, :t], s_t.reshape(bsz, RW_HEADS, RW_HEAD, RW_HEAD)


def _gla_body(q_ref, k_ref, g_ref, v_ref, s0_ref, o_ref, st_ref, s_scr, l_scr,
              *, chunk, tile, heads, dk, dv, zero_init):
    c = chunk
    ref_row = (c - 1) // 2
    t_idx = pl.program_id(1)
    incl = _iota((c, c), 1) <= _iota((c, c), 0)

    @pl.when(t_idx == 0)
    def _init():
        if zero_init:
            s_scr[...] = jnp.zeros(s_scr.shape, F32)
        else:
            s_scr[...] = s0_ref[...]

    l_scr[...] = _dot_exact_rhs(_chunk_tri(tile, c), g_ref[...])

    def chunk_step(ci, carry):
        rows = pl.ds(pl.multiple_of(ci * c, c), c)
        ld = []
        for h in range(heads):
            kl = slice(h * dk, (h + 1) * dk)
            vl = slice(h * dv, (h + 1) * dv)
            q, k, v = q_ref[rows, kl], k_ref[rows, kl], v_ref[rows, vl]
            bc = l_scr[rows, kl]
            b_ref = bc[ref_row:ref_row + 1, :]
            b_last = bc[c - 1:c, :]
            ld.append(dict(qi=(q * jnp.exp(bc - b_ref)).astype(BF16), ki=(k * jnp.exp(b_ref - bc)).astype(BF16),
                           qd=(q * jnp.exp(bc)).astype(BF16), kd=(k * jnp.exp(b_last - bc)).astype(BF16),
                           v=v.astype(BF16), w_last=jnp.exp(b_last), st=s_scr[h]))
        for x in ld:
            x["att"] = _dot_nt(x["qi"], x["ki"])
            x["os"] = _dot_nt(x["qd"], x["st"].astype(BF16))
            x["ds"] = _dot_tn(x["v"], x["kd"])
        for h, x in enumerate(ld):
            o_ref[rows, h * dv:(h + 1) * dv] = x["os"] + _dot(jnp.where(incl, x["att"], 0.0).astype(BF16), x["v"])
            s_scr[h] = x["st"] * x["w_last"] + x["ds"]
        return carry

    lax.fori_loop(0, tile // c, chunk_step, 0)

    @pl.when(t_idx == pl.num_programs(1) - 1)
    def _final():
        st_ref[...] = s_scr[...]


def _gla_scan(q, k, g, v, s0, heads, dk, dv, zero_init):
    bsz, t, _ = q.shape
    c = CHUNK if t >= CHUNK else MIN_CHUNK
    tp = -(-t // c) * c
    if tp != t:
        pad = lambda z: jnp.pad(z, ((0, 0), (0, tp - t), (0, 0)))
        q, k, g, v = map(pad, (q, k, g, v))
    tile = min(tp, 256)
    assert tp % tile == 0
    s0t = jnp.swapaxes(s0, 2, 3)
    kseq = pl.BlockSpec((None, tile, heads * dk), lambda i, j: (i, j, 0))
    vseq = pl.BlockSpec((None, tile, heads * dv), lambda i, j: (i, j, 0))
    st = pl.BlockSpec((None, heads, dv, dk), lambda i, j: (i, 0, 0, 0))
    o, s_t = pl.pallas_call(
        functools.partial(_gla_body, chunk=c, tile=tile, heads=heads, dk=dk, dv=dv, zero_init=zero_init),
        grid=(bsz, tp // tile),
        in_specs=[kseq, kseq, kseq, vseq, st],
        out_specs=[vseq, st],
        out_shape=[jax.ShapeDtypeStruct((bsz, tp, heads * dv), F32),
                   jax.ShapeDtypeStruct((bsz, heads, dv, dk), F32)],
        scratch_shapes=[pltpu.VMEM((heads, dv, dk), F32), pltpu.VMEM((tile, heads * dk), F32)],
        compiler_params=pltpu.CompilerParams(dimension_semantics=("arbitrary", "arbitrary"),
                                             vmem_limit_bytes=VMEM_LIMIT),
        name="gla_scan",
    )(q, k, g, v, s0t)
    return o[:, :t], jnp.swapaxes(s_t, 2, 3)


def _layer_norm(x, w, b):
    mu = jnp.mean(x, -1, keepdims=True)
    xc = x - mu
    var = jnp.mean(xc * xc, -1, keepdims=True)
    return xc * lax.rsqrt(var + LN_EPS) * w + b


def _head_rms_gate(o, gain, gate, heads, dv):
    bsz, t, _ = o.shape
    o = o.reshape(bsz, t, heads, dv)
    o = o * lax.rsqrt(jnp.mean(o * o, -1, keepdims=True) + RMS_EPS) * gain
    return o.reshape(bsz, t, heads * dv) * jax.nn.silu(gate)


def _mm3(x, w):
    bsz, t, kdim = x.shape
    return _mm(x.reshape(bsz * t, kdim), w).reshape(bsz, t, w.shape[1])


def _rwkv_layer(x, shift_prev, s0, v_first, vmix, p, zero_init):
    bsz, t, d = x.shape
    h, n = RW_HEADS, RW_HEAD
    x_prev = jnp.concatenate([shift_prev[:, None, :], x[:, :-1]], axis=1)
    xx = x_prev - x
    xr, xw, xk, xv, xa, xg = (x + xx * p["mix"][j] for j in range(6))
    r = _mm3(xr, p["wr"])
    k = _mm3(xk, p["wk"])
    v = _mm3(xv, p["wv"])
    w_log = -jnp.exp(-jax.nn.softplus(-(p["w0"] + _mm3(jnp.tanh(_mm3(xw, p["w1"])), p["w2"]))) - 0.5)
    a = jax.nn.sigmoid(p["a0"] + _mm3(_mm3(xa, p["a1"]), p["a2"]))
    if vmix is None:
        v_first = v
    else:
        v0, v1, v2 = vmix
        v = v + (v_first - v) * jax.nn.sigmoid(v0 + _mm3(_mm3(xv, v1), v2))
    g = _mm3(jax.nn.sigmoid(_mm3(xg, p["g1"])), p["g2"])

    kk = (k * p["k_k"]).reshape(bsz, t, h, n)
    kk = kk / jnp.maximum(jnp.sqrt(jnp.sum(kk * kk, -1, keepdims=True)), 1e-12)
    kk = kk.reshape(bsz, t, d)
    k = k * (1.0 + (a - 1.0) * p["k_a"])

    y, s_t = _rwkv_scan(r, w_log, k, v, -kk, kk * a, s0, zero_init)

    y = y.reshape(bsz, t, h, n)
    mu = jnp.mean(y, -1, keepdims=True)
    yc = y - mu
    y = yc * lax.rsqrt(jnp.mean(yc * yc, -1, keepdims=True) + RW_LNX_EPS)
    y = y.reshape(bsz, t, d) * p["lnx_w"] + p["lnx_b"]
    rk = jnp.sum((r * k).reshape(bsz, t, h, n) * p["r_k"], -1, keepdims=True)
    y = y + (rk * v.reshape(bsz, t, h, n)).reshape(bsz, t, d)
    out = _mm3(y * g, p["wo"])
    return out, x[:, -1], s_t, v_first


def _hgrn_layer(x, s0, lb, p, zero_init):
    kdim = HG_HEAD
    q = jax.nn.silu(_mm3(x, p["wq"])) * kdim ** -0.5
    f = _mm3(x, p["wf"])
    log_g = jnp.logaddexp(jnp.log(lb), jnp.log1p(-lb) + jax.nn.log_sigmoid(f))
    k = (1.0 - lb) * jax.nn.sigmoid(-f)
    i = _mm3(x, p["wi"])
    o, s_t = _gla_scan(q, k, log_g, i, s0, HG_HEADS, HG_HEAD, HG_HEAD, zero_init)
    o = _head_rms_gate(o, p["norm_w"], _mm3(x, p["wg"]), HG_HEADS, HG_HEAD)
    return _mm3(o, p["wo"]), s_t


def _gla_layer(x, s0, p, zero_init):
    q = _mm3(x, p["wq"]) * GL_DK ** -0.5
    k = _mm3(x, p["wk"])
    v = _mm3(x, p["wv"])
    log_g = jax.nn.log_sigmoid(_mm3(_mm3(x, p["gk1"]), p["gk2"]) + p["gk_b"]) / GL_GATE_NORM
    o, s_t = _gla_scan(q, k, log_g, v, s0, GL_HEADS, GL_DK, GL_DV, zero_init)
    o = _head_rms_gate(o, p["norm_w"], _mm3(x, p["wg"]), GL_HEADS, GL_DV)
    return _mm3(o, p["wo"]), s_t


def _conv_ffn(x, buf, p):
    t = x.shape[1]
    u = _mm3(x, p["wu"])
    z = jnp.concatenate([buf, _mm3(x, p["wg"])], axis=1)
    zc = p["conv_b"] + sum(p["conv_w"][j] * z[:, j:j + t] for j in range(CONV_W))
    return _mm3(jax.nn.silu(zc) * u, p["wd"]), z[:, t:]


def _run_trunk(x, st_rw, st_shift, st_hg, st_gl, st_conv, rw, rw_vmix, hg, lower_bounds, gl, ffn, ln, zero_init):
    new_rw, new_shift, new_hg, new_gl, new_conv = [], [], [], [], []
    v_first = None
    for i in range(DEPTH):
        j = i // N_MIXERS
        kind = i % N_MIXERS
        if kind == 0:
            vm = None if j == 0 else tuple(q[j - 1] for q in rw_vmix)
            hdn, sh, s_t, v_first = _rwkv_layer(x, st_shift[j], st_rw[j], v_first, vm,
                                                {n: q[j] for n, q in rw.items()}, zero_init)
            new_rw.append(s_t)
            new_shift.append(sh)
        elif kind == 1:
            hdn, s_t = _hgrn_layer(x, st_hg[j], lower_bounds[i], {n: q[j] for n, q in hg.items()}, zero_init)
            new_hg.append(s_t)
        else:
            hdn, s_t = _gla_layer(x, st_gl[j], {n: q[j] for n, q in gl.items()}, zero_init)
            new_gl.append(s_t)
        x = _layer_norm(DN_ALPHA * x + hdn, ln["ln1_w"][i], ln["ln1_b"][i])
        f, buf = _conv_ffn(x, st_conv[i], {n: q[i] for n, q in ffn.items()})
        new_conv.append(buf)
        x = _layer_norm(DN_ALPHA * x + f, ln["ln2_w"][i], ln["ln2_b"][i])
    return (x, jnp.stack(new_rw), jnp.stack(new_shift), jnp.stack(new_hg), jnp.stack(new_gl), jnp.stack(new_conv))


def kernel(x_prompt, x_sample, state_rwkv, state_rwkv_shift, state_hgrn, state_gla, state_ffn_conv, rw_mix, rw_wr, rw_wk, rw_wv, rw_wo, rw_w0, rw_w1, rw_w2, rw_a0, rw_a1, rw_a2, rw_g1, rw_g2, rw_k_k, rw_k_a, rw_r_k, rw_lnx_w, rw_lnx_b, rw_v0, rw_v1, rw_v2, hg_wq, hg_wf, hg_wi, hg_wg, hg_wo, hg_norm_w, hg_lb_param, gl_wq, gl_wk, gl_wv, gl_wg, gl_gk1, gl_gk2, gl_gk_b, gl_wo, gl_norm_w, ffn_wu, ffn_wg, ffn_conv_w, ffn_conv_b, ffn_wd, ln1_w, ln1_b, ln2_w, ln2_b):
    bf = lambda w: w.astype(BF16)
    rw = dict(mix=rw_mix, wr=bf(rw_wr), wk=bf(rw_wk), wv=bf(rw_wv), wo=bf(rw_wo), w0=rw_w0, w1=bf(rw_w1),
              w2=bf(rw_w2), a0=rw_a0, a1=bf(rw_a1), a2=bf(rw_a2), g1=bf(rw_g1), g2=bf(rw_g2), k_k=rw_k_k,
              k_a=rw_k_a, r_k=rw_r_k, lnx_w=rw_lnx_w, lnx_b=rw_lnx_b)
    rw_vmix = (rw_v0, bf(rw_v1), bf(rw_v2))
    hg = dict(wq=bf(hg_wq), wf=bf(hg_wf), wi=bf(hg_wi), wg=bf(hg_wg), wo=bf(hg_wo), norm_w=hg_norm_w)
    gl = dict(wq=bf(gl_wq), wk=bf(gl_wk), wv=bf(gl_wv), wg=bf(gl_wg), gk1=bf(gl_gk1), gk2=bf(gl_gk2),
              gk_b=gl_gk_b, wo=bf(gl_wo), norm_w=gl_norm_w)
    ffn = dict(wu=bf(ffn_wu), wg=bf(ffn_wg), conv_w=ffn_conv_w, conv_b=ffn_conv_b, wd=bf(ffn_wd))
    ln = dict(ln1_w=ln1_w, ln1_b=ln1_b, ln2_w=ln2_w, ln2_b=ln2_b)
    lb_soft = jax.nn.softmax(hg_lb_param, axis=0)
    lower_bounds = jnp.cumsum(lb_soft, axis=0) - lb_soft[0]

    nb = x_prompt.shape[0]
    zeros = lambda s: jnp.zeros((s.shape[0], nb) + s.shape[2:], s.dtype)
    p_out = _run_trunk(x_prompt, zeros(state_rwkv), zeros(state_rwkv_shift), zeros(state_hgrn), zeros(state_gla),
                       zeros(state_ffn_conv), rw, rw_vmix, hg, lower_bounds, gl, ffn, ln, True)
    s_out = _run_trunk(x_sample, state_rwkv, state_rwkv_shift, state_hgrn, state_gla, state_ffn_conv,
                       rw, rw_vmix, hg, lower_bounds, gl, ffn, ln, False)
    return (p_out[0], s_out[0]) + tuple(p_out[1:]) + tuple(s_out[1:])
```

```python
import functools
import math

import jax
import jax.numpy as jnp
from jax import lax
from jax.experimental import pallas as pl
from jax.experimental.pallas import tpu as pltpu

D_MODEL = 1024
DEPTH = 4
N_MIXERS = 3
RW_HEAD = 64
RW_HEADS = D_MODEL // RW_HEAD
RW_LNX_EPS = 64e-5
HG_HEAD = 128
HG_HEADS = D_MODEL // HG_HEAD
GL_HEADS = 4
GL_DK = (D_MODEL // 2) // GL_HEADS
GL_DV = D_MODEL // GL_HEADS
GL_GATE_NORM = 16.0
CONV_W = 3
LN_EPS = 1e-5
RMS_EPS = 1e-5
DN_ALPHA = (2 * DEPTH) ** 0.25

LANES = 128
RW_GROUP = 256
RW_GROUP_HEADS = RW_GROUP // RW_HEAD
CHUNK = 32
MIN_CHUNK = 8
RW_BATCH_PER_STEP = 2
FFN_ROWS = 512
FFN_COLS = 256
VMEM_LIMIT = 56 * 1024 * 1024

BF16 = jnp.bfloat16
F32 = jnp.float32


def _dot(a, b):
    return jnp.dot(a, b, preferred_element_type=F32)


def _dot_nt(a, b):
    return lax.dot_general(a, b, (((1,), (1,)), ((), ())), preferred_element_type=F32)


def _dot_tn(a, b):
    return lax.dot_general(a, b, (((0,), (0,)), ((), ())), preferred_element_type=F32)


def _split3(x):
    hi = x.astype(BF16)
    r1 = x - hi.astype(F32)
    mid = r1.astype(BF16)
    lo = (r1 - mid.astype(F32)).astype(BF16)
    return hi, mid, lo


def _dot_exact_rhs(sel, x):
    hi, mid, lo = _split3(x)
    return _dot(sel, hi) + _dot(sel, mid) + _dot(sel, lo)


def _dot_exact_lhs(x, sel):
    hi, mid, lo = _split3(x)
    return _dot(hi, sel) + _dot(mid, sel) + _dot(lo, sel)


def _iota(shape, dim):
    return lax.broadcasted_iota(jnp.int32, shape, dim)


def _chunk_tri(n, c):
    row, col = _iota((n, n), 0), _iota((n, n), 1)
    sh = int(math.log2(c))
    keep = (col <= row) & ((row >> sh) == (col >> sh))
    return jnp.where(keep, 1.0, 0.0).astype(BF16)


def _mm_body(x_ref, w_ref, o_ref):
    o_ref[...] = _dot(x_ref[...].astype(BF16), w_ref[...])


def _mm(x, w, tm=512):
    m, k = x.shape
    n = w.shape[1]
    tm = min(tm, m)
    assert m % tm == 0
    return pl.pallas_call(
        _mm_body,
        grid=(m // tm,),
        in_specs=[pl.BlockSpec((tm, k), lambda i: (i, 0)), pl.BlockSpec((k, n), lambda i: (0, 0))],
        out_specs=pl.BlockSpec((tm, n), lambda i: (i, 0)),
        out_shape=jax.ShapeDtypeStruct((m, n), F32),
        compiler_params=pltpu.CompilerParams(dimension_semantics=("arbitrary",), vmem_limit_bytes=VMEM_LIMIT),
        name="mm",
    )(x, w)


def _rwkv_body(r_ref, wl_ref, k_ref, v_ref, a_ref, b_ref, s0_ref, y_ref, st_ref, s_scr, l_scr,
               *, chunk, tile, nb, zero_init):
    c = chunk
    gh = RW_GROUP_HEADS
    n_groups = D_MODEL // RW_GROUP
    t_idx = pl.program_id(1)
    log2c = int(math.log2(c))

    bd_r, bd_c = _iota((RW_GROUP, RW_GROUP), 0), _iota((RW_GROUP, RW_GROUP), 1)
    bd256 = (bd_r >> 6) == (bd_c >> 6)
    hm = (_iota((gh * c, RW_GROUP), 0) >> log2c) == (_iota((gh * c, RW_GROUP), 1) >> 6)
    hm_b = jnp.where(hm, 1.0, 0.0).astype(BF16)
    bdm = (_iota((gh * c, gh * c), 0) >> log2c) == (_iota((gh * c, gh * c), 1) >> log2c)
    bdm_b = jnp.where(bdm, 1.0, 0.0).astype(BF16)
    trow = _iota((c, gh * c), 0)
    tcol = _iota((c, gh * c), 1) & (c - 1)
    strict = tcol < trow
    incl = tcol <= trow
    e_r, e_c = _iota((RW_HEAD, RW_GROUP), 0), _iota((RW_HEAD, RW_GROUP), 1)
    expand = jnp.where((e_c & (RW_HEAD - 1)) == e_r, 1.0, 0.0).astype(BF16)
    x_r, x_c = _iota((RW_GROUP, RW_HEAD), 0), _iota((RW_GROUP, RW_HEAD), 1)
    extract = jnp.where((x_r & (RW_HEAD - 1)) == x_c, 1.0, 0.0).astype(BF16)

    chains = [(n, g) for n in range(nb) for g in range(n_groups)]

    @pl.when(t_idx == 0)
    def _init():
        for i, (n, g) in enumerate(chains):
            if zero_init:
                s_scr[i] = jnp.zeros((RW_GROUP, RW_GROUP), F32)
            else:
                rows = s0_ref[n, g * RW_GROUP:(g + 1) * RW_GROUP, :]
                tiled = _dot_exact_lhs(rows, expand)
                s_scr[i] = jnp.where(bd256, tiled, 0.0)

    tri = _chunk_tri(tile, c)
    for n in range(nb):
        l_scr[n] = _dot_exact_rhs(tri, wl_ref[n])

    def stack(x_b):
        return jnp.concatenate([x_b] * gh, axis=0) * hm_b

    def chunk_step(ci, carry):
        rows = pl.ds(pl.multiple_of(ci * c, c), c)
        ld = []
        for n, g in chains:
            lanes = slice(g * RW_GROUP, (g + 1) * RW_GROUP)
            ld.append(dict(r=r_ref[n, rows, lanes], wl=wl_ref[n, rows, lanes], k=k_ref[n, rows, lanes],
                           v=v_ref[n, rows, lanes], a=a_ref[n, rows, lanes], b=b_ref[n, rows, lanes],
                           lc=l_scr[n, rows, lanes]))
        for i, x in enumerate(ld):
            lc = x["lc"]
            x["lend"] = lc[c - 1:c, :]
            w_inv = jnp.exp(-lc)
            at = x["a"] * jnp.exp(lc - x["wl"])
            rt = x["r"] * jnp.exp(lc)
            x["ar"] = jnp.concatenate([at, rt], axis=0).astype(BF16)
            x["bts"] = stack((x["b"] * w_inv).astype(BF16))
            x["kts"] = stack((x["k"] * w_inv).astype(BF16))
            x["vs"] = stack(x["v"].astype(BF16))
            x["sg"] = s_scr[i]
        for x in ld:
            x["sb"] = _dot_nt(x["ar"], x["bts"])
            x["sk"] = _dot_nt(x["ar"], x["kts"])
            x["x2"] = _dot_nt(x["ar"], x["sg"].astype(BF16))
        for x in ld:
            x["p"] = jnp.where(strict, x["sb"][:c], 0.0).astype(BF16)
            x["arb"] = jnp.where(incl, x["sb"][c:], 0.0).astype(BF16)
            mak = jnp.where(strict, x["sk"][:c], 0.0).astype(BF16)
            x["ark"] = jnp.where(incl, x["sk"][c:], 0.0).astype(BF16)
            x["u"] = x["x2"][:c] + _dot(mak, x["vs"])
        for it in range(log2c):
            for x in ld:
                x["u"] = x["u"] + _dot(x["p"], stack(x["u"].astype(BF16)))
            if it < log2c - 1:
                for x in ld:
                    x["p"] = _dot(x["p"], jnp.concatenate([x["p"]] * gh, axis=0) * bdm_b).astype(BF16)
        for x, (n, g) in zip(ld, chains):
            y = x["x2"][c:] + _dot(x["arb"], stack(x["u"].astype(BF16))) + _dot(x["ark"], x["vs"])
            y_ref[n, rows, g * RW_GROUP:(g + 1) * RW_GROUP] = y
        for i, x in enumerate(ld):
            w_end = jnp.exp(x["lend"] - x["lc"])
            uv = jnp.concatenate([x["u"], x["v"]], axis=0).astype(BF16)
            bk = jnp.concatenate([x["b"] * w_end, x["k"] * w_end], axis=0).astype(BF16)
            s_scr[i] = x["sg"] * jnp.exp(x["lend"]) + jnp.where(bd256, _dot_tn(uv, bk), 0.0)
        return carry

    lax.fori_loop(0, tile // c, chunk_step, 0)

    @pl.when(t_idx == pl.num_programs(1) - 1)
    def _final():
        for i, (n, g) in enumerate(chains):
            sg = jnp.where(bd256, s_scr[i], 0.0)
            st_ref[n, g * RW_GROUP:(g + 1) * RW_GROUP, :] = _dot_exact_lhs(sg, extract)


def _rwkv_scan(r, wl, k, v, a, b, s0, zero_init):
    bsz, t, d = r.shape
    c = CHUNK if t >= CHUNK else MIN_CHUNK
    tp = -(-t // c) * c
    if tp != t:
        pad = lambda z: jnp.pad(z, ((0, 0), (0, tp - t), (0, 0)))
        r, wl, k, v, a, b = map(pad, (r, wl, k, v, a, b))
    tile = min(tp, 256)
    nb = RW_BATCH_PER_STEP
    assert tp % tile == 0 and bsz % nb == 0
    s0r = s0.reshape(bsz, d, RW_HEAD)
    seq = pl.BlockSpec((nb, tile, d), lambda i, j: (i, j, 0))
    st = pl.BlockSpec((nb, d, RW_HEAD), lambda i, j: (i, 0, 0))
    y, s_t = pl.pallas_call(
        functools.partial(_rwkv_body, chunk=c, tile=tile, nb=nb, zero_init=zero_init),
        grid=(bsz // nb, tp // tile),
        in_specs=[seq] * 6 + [st],
        out_specs=[seq, st],
        out_shape=[jax.ShapeDtypeStruct((bsz, tp, d), F32), jax.ShapeDtypeStruct((bsz, d, RW_HEAD), F32)],
        scratch_shapes=[pltpu.VMEM((nb * (d // RW_GROUP), RW_GROUP, RW_GROUP), F32),
                        pltpu.VMEM((nb, tile, d), F32)],
        compiler_params=pltpu.CompilerParams(dimension_semantics=("arbitrary", "arbitrary"),
                                             vmem_limit_bytes=VMEM_LIMIT),
        name="rwkv_scan",
    )(r, wl, k, v, a, b, s0r)
    return y[:, :t], s_t.reshape(bsz, RW_HEADS, RW_HEAD, RW_HEAD)


def _gla_body(q_ref, k_ref, g_ref, v_ref, s0_ref, o_ref, st_ref, s_scr, l_scr,
              *, chunk, tile, heads, dk, dv, zero_init):
    c = chunk
    ref_row = (c - 1) // 2
    t_idx = pl.program_id(1)
    incl = _iota((c, c), 1) <= _iota((c, c), 0)

    @pl.when(t_idx == 0)
    def _init():
        if zero_init:
            s_scr[...] = jnp.zeros(s_scr.shape, F32)
        else:
            s_scr[...] = s0_ref[...]

    l_scr[...] = _dot_exact_rhs(_chunk_tri(tile, c), g_ref[...])

    def chunk_step(ci, carry):
        rows = pl.ds(pl.multiple_of(ci * c, c), c)
        ld = []
        for h in range(heads):
            kl = slice(h * dk, (h + 1) * dk)
            vl = slice(h * dv, (h + 1) * dv)
            q, k, v = q_ref[rows, kl], k_ref[rows, kl], v_ref[rows, vl]
            bc = l_scr[rows, kl]
            b_ref = bc[ref_row:ref_row + 1, :]
            b_last = bc[c - 1:c, :]
            ld.append(dict(qi=(q * jnp.exp(bc - b_ref)).astype(BF16), ki=(k * jnp.exp(b_ref - bc)).astype(BF16),
                           qd=(q * jnp.exp(bc)).astype(BF16), kd=(k * jnp.exp(b_last - bc)).astype(BF16),
                           v=v.astype(BF16), w_last=jnp.exp(b_last), st=s_scr[h]))
        for x in ld:
            x["att"] = _dot_nt(x["qi"], x["ki"])
            x["os"] = _dot_nt(x["qd"], x["st"].astype(BF16))
            x["ds"] = _dot_tn(x["v"], x["kd"])
        for h, x in enumerate(ld):
            o_ref[rows, h * dv:(h + 1) * dv] = x["os"] + _dot(jnp.where(incl, x["att"], 0.0).astype(BF16), x["v"])
            s_scr[h] = x["st"] * x["w_last"] + x["ds"]
        return carry

    lax.fori_loop(0, tile // c, chunk_step, 0)

    @pl.when(t_idx == pl.num_programs(1) - 1)
    def _final():
        st_ref[...] = s_scr[...]


def _gla_scan(q, k, g, v, s0, heads, dk, dv, zero_init):
    bsz, t, _ = q.shape
    c = CHUNK if t >= CHUNK else MIN_CHUNK
    tp = -(-t // c) * c
    if tp != t:
        pad = lambda z: jnp.pad(z, ((0, 0), (0, tp - t), (0, 0)))
        q, k, g, v = map(pad, (q, k, g, v))
    tile = min(tp, 256)
    assert tp % tile == 0
    s0t = jnp.swapaxes(s0, 2, 3)
    kseq = pl.BlockSpec((None, tile, heads * dk), lambda i, j: (i, j, 0))
    vseq = pl.BlockSpec((None, tile, heads * dv), lambda i, j: (i, j, 0))
    st = pl.BlockSpec((None, heads, dv, dk), lambda i, j: (i, 0, 0, 0))
    o, s_t = pl.pallas_call(
        functools.partial(_gla_body, chunk=c, tile=tile, heads=heads, dk=dk, dv=dv, zero_init=zero_init),
        grid=(bsz, tp // tile),
        in_specs=[kseq, kseq, kseq, vseq, st],
        out_specs=[vseq, st],
        out_shape=[jax.ShapeDtypeStruct((bsz, tp, heads * dv), F32),
                   jax.ShapeDtypeStruct((bsz, heads, dv, dk), F32)],
        scratch_shapes=[pltpu.VMEM((heads, dv, dk), F32), pltpu.VMEM((tile, heads * dk), F32)],
        compiler_params=pltpu.CompilerParams(dimension_semantics=("arbitrary", "arbitrary"),
                                             vmem_limit_bytes=VMEM_LIMIT),
        name="gla_scan",
    )(q, k, g, v, s0t)
    return o[:, :t], jnp.swapaxes(s_t, 2, 3)


def _ln_rows(x, w, b):
    mu = jnp.mean(x, -1, keepdims=True)
    xc = x - mu
    var = jnp.mean(xc * xc, -1, keepdims=True)
    return xc * lax.rsqrt(var + LN_EPS) * w + b


def _resident(shape):
    return pl.BlockSpec(shape, lambda i, j: (0,) * len(shape), pipeline_mode=pl.Buffered(1))


def _cf_body(x_ref, o_ref, fill_ref, wo_ref, l1w_ref, l1b_ref, wu_ref, wg_ref, wd_ref, cw_ref, cb_ref,
             l2w_ref, l2b_ref, out_ref, z_ref, h_scr, carry_scr, *, rows, seq_len):
    long_mode = seq_len >= rows
    t_idx = pl.program_id(1)
    x = x_ref[...]
    x1 = _ln_rows(DN_ALPHA * x + _dot(o_ref[...].astype(BF16), wo_ref[...]), l1w_ref[...], l1b_ref[...])
    x1b = x1.astype(BF16)
    row = _iota((rows, FFN_COLS), 0)
    if long_mode:
        @pl.when(t_idx == 0)
        def _load_state():
            carry_scr[...] = fill_ref[...]
    else:
        assert seq_len & (seq_len - 1) == 0
        tpos = row & (seq_len - 1)

    for j in range(z_ref.shape[-1] // FFN_COLS):
        cs = slice(j * FFN_COLS, (j + 1) * FFN_COLS)
        zc = _dot(x1b, wg_ref[:, cs])
        uc = _dot(x1b, wu_ref[:, cs])
        r1 = pltpu.roll(zc, 1, 0)
        r2 = pltpu.roll(zc, 2, 0)
        if long_mode:
            c0, c1 = carry_scr[0:1, cs], carry_scr[1:2, cs]
            zp1 = jnp.where(row == 0, c1, r1)
            zp2 = jnp.where(row == 0, c0, jnp.where(row == 1, c1, r2))
            carry_scr[:, cs] = zc[rows - 2:rows, :]
        else:
            fill = fill_ref[:, cs]
            zp1 = jnp.where(tpos >= 1, r1, pltpu.roll(fill, rows - 1, 0))
            zp2 = jnp.where(tpos >= 2, r2, fill)
            z_ref[:, cs] = zc
        cw = cw_ref[:, cs]
        pre = cb_ref[:, cs] + cw[0:1] * zp2 + cw[1:2] * zp1 + cw[2:3] * zc
        h_scr[:, cs] = (pre * jax.nn.sigmoid(pre) * uc).astype(BF16)

    out_ref[...] = _ln_rows(DN_ALPHA * x1 + _dot(h_scr[...], wd_ref[...]), l2w_ref[...], l2b_ref[...])
    if long_mode:
        @pl.when(t_idx == pl.num_programs(1) - 1)
        def _store_state():
            z_ref[...] = carry_scr[...]


def _proj_ffn(x, o, buf, wo, ln1, p, ln2):
    bsz, t, d = x.shape
    f = p["wu"].shape[1]
    long_mode = t >= FFN_ROWS
    if long_mode:
        rows = FFN_ROWS
        xs, os_, fill = x, o, buf
        z_shape, z_rows = (bsz, CONV_W - 1, f), CONV_W - 1
    else:
        rows = bsz * t
        xs, os_ = x.reshape(1, rows, d), o.reshape(1, rows, d)
        fill = jnp.concatenate([buf, jnp.zeros((bsz, t - (CONV_W - 1), f), buf.dtype)], axis=1).reshape(1, rows, f)
        z_shape, z_rows = (1, rows, f), rows
    nseq, tt = xs.shape[0], xs.shape[1]
    assert tt % rows == 0 and f % FFN_COLS == 0
    seq = pl.BlockSpec((None, rows, d), lambda i, j: (i, j, 0))
    vec = lambda n: _resident((1, n))
    out, z = pl.pallas_call(
        functools.partial(_cf_body, rows=rows, seq_len=t),
        grid=(nseq, tt // rows),
        in_specs=[seq, seq, pl.BlockSpec((None, fill.shape[1], f), lambda i, j: (i, 0, 0)),
                  _resident((d, d)), vec(d), vec(d), _resident((d, f)), _resident((d, f)), _resident((f, d)),
                  _resident((CONV_W, f)), vec(f), vec(d), vec(d)],
        out_specs=[seq, pl.BlockSpec((None, z_rows, f), lambda i, j: (i, 0, 0))],
        out_shape=[jax.ShapeDtypeStruct(xs.shape, F32), jax.ShapeDtypeStruct(z_shape, F32)],
        scratch_shapes=[pltpu.VMEM((rows, f), BF16), pltpu.VMEM((CONV_W - 1, f), F32)],
        compiler_params=pltpu.CompilerParams(dimension_semantics=("arbitrary", "arbitrary"),
                                             vmem_limit_bytes=VMEM_LIMIT),
        name="proj_ffn",
    )(xs, os_, fill, wo, ln1[0].reshape(1, d), ln1[1].reshape(1, d), p["wu"], p["wg"], p["wd"],
      p["conv_w"], p["conv_b"].reshape(1, f), ln2[0].reshape(1, d), ln2[1].reshape(1, d))
    if long_mode:
        return out, z
    return out.reshape(bsz, t, d), z.reshape(bsz, t, f)[:, t - (CONV_W - 1):]


def _layer_norm(x, w, b):
    mu = jnp.mean(x, -1, keepdims=True)
    xc = x - mu
    var = jnp.mean(xc * xc, -1, keepdims=True)
    return xc * lax.rsqrt(var + LN_EPS) * w + b


def _head_rms_gate(o, gain, gate, heads, dv):
    bsz, t, _ = o.shape
    o = o.reshape(bsz, t, heads, dv)
    o = o * lax.rsqrt(jnp.mean(o * o, -1, keepdims=True) + RMS_EPS) * gain
    return o.reshape(bsz, t, heads * dv) * jax.nn.silu(gate)


def _mm3(x, w):
    bsz, t, kdim = x.shape
    return _mm(x.reshape(bsz * t, kdim), w).reshape(bsz, t, w.shape[1])


def _rwkv_layer(x, shift_prev, s0, v_first, vmix, p, zero_init):
    bsz, t, d = x.shape
    h, n = RW_HEADS, RW_HEAD
    x_prev = jnp.concatenate([shift_prev[:, None, :], x[:, :-1]], axis=1)
    xx = x_prev - x
    xr, xw, xk, xv, xa, xg = (x + xx * p["mix"][j] for j in range(6))
    r = _mm3(xr, p["wr"])
    k = _mm3(xk, p["wk"])
    v = _mm3(xv, p["wv"])
    w_log = -jnp.exp(-jax.nn.softplus(-(p["w0"] + _mm3(jnp.tanh(_mm3(xw, p["w1"])), p["w2"]))) - 0.5)
    a = jax.nn.sigmoid(p["a0"] + _mm3(_mm3(xa, p["a1"]), p["a2"]))
    if vmix is None:
        v_first = v
    else:
        v0, v1, v2 = vmix
        v = v + (v_first - v) * jax.nn.sigmoid(v0 + _mm3(_mm3(xv, v1), v2))
    g = _mm3(jax.nn.sigmoid(_mm3(xg, p["g1"])), p["g2"])

    kk = (k * p["k_k"]).reshape(bsz, t, h, n)
    kk = kk / jnp.maximum(jnp.sqrt(jnp.sum(kk * kk, -1, keepdims=True)), 1e-12)
    kk = kk.reshape(bsz, t, d)
    k = k * (1.0 + (a - 1.0) * p["k_a"])

    y, s_t = _rwkv_scan(r, w_log, k, v, -kk, kk * a, s0, zero_init)

    y = y.reshape(bsz, t, h, n)
    mu = jnp.mean(y, -1, keepdims=True)
    yc = y - mu
    y = yc * lax.rsqrt(jnp.mean(yc * yc, -1, keepdims=True) + RW_LNX_EPS)
    y = y.reshape(bsz, t, d) * p["lnx_w"] + p["lnx_b"]
    rk = jnp.sum((r * k).reshape(bsz, t, h, n) * p["r_k"], -1, keepdims=True)
    y = y + (rk * v.reshape(bsz, t, h, n)).reshape(bsz, t, d)
    return y * g, x[:, -1], s_t, v_first


def _hgrn_layer(x, s0, lb, p, zero_init):
    kdim = HG_HEAD
    q = jax.nn.silu(_mm3(x, p["wq"])) * kdim ** -0.5
    f = _mm3(x, p["wf"])
    log_g = jnp.logaddexp(jnp.log(lb), jnp.log1p(-lb) + jax.nn.log_sigmoid(f))
    k = (1.0 - lb) * jax.nn.sigmoid(-f)
    i = _mm3(x, p["wi"])
    o, s_t = _gla_scan(q, k, log_g, i, s0, HG_HEADS, HG_HEAD, HG_HEAD, zero_init)
    return _head_rms_gate(o, p["norm_w"], _mm3(x, p["wg"]), HG_HEADS, HG_HEAD), s_t


def _gla_layer(x, s0, p, zero_init):
    q = _mm3(x, p["wq"]) * GL_DK ** -0.5
    k = _mm3(x, p["wk"])
    v = _mm3(x, p["wv"])
    log_g = jax.nn.log_sigmoid(_mm3(_mm3(x, p["gk1"]), p["gk2"]) + p["gk_b"]) / GL_GATE_NORM
    o, s_t = _gla_scan(q, k, log_g, v, s0, GL_HEADS, GL_DK, GL_DV, zero_init)
    return _head_rms_gate(o, p["norm_w"], _mm3(x, p["wg"]), GL_HEADS, GL_DV), s_t


def _run_trunk(x, st_rw, st_shift, st_hg, st_gl, st_conv, rw, rw_vmix, hg, lower_bounds, gl, ffn, ln, zero_init):
    new_rw, new_shift, new_hg, new_gl, new_conv = [], [], [], [], []
    v_first = None
    for i in range(DEPTH):
        j = i // N_MIXERS
        kind = i % N_MIXERS
        if kind == 0:
            vm = None if j == 0 else tuple(q[j - 1] for q in rw_vmix)
            o, sh, s_t, v_first = _rwkv_layer(x, st_shift[j], st_rw[j], v_first, vm,
                                              {n: q[j] for n, q in rw.items()}, zero_init)
            wo = rw["wo"][j]
            new_rw.append(s_t)
            new_shift.append(sh)
        elif kind == 1:
            o, s_t = _hgrn_layer(x, st_hg[j], lower_bounds[i], {n: q[j] for n, q in hg.items()}, zero_init)
            wo = hg["wo"][j]
            new_hg.append(s_t)
        else:
            o, s_t = _gla_layer(x, st_gl[j], {n: q[j] for n, q in gl.items()}, zero_init)
            wo = gl["wo"][j]
            new_gl.append(s_t)
        x, buf = _proj_ffn(x, o, st_conv[i], wo, (ln["ln1_w"][i], ln["ln1_b"][i]),
                           {n: q[i] for n, q in ffn.items()}, (ln["ln2_w"][i], ln["ln2_b"][i]))
        new_conv.append(buf)
    return (x, jnp.stack(new_rw), jnp.stack(new_shift), jnp.stack(new_hg), jnp.stack(new_gl), jnp.stack(new_conv))


def kernel(x_prompt, x_sample, state_rwkv, state_rwkv_shift, state_hgrn, state_gla, state_ffn_conv, rw_mix, rw_wr, rw_wk, rw_wv, rw_wo, rw_w0, rw_w1, rw_w2, rw_a0, rw_a1, rw_a2, rw_g1, rw_g2, rw_k_k, rw_k_a, rw_r_k, rw_lnx_w, rw_lnx_b, rw_v0, rw_v1, rw_v2, hg_wq, hg_wf, hg_wi, hg_wg, hg_wo, hg_norm_w, hg_lb_param, gl_wq, gl_wk, gl_wv, gl_wg, gl_gk1, gl_gk2, gl_gk_b, gl_wo, gl_norm_w, ffn_wu, ffn_wg, ffn_conv_w, ffn_conv_b, ffn_wd, ln1_w, ln1_b, ln2_w, ln2_b):
    bf = lambda w: w.astype(BF16)
    rw = dict(mix=rw_mix, wr=bf(rw_wr), wk=bf(rw_wk), wv=bf(rw_wv), wo=bf(rw_wo), w0=rw_w0, w1=bf(rw_w1),
              w2=bf(rw_w2), a0=rw_a0, a1=bf(rw_a1), a2=bf(rw_a2), g1=bf(rw_g1), g2=bf(rw_g2), k_k=rw_k_k,
              k_a=rw_k_a, r_k=rw_r_k, lnx_w=rw_lnx_w, lnx_b=rw_lnx_b)
    rw_vmix = (rw_v0, bf(rw_v1), bf(rw_v2))
    hg = dict(wq=bf(hg_wq), wf=bf(hg_wf), wi=bf(hg_wi), wg=bf(hg_wg), wo=bf(hg_wo), norm_w=hg_norm_w)
    gl = dict(wq=bf(gl_wq), wk=bf(gl_wk), wv=bf(gl_wv), wg=bf(gl_wg), gk1=bf(gl_gk1), gk2=bf(gl_gk2),
              gk_b=gl_gk_b, wo=bf(gl_wo), norm_w=gl_norm_w)
    ffn = dict(wu=bf(ffn_wu), wg=bf(ffn_wg), conv_w=ffn_conv_w, conv_b=ffn_conv_b, wd=bf(ffn_wd))
    ln = dict(ln1_w=ln1_w, ln1_b=ln1_b, ln2_w=ln2_w, ln2_b=ln2_b)
    lb_soft = jax.nn.softmax(hg_lb_param, axis=0)
    lower_bounds = jnp.cumsum(lb_soft, axis=0) - lb_soft[0]

    nb = x_prompt.shape[0]
    zeros = lambda s: jnp.zeros((s.shape[0], nb) + s.shape[2:], s.dtype)
    p_out = _run_trunk(x_prompt, zeros(state_rwkv), zeros(state_rwkv_shift), zeros(state_hgrn), zeros(state_gla),
                       zeros(state_ffn_conv), rw, rw_vmix, hg, lower_bounds, gl, ffn, ln, True)
    s_out = _run_trunk(x_sample, state_rwkv, state_rwkv_shift, state_hgrn, state_gla, state_ffn_conv,
                       rw, rw_vmix, hg, lower_bounds, gl, ffn, ln, False)
    return (p_out[0], s_out[0]) + tuple(p_out[1:]) + tuple(s_out[1:])
```

```python
import functools
import math

import jax
import jax.numpy as jnp
from jax import lax
from jax.experimental import pallas as pl
from jax.experimental.pallas import tpu as pltpu

D_MODEL = 1024
DEPTH = 4
N_MIXERS = 3
RW_HEAD = 64
RW_HEADS = D_MODEL // RW_HEAD
RW_LNX_EPS = 64e-5
HG_HEAD = 128
HG_HEADS = D_MODEL // HG_HEAD
GL_HEADS = 4
GL_DK = (D_MODEL // 2) // GL_HEADS
GL_DV = D_MODEL // GL_HEADS
GL_GATE_NORM = 16.0
CONV_W = 3
LN_EPS = 1e-5
RMS_EPS = 1e-5
DN_ALPHA = (2 * DEPTH) ** 0.25

LANES = 128
SUBLANES = 8
RW_GROUP = 256
RW_GROUP_HEADS = RW_GROUP // RW_HEAD
CHUNK = 32
RW_BATCH_PER_STEP = 2
SCAN_ROWS = 256
PRE_ROWS = 256
FFN_ROWS = 512
FFN_COLS = 256
VMEM_LIMIT = 56 * 1024 * 1024

BF16 = jnp.bfloat16
F32 = jnp.float32


def _dot(a, b):
    return jnp.dot(a, b, preferred_element_type=F32)


def _dot_nt(a, b):
    return lax.dot_general(a, b, (((1,), (1,)), ((), ())), preferred_element_type=F32)


def _dot_tn(a, b):
    return lax.dot_general(a, b, (((0,), (0,)), ((), ())), preferred_element_type=F32)


def _split3(x):
    hi = x.astype(BF16)
    r1 = x - hi.astype(F32)
    mid = r1.astype(BF16)
    lo = (r1 - mid.astype(F32)).astype(BF16)
    return hi, mid, lo


def _dot_exact_rhs(sel, x):
    hi, mid, lo = _split3(x)
    return _dot(sel, hi) + _dot(sel, mid) + _dot(sel, lo)


def _dot_exact_lhs(x, sel):
    hi, mid, lo = _split3(x)
    return _dot(hi, sel) + _dot(mid, sel) + _dot(lo, sel)


def _head_sums(x, ones_bd):
    hi = x.astype(BF16)
    lo = (x - hi.astype(F32)).astype(BF16)
    return _dot(hi, ones_bd) + _dot(lo, ones_bd)


def _iota(shape, dim):
    return lax.broadcasted_iota(jnp.int32, shape, dim)


def _chunk_tri(n, c):
    row, col = _iota((n, n), 0), _iota((n, n), 1)
    sh = int(math.log2(c))
    keep = (col <= row) & ((row >> sh) == (col >> sh))
    return jnp.where(keep, 1.0, 0.0).astype(BF16)


def _softplus(t):
    return jnp.maximum(t, 0.0) + jnp.log(1.0 + jnp.exp(-jnp.abs(t)))


def _ln_rows(x, w, b):
    mu = jnp.mean(x, -1, keepdims=True)
    xc = x - mu
    var = jnp.mean(xc * xc, -1, keepdims=True)
    return xc * lax.rsqrt(var + LN_EPS) * w + b


def _resident(shape):
    return pl.BlockSpec(shape, lambda i, j: (0,) * len(shape), pipeline_mode=pl.Buffered(1))


def _params():
    return pltpu.CompilerParams(dimension_semantics=("arbitrary", "arbitrary"), vmem_limit_bytes=VMEM_LIMIT)


class _Rows:
    def __init__(self, bsz, t, valid, max_rows):
        self.bsz, self.t, self.valid = bsz, t, valid
        self.long = t >= max_rows
        if self.long:
            assert t % max_rows == 0 and valid == t
            self.rows, self.grid, self.view = max_rows, (bsz, t // max_rows), (bsz, t)
        else:
            total = bsz * t
            self.rows = min(total, max_rows)
            assert t & (t - 1) == 0 and self.rows % t == 0 and total % self.rows == 0
            self.grid, self.view = (1, total // self.rows), (1, total)

    def flat(self, x):
        return x.reshape(self.view + x.shape[2:])

    def unflat(self, x):
        return x.reshape((self.bsz, self.t) + x.shape[2:])

    def seq_spec(self, width):
        return pl.BlockSpec((None, self.rows, width), lambda i, j: (i, j, 0))

    def state_rows(self, state):
        if self.long:
            return state
        return self.flat(jnp.pad(state, ((0, 0), (0, self.t - state.shape[1]), (0, 0))))

    def state_spec(self, n, width):
        if self.long:
            return pl.BlockSpec((None, n, width), lambda i, j: (i, 0, 0))
        return self.seq_spec(width)


def _prev_rows(x, shift, rl, t_idx, carry_scr, fill_ref, n_prev):
    rows, width = x.shape
    row = _iota((rows, width), 0)
    rolled = pltpu.roll(x, shift, 0)
    if rl.long:
        out = rolled
        for i in range(shift):
            src = n_prev - shift + i
            out = jnp.where(row == i, carry_scr[src:src + 1, :], out)
        return out
    fill = fill_ref[...]
    if n_prev != shift:
        fill = pltpu.roll(fill, rows - (n_prev - shift), 0)
    return jnp.where((row & (rl.t - 1)) >= shift, rolled, fill)


def _rwkv_pre_body(*refs, rl, has_vmix):
    it = iter(refs)
    x_ref, fill_ref = next(it), next(it)
    vf_ref = next(it) if has_vmix else None
    mix_ref, wr_ref, wk_ref, wv_ref, w0_ref, w1_ref, w2_ref = (next(it) for _ in range(7))
    a0_ref, a1_ref, a2_ref, g1_ref, g2_ref, kk_ref, ka_ref = (next(it) for _ in range(7))
    v0_ref, v1_ref, v2_ref = (next(it) for _ in range(3)) if has_vmix else (None, None, None)
    r_out, wl_out, k_out, v_out, a_out, b_out, g_out, carry_scr = (next(it) for _ in range(8))

    t_idx = pl.program_id(1)
    rows = rl.rows
    x = x_ref[...]
    if rl.long:
        @pl.when(t_idx == 0)
        def _load_state():
            carry_scr[0:1, :] = fill_ref[...]
    x_prev = _prev_rows(x, 1, rl, t_idx, carry_scr, fill_ref, 1)
    if rl.long:
        carry_scr[0:1, :] = x[rows - 1:rows, :]
    xx = x_prev - x
    mixed = [(x + xx * mix_ref[j:j + 1, :]).astype(BF16) for j in range(6)]
    xr, xw, xk, xv, xa, xg = mixed
    r = _dot(xr, wr_ref[...])
    k = _dot(xk, wk_ref[...])
    v = _dot(xv, wv_ref[...])
    wl = -jnp.exp(-_softplus(-(w0_ref[...] + _dot(jnp.tanh(_dot(xw, w1_ref[...])).astype(BF16), w2_ref[...]))) - 0.5)
    a = jax.nn.sigmoid(a0_ref[...] + _dot(_dot(xa, a1_ref[...]).astype(BF16), a2_ref[...]))
    if has_vmix:
        gate = jax.nn.sigmoid(v0_ref[...] + _dot(_dot(xv, v1_ref[...]).astype(BF16), v2_ref[...]))
        v = v + (vf_ref[...] - v) * gate
    g_out[...] = _dot(jax.nn.sigmoid(_dot(xg, g1_ref[...])).astype(BF16), g2_ref[...])

    kk = k * kk_ref[...]
    k = k * (1.0 + (a - 1.0) * ka_ref[...])
    if not rl.long and rl.valid < rl.t:
        live = (_iota((rows, D_MODEL), 0) & (rl.t - 1)) < rl.valid
        wl, k, v, kk = (jnp.where(live, z, 0.0) for z in (wl, k, v, kk))
    r_out[...] = r
    wl_out[...] = wl
    k_out[...] = k
    v_out[...] = v
    bd = (_iota((RW_GROUP, RW_GROUP), 0) >> 6) == (_iota((RW_GROUP, RW_GROUP), 1) >> 6)
    ones_bd = jnp.where(bd, 1.0, 0.0).astype(BF16)
    for g in range(D_MODEL // RW_GROUP):
        lanes = slice(g * RW_GROUP, (g + 1) * RW_GROUP)
        kkg = kk[:, lanes]
        kkn = kkg / jnp.maximum(jnp.sqrt(_head_sums(kkg * kkg, ones_bd)), 1e-12)
        a_out[:, lanes] = -kkn
        b_out[:, lanes] = kkn * a[:, lanes]


def _rwkv_pre(x, shift, v_first, vmix, p, valid):
    bsz, t, d = x.shape
    rl = _Rows(bsz, t, valid, PRE_ROWS)
    has_vmix = vmix is not None
    seq = rl.seq_spec(d)
    vec = lambda z: z.reshape(1, d)
    args = [rl.flat(x), rl.state_rows(shift[:, None, :])]
    specs = [seq, rl.state_spec(1, d)]
    if has_vmix:
        args.append(rl.flat(v_first))
        specs.append(seq)
    weights = [p["mix"], p["wr"], p["wk"], p["wv"], vec(p["w0"]), p["w1"], p["w2"], vec(p["a0"]), p["a1"], p["a2"],
               p["g1"], p["g2"], vec(p["k_k"]), vec(p["k_a"])]
    if has_vmix:
        weights += [vec(vmix[0]), vmix[1], vmix[2]]
    args += weights
    specs += [_resident(w.shape) for w in weights]
    outs = pl.pallas_call(
        functools.partial(_rwkv_pre_body, rl=rl, has_vmix=has_vmix),
        grid=rl.grid,
        in_specs=specs,
        out_specs=[seq] * 7,
        out_shape=[jax.ShapeDtypeStruct(rl.view + (d,), F32)] * 7,
        scratch_shapes=[pltpu.VMEM((SUBLANES, d), F32)],
        compiler_params=_params(),
        name="rwkv_pre",
    )(*args)
    return [rl.unflat(o) for o in outs]


def _rwkv_body(r_ref, wl_ref, k_ref, v_ref, a_ref, b_ref, g_ref, s0_ref, rk_ref, lw_ref, lb_ref,
               y_ref, st_ref, s_scr, l_scr, *, chunk, tile, nb, zero_init):
    c = chunk
    gh = RW_GROUP_HEADS
    n_groups = D_MODEL // RW_GROUP
    t_idx = pl.program_id(1)
    log2c = int(math.log2(c))

    bd_r, bd_c = _iota((RW_GROUP, RW_GROUP), 0), _iota((RW_GROUP, RW_GROUP), 1)
    bd256 = (bd_r >> 6) == (bd_c >> 6)
    ones_bd = jnp.where(bd256, 1.0, 0.0).astype(BF16)
    hm = (_iota((gh * c, RW_GROUP), 0) >> log2c) == (_iota((gh * c, RW_GROUP), 1) >> 6)
    hm_b = jnp.where(hm, 1.0, 0.0).astype(BF16)
    bdm = (_iota((gh * c, gh * c), 0) >> log2c) == (_iota((gh * c, gh * c), 1) >> log2c)
    bdm_b = jnp.where(bdm, 1.0, 0.0).astype(BF16)
    trow = _iota((c, gh * c), 0)
    tcol = _iota((c, gh * c), 1) & (c - 1)
    strict = tcol < trow
    incl = tcol <= trow
    e_r, e_c = _iota((RW_HEAD, RW_GROUP), 0), _iota((RW_HEAD, RW_GROUP), 1)
    expand = jnp.where((e_c & (RW_HEAD - 1)) == e_r, 1.0, 0.0).astype(BF16)
    x_r, x_c = _iota((RW_GROUP, RW_HEAD), 0), _iota((RW_GROUP, RW_HEAD), 1)
    extract = jnp.where((x_r & (RW_HEAD - 1)) == x_c, 1.0, 0.0).astype(BF16)

    chains = [(n, g) for n in range(nb) for g in range(n_groups)]

    @pl.when(t_idx == 0)
    def _init():
        for i, (n, g) in enumerate(chains):
            if zero_init:
                s_scr[i] = jnp.zeros((RW_GROUP, RW_GROUP), F32)
            else:
                rows = s0_ref[n, g * RW_GROUP:(g + 1) * RW_GROUP, :]
                tiled = _dot_exact_lhs(rows, expand)
                s_scr[i] = jnp.where(bd256, tiled, 0.0)

    tri = _chunk_tri(tile, c)
    for n in range(nb):
        l_scr[n] = _dot_exact_rhs(tri, wl_ref[n])

    def stack(x_b):
        return jnp.concatenate([x_b] * gh, axis=0) * hm_b

    def chunk_step(ci, carry):
        rows = pl.ds(pl.multiple_of(ci * c, c), c)
        ld = []
        for n, g in chains:
            lanes = slice(g * RW_GROUP, (g + 1) * RW_GROUP)
            ld.append(dict(r=r_ref[n, rows, lanes], wl=wl_ref[n, rows, lanes], k=k_ref[n, rows, lanes],
                           v=v_ref[n, rows, lanes], a=a_ref[n, rows, lanes], b=b_ref[n, rows, lanes],
                           lc=l_scr[n, rows, lanes]))
        for i, x in enumerate(ld):
            lc = x["lc"]
            x["lend"] = lc[c - 1:c, :]
            w_inv = jnp.exp(-lc)
            at = x["a"] * jnp.exp(lc - x["wl"])
            rt = x["r"] * jnp.exp(lc)
            x["ar"] = jnp.concatenate([at, rt], axis=0).astype(BF16)
            x["bts"] = stack((x["b"] * w_inv).astype(BF16))
            x["kts"] = stack((x["k"] * w_inv).astype(BF16))
            x["vs"] = stack(x["v"].astype(BF16))
            x["sg"] = s_scr[i]
        for x in ld:
            x["sb"] = _dot_nt(x["ar"], x["bts"])
            x["sk"] = _dot_nt(x["ar"], x["kts"])
            x["x2"] = _dot_nt(x["ar"], x["sg"].astype(BF16))
        for x in ld:
            x["p"] = jnp.where(strict, x["sb"][:c], 0.0).astype(BF16)
            x["arb"] = jnp.where(incl, x["sb"][c:], 0.0).astype(BF16)
            mak = jnp.where(strict, x["sk"][:c], 0.0).astype(BF16)
            x["ark"] = jnp.where(incl, x["sk"][c:], 0.0).astype(BF16)
            x["u"] = x["x2"][:c] + _dot(mak, x["vs"])
        for it in range(log2c):
            for x in ld:
                x["u"] = x["u"] + _dot(x["p"], stack(x["u"].astype(BF16)))
            if it < log2c - 1:
                for x in ld:
                    x["p"] = _dot(x["p"], jnp.concatenate([x["p"]] * gh, axis=0) * bdm_b).astype(BF16)
        for x, (n, g) in zip(ld, chains):
            y = x["x2"][c:] + _dot(x["arb"], stack(x["u"].astype(BF16))) + _dot(x["ark"], x["vs"])
            y_ref[n, rows, g * RW_GROUP:(g + 1) * RW_GROUP] = y
        for i, x in enumerate(ld):
            w_end = jnp.exp(x["lend"] - x["lc"])
            uv = jnp.concatenate([x["u"], x["v"]], axis=0).astype(BF16)
            bk = jnp.concatenate([x["b"] * w_end, x["k"] * w_end], axis=0).astype(BF16)
            s_scr[i] = x["sg"] * jnp.exp(x["lend"]) + jnp.where(bd256, _dot_tn(uv, bk), 0.0)
        return carry

    lax.fori_loop(0, tile // c, chunk_step, 0)

    inv_n = 1.0 / RW_HEAD
    for n, g in chains:
        lanes = slice(g * RW_GROUP, (g + 1) * RW_GROUP)
        y = y_ref[n, :, lanes]
        yc = y - _head_sums(y, ones_bd) * inv_n
        yn = yc * lax.rsqrt(_head_sums(yc * yc, ones_bd) * inv_n + RW_LNX_EPS) * lw_ref[:, lanes] + lb_ref[:, lanes]
        bonus = _head_sums(r_ref[n, :, lanes] * k_ref[n, :, lanes] * rk_ref[:, lanes], ones_bd) * v_ref[n, :, lanes]
        y_ref[n, :, lanes] = (yn + bonus) * g_ref[n, :, lanes]

    @pl.when(t_idx == pl.num_programs(1) - 1)
    def _final():
        for i, (n, g) in enumerate(chains):
            sg = jnp.where(bd256, s_scr[i], 0.0)
            st_ref[n, g * RW_GROUP:(g + 1) * RW_GROUP, :] = _dot_exact_lhs(sg, extract)


def _rwkv_scan(r, wl, k, v, a, b, g, s0, p, zero_init):
    bsz, t, d = r.shape
    c = min(CHUNK, t)
    tile = min(t, SCAN_ROWS)
    nb = RW_BATCH_PER_STEP
    assert t % tile == 0 and tile % c == 0 and c % SUBLANES == 0 and bsz % nb == 0
    s0r = s0.reshape(bsz, d, RW_HEAD)
    seq = pl.BlockSpec((nb, tile, d), lambda i, j: (i, j, 0))
    st = pl.BlockSpec((nb, d, RW_HEAD), lambda i, j: (i, 0, 0))
    vec = _resident((1, d))
    y, s_t = pl.pallas_call(
        functools.partial(_rwkv_body, chunk=c, tile=tile, nb=nb, zero_init=zero_init),
        grid=(bsz // nb, t // tile),
        in_specs=[seq] * 7 + [st, vec, vec, vec],
        out_specs=[seq, st],
        out_shape=[jax.ShapeDtypeStruct((bsz, t, d), F32), jax.ShapeDtypeStruct((bsz, d, RW_HEAD), F32)],
        scratch_shapes=[pltpu.VMEM((nb * (d // RW_GROUP), RW_GROUP, RW_GROUP), F32),
                        pltpu.VMEM((nb, tile, d), F32)],
        compiler_params=_params(),
        name="rwkv_scan",
    )(r, wl, k, v, a, b, g, s0r, p["r_k"].reshape(1, d), p["lnx_w"].reshape(1, d), p["lnx_b"].reshape(1, d))
    return y, s_t.reshape(bsz, RW_HEADS, RW_HEAD, RW_HEAD)


def _hgrn_pre_body(x_ref, wq_ref, wf_ref, wi_ref, wg_ref, llb_ref, l1m_ref, omlb_ref,
                   q_out, k_out, g_out, i_out, gate_out, *, rl):
    xb = x_ref[...].astype(BF16)
    qz = _dot(xb, wq_ref[...])
    q_out[...] = qz * jax.nn.sigmoid(qz) * HG_HEAD ** -0.5
    f = _dot(xb, wf_ref[...])
    la = llb_ref[...]
    lb = l1m_ref[...] - _softplus(-f)
    log_g = jnp.maximum(la, lb) + jnp.log(1.0 + jnp.exp(-jnp.abs(la - lb)))
    k = omlb_ref[...] * jax.nn.sigmoid(-f)
    if not rl.long and rl.valid < rl.t:
        live = (_iota(f.shape, 0) & (rl.t - 1)) < rl.valid
        log_g, k = jnp.where(live, log_g, 0.0), jnp.where(live, k, 0.0)
    g_out[...] = log_g
    k_out[...] = k
    i_out[...] = _dot(xb, wi_ref[...])
    gate_out[...] = _dot(xb, wg_ref[...])


def _hgrn_pre(x, lb, p, valid):
    bsz, t, d = x.shape
    rl = _Rows(bsz, t, valid, PRE_ROWS)
    seq = rl.seq_spec(d)
    vecs = [jnp.log(lb).reshape(1, d), jnp.log1p(-lb).reshape(1, d), (1.0 - lb).reshape(1, d)]
    weights = [p["wq"], p["wf"], p["wi"], p["wg"]] + vecs
    outs = pl.pallas_call(
        functools.partial(_hgrn_pre_body, rl=rl),
        grid=rl.grid,
        in_specs=[seq] + [_resident(w.shape) for w in weights],
        out_specs=[seq] * 5,
        out_shape=[jax.ShapeDtypeStruct(rl.view + (d,), F32)] * 5,
        compiler_params=_params(),
        name="hgrn_pre",
    )(rl.flat(x), *weights)
    return [rl.unflat(o) for o in outs]


def _gla_pre_body(x_ref, wq_ref, wk_ref, wv_ref, wg_ref, gk1_ref, gk2_ref, gkb_ref,
                  q_out, k_out, g_out, v_out, gate_out, *, rl):
    xb = x_ref[...].astype(BF16)
    q_out[...] = _dot(xb, wq_ref[...]) * GL_DK ** -0.5
    k = _dot(xb, wk_ref[...])
    gk = _dot(_dot(xb, gk1_ref[...]).astype(BF16), gk2_ref[...]) + gkb_ref[...]
    log_g = -_softplus(-gk) * (1.0 / GL_GATE_NORM)
    if not rl.long and rl.valid < rl.t:
        live = (_iota(k.shape, 0) & (rl.t - 1)) < rl.valid
        log_g, k = jnp.where(live, log_g, 0.0), jnp.where(live, k, 0.0)
    g_out[...] = log_g
    k_out[...] = k
    v_out[...] = _dot(xb, wv_ref[...])
    gate_out[...] = _dot(xb, wg_ref[...])


def _gla_pre(x, p, valid):
    bsz, t, d = x.shape
    dkk = GL_HEADS * GL_DK
    rl = _Rows(bsz, t, valid, PRE_ROWS)
    rank = p["gk1"].shape[1]
    gk1 = jnp.pad(p["gk1"], ((0, 0), (0, LANES - rank)))
    gk2 = jnp.pad(p["gk2"], ((0, LANES - rank), (0, 0)))
    weights = [p["wq"], p["wk"], p["wv"], p["wg"], gk1, gk2, p["gk_b"].reshape(1, dkk)]
    kseq, vseq = rl.seq_spec(dkk), rl.seq_spec(d)
    kshape, vshape = jax.ShapeDtypeStruct(rl.view + (dkk,), F32), jax.ShapeDtypeStruct(rl.view + (d,), F32)
    outs = pl.pallas_call(
        functools.partial(_gla_pre_body, rl=rl),
        grid=rl.grid,
        in_specs=[vseq] + [_resident(w.shape) for w in weights],
        out_specs=[kseq, kseq, kseq, vseq, vseq],
        out_shape=[kshape, kshape, kshape, vshape, vshape],
        compiler_params=_params(),
        name="gla_pre",
    )(rl.flat(x), *weights)
    return [rl.unflat(o) for o in outs]


def _gla_body(q_ref, k_ref, g_ref, v_ref, gate_ref, s0_ref, gain_ref, o_ref, st_ref, s_scr, l_scr,
              *, chunk, tile, heads, dk, dv, zero_init):
    c = chunk
    ref_row = (c - 1) // 2
    t_idx = pl.program_id(1)
    incl = _iota((c, c), 1) <= _iota((c, c), 0)

    @pl.when(t_idx == 0)
    def _init():
        for h in range(heads):
            s_scr[h] = jnp.zeros((dv, dk), F32) if zero_init else s0_ref[h].T

    l_scr[...] = _dot_exact_rhs(_chunk_tri(tile, c), g_ref[...])

    def chunk_step(ci, carry):
        rows = pl.ds(pl.multiple_of(ci * c, c), c)
        ld = []
        for h in range(heads):
            kl = slice(h * dk, (h + 1) * dk)
            vl = slice(h * dv, (h + 1) * dv)
            q, k, v = q_ref[rows, kl], k_ref[rows, kl], v_ref[rows, vl]
            bc = l_scr[rows, kl]
            b_ref = bc[ref_row:ref_row + 1, :]
            b_last = bc[c - 1:c, :]
            ld.append(dict(qi=(q * jnp.exp(bc - b_ref)).astype(BF16), ki=(k * jnp.exp(b_ref - bc)).astype(BF16),
                           qd=(q * jnp.exp(bc)).astype(BF16), kd=(k * jnp.exp(b_last - bc)).astype(BF16),
                           v=v.astype(BF16), w_last=jnp.exp(b_last), st=s_scr[h]))
        for x in ld:
            x["att"] = _dot_nt(x["qi"], x["ki"])
            x["os"] = _dot_nt(x["qd"], x["st"].astype(BF16))
            x["ds"] = _dot_tn(x["v"], x["kd"])
        for h, x in enumerate(ld):
            o_ref[rows, h * dv:(h + 1) * dv] = x["os"] + _dot(jnp.where(incl, x["att"], 0.0).astype(BF16), x["v"])
            s_scr[h] = x["st"] * x["w_last"] + x["ds"]
        return carry

    lax.fori_loop(0, tile // c, chunk_step, 0)

    for h in range(heads):
        vl = slice(h * dv, (h + 1) * dv)
        o = o_ref[:, vl]
        gate = gate_ref[:, vl]
        on = o * lax.rsqrt(jnp.mean(o * o, -1, keepdims=True) + RMS_EPS) * gain_ref[...]
        o_ref[:, vl] = on * (gate * jax.nn.sigmoid(gate))

    @pl.when(t_idx == pl.num_programs(1) - 1)
    def _final():
        for h in range(heads):
            st_ref[h] = s_scr[h].T


def _gla_scan(q, k, g, v, gate, s0, gain, heads, dk, dv, zero_init):
    bsz, t, _ = q.shape
    c = min(CHUNK, t)
    tile = min(t, SCAN_ROWS)
    assert t % tile == 0 and tile % c == 0 and c % SUBLANES == 0
    kseq = pl.BlockSpec((None, tile, heads * dk), lambda i, j: (i, j, 0))
    vseq = pl.BlockSpec((None, tile, heads * dv), lambda i, j: (i, j, 0))
    st = pl.BlockSpec((None, heads, dk, dv), lambda i, j: (i, 0, 0, 0))
    return pl.pallas_call(
        functools.partial(_gla_body, chunk=c, tile=tile, heads=heads, dk=dk, dv=dv, zero_init=zero_init),
        grid=(bsz, t // tile),
        in_specs=[kseq, kseq, kseq, vseq, vseq, st, _resident((1, dv))],
        out_specs=[vseq, st],
        out_shape=[jax.ShapeDtypeStruct((bsz, t, heads * dv), F32),
                   jax.ShapeDtypeStruct((bsz, heads, dk, dv), F32)],
        scratch_shapes=[pltpu.VMEM((heads, dv, dk), F32), pltpu.VMEM((tile, heads * dk), F32)],
        compiler_params=_params(),
        name="gla_scan",
    )(q, k, g, v, gate, s0, gain.reshape(1, dv))


def _cf_body(x_ref, o_ref, fill_ref, wo_ref, l1w_ref, l1b_ref, wu_ref, wg_ref, wd_ref, cw_ref, cb_ref,
             l2w_ref, l2b_ref, out_ref, z_ref, h_scr, carry_scr, *, rl):
    n_state = CONV_W - 1
    t_idx = pl.program_id(1)
    rows = rl.rows
    x = x_ref[...]
    x1 = _ln_rows(DN_ALPHA * x + _dot(o_ref[...].astype(BF16), wo_ref[...]), l1w_ref[...], l1b_ref[...])
    x1b = x1.astype(BF16)
    if rl.long:
        @pl.when(t_idx == 0)
        def _load_state():
            carry_scr[...] = fill_ref[...]

    for j in range(wu_ref.shape[1] // FFN_COLS):
        cs = slice(j * FFN_COLS, (j + 1) * FFN_COLS)
        zc = _dot(x1b, wg_ref[:, cs])
        uc = _dot(x1b, wu_ref[:, cs])
        carry_c = carry_scr.at[:, cs] if rl.long else None
        fill_c = None if rl.long else fill_ref.at[:, cs]
        zp1 = _prev_rows(zc, 1, rl, t_idx, carry_c, fill_c, n_state)
        zp2 = _prev_rows(zc, 2, rl, t_idx, carry_c, fill_c, n_state)
        if rl.long:
            carry_scr[:, cs] = zc[rows - n_state:rows, :]
        else:
            z_ref[:, cs] = zc
        cw = cw_ref[:, cs]
        pre = cb_ref[:, cs] + cw[0:1] * zp2 + cw[1:2] * zp1 + cw[2:3] * zc
        h_scr[:, cs] = (pre * jax.nn.sigmoid(pre) * uc).astype(BF16)

    out_ref[...] = _ln_rows(DN_ALPHA * x1 + _dot(h_scr[...], wd_ref[...]), l2w_ref[...], l2b_ref[...])
    if rl.long:
        @pl.when(t_idx == pl.num_programs(1) - 1)
        def _store_state():
            z_ref[...] = carry_scr[...]


def _proj_ffn(x, o, buf, wo, ln1, p, ln2, valid):
    bsz, t, d = x.shape
    f = p["wu"].shape[1]
    n_state = CONV_W - 1
    rl = _Rows(bsz, t, valid, FFN_ROWS if t >= FFN_ROWS else FFN_ROWS // 2)
    assert f % FFN_COLS == 0 and valid >= n_state
    seq = rl.seq_spec(d)
    vec = lambda n: _resident((1, n))
    if rl.long:
        z_spec, z_shape = pl.BlockSpec((None, n_state, f), lambda i, j: (i, 0, 0)), (bsz, n_state, f)
    else:
        z_spec, z_shape = rl.seq_spec(f), rl.view + (f,)
    out, z = pl.pallas_call(
        functools.partial(_cf_body, rl=rl),
        grid=rl.grid,
        in_specs=[seq, seq, rl.state_spec(n_state, f),
                  _resident((d, d)), vec(d), vec(d), _resident((d, f)), _resident((d, f)), _resident((f, d)),
                  _resident((CONV_W, f)), vec(f), vec(d), vec(d)],
        out_specs=[seq, z_spec],
        out_shape=[jax.ShapeDtypeStruct(rl.view + (d,), F32), jax.ShapeDtypeStruct(z_shape, F32)],
        scratch_shapes=[pltpu.VMEM((rl.rows, f), BF16), pltpu.VMEM((n_state, f), F32)],
        compiler_params=_params(),
        name="proj_ffn",
    )(rl.flat(x), rl.flat(o), rl.state_rows(buf), wo, ln1[0].reshape(1, d), ln1[1].reshape(1, d),
      p["wu"], p["wg"], p["wd"], p["conv_w"], p["conv_b"].reshape(1, f), ln2[0].reshape(1, d), ln2[1].reshape(1, d))
    if rl.long:
        return out, z
    return rl.unflat(out), rl.unflat(z)[:, valid - n_state:valid]


def _run_trunk(x, valid, st_rw, st_shift, st_hg, st_gl, st_conv, rw, rw_vmix, hg, lower_bounds, gl, ffn, ln, zero_init):
    new_rw, new_shift, new_hg, new_gl, new_conv = [], [], [], [], []
    v_first = None
    for i in range(DEPTH):
        j = i // N_MIXERS
        kind = i % N_MIXERS
        if kind == 0:
            p = {n: q[j] for n, q in rw.items()}
            vm = None if j == 0 else tuple(q[j - 1] for q in rw_vmix)
            r, wl, k, v, a, b, g = _rwkv_pre(x, st_shift[j], v_first, vm, p, valid)
            if vm is None:
                v_first = v
            o, s_t = _rwkv_scan(r, wl, k, v, a, b, g, st_rw[j], p, zero_init)
            new_rw.append(s_t)
            new_shift.append(x[:, valid - 1])
        elif kind == 1:
            p = {n: q[j] for n, q in hg.items()}
            q_, k, log_g, val, gate = _hgrn_pre(x, lower_bounds[i], p, valid)
            o, s_t = _gla_scan(q_, k, log_g, val, gate, st_hg[j], p["norm_w"], HG_HEADS, HG_HEAD, HG_HEAD, zero_init)
            new_hg.append(s_t)
        else:
            p = {n: q[j] for n, q in gl.items()}
            q_, k, log_g, val, gate = _gla_pre(x, p, valid)
            o, s_t = _gla_scan(q_, k, log_g, val, gate, st_gl[j], p["norm_w"], GL_HEADS, GL_DK, GL_DV, zero_init)
            new_gl.append(s_t)
        x, buf = _proj_ffn(x, o, st_conv[i], p["wo"], (ln["ln1_w"][i], ln["ln1_b"][i]),
                           {n: q[i] for n, q in ffn.items()}, (ln["ln2_w"][i], ln["ln2_b"][i]), valid)
        new_conv.append(buf)
    return (x[:, :valid], jnp.stack(new_rw), jnp.stack(new_shift), jnp.stack(new_hg), jnp.stack(new_gl),
            jnp.stack(new_conv))


def kernel(x_prompt, x_sample, state_rwkv, state_rwkv_shift, state_hgrn, state_gla, state_ffn_conv, rw_mix, rw_wr, rw_wk, rw_wv, rw_wo, rw_w0, rw_w1, rw_w2, rw_a0, rw_a1, rw_a2, rw_g1, rw_g2, rw_k_k, rw_k_a, rw_r_k, rw_lnx_w, rw_lnx_b, rw_v0, rw_v1, rw_v2, hg_wq, hg_wf, hg_wi, hg_wg, hg_wo, hg_norm_w, hg_lb_param, gl_wq, gl_wk, gl_wv, gl_wg, gl_gk1, gl_gk2, gl_gk_b, gl_wo, gl_norm_w, ffn_wu, ffn_wg, ffn_conv_w, ffn_conv_b, ffn_wd, ln1_w, ln1_b, ln2_w, ln2_b):
    bf = lambda w: w.astype(BF16)
    rw = dict(mix=rw_mix, wr=bf(rw_wr), wk=bf(rw_wk), wv=bf(rw_wv), wo=bf(rw_wo), w0=rw_w0, w1=bf(rw_w1),
              w2=bf(rw_w2), a0=rw_a0, a1=bf(rw_a1), a2=bf(rw_a2), g1=bf(rw_g1), g2=bf(rw_g2), k_k=rw_k_k,
              k_a=rw_k_a, r_k=rw_r_k, lnx_w=rw_lnx_w, lnx_b=rw_lnx_b)
    rw_vmix = (rw_v0, bf(rw_v1), bf(rw_v2))
    hg = dict(wq=bf(hg_wq), wf=bf(hg_wf), wi=bf(hg_wi), wg=bf(hg_wg), wo=bf(hg_wo), norm_w=hg_norm_w)
    gl = dict(wq=bf(gl_wq), wk=bf(gl_wk), wv=bf(gl_wv), wg=bf(gl_wg), gk1=bf(gl_gk1), gk2=bf(gl_gk2),
              gk_b=gl_gk_b, wo=bf(gl_wo), norm_w=gl_norm_w)
    ffn = dict(wu=bf(ffn_wu), wg=bf(ffn_wg), conv_w=ffn_conv_w, conv_b=ffn_conv_b, wd=bf(ffn_wd))
    ln = dict(ln1_w=ln1_w, ln1_b=ln1_b, ln2_w=ln2_w, ln2_b=ln2_b)
    lb_soft = jax.nn.softmax(hg_lb_param, axis=0)
    lower_bounds = jnp.cumsum(lb_soft, axis=0) - lb_soft[0]

    nb = x_prompt.shape[0]
    zeros = lambda s: jnp.zeros((s.shape[0], nb) + s.shape[2:], s.dtype)
    p_out = _run_trunk(x_prompt, x_prompt.shape[1], zeros(state_rwkv), zeros(state_rwkv_shift), zeros(state_hgrn),
                       zeros(state_gla), zeros(state_ffn_conv), rw, rw_vmix, hg, lower_bounds, gl, ffn, ln, True)
    t_s = x_sample.shape[1]
    t_pad = max(SUBLANES, 1 << (t_s - 1).bit_length())
    xs = jnp.pad(x_sample, ((0, 0), (0, t_pad - t_s), (0, 0)))
    s_out = _run_trunk(xs, t_s, state_rwkv, state_rwkv_shift, state_hgrn, state_gla, state_ffn_conv,
                       rw, rw_vmix, hg, lower_bounds, gl, ffn, ln, False)
    return (p_out[0], s_out[0]) + tuple(p_out[1:]) + tuple(s_out[1:])
```

```python
import functools
import math

import jax
import jax.numpy as jnp
from jax import lax
from jax.experimental import pallas as pl
from jax.experimental.pallas import tpu as pltpu

D_MODEL = 1024
DEPTH = 4
N_MIXERS = 3
RW_HEAD = 64
RW_HEADS = D_MODEL // RW_HEAD
RW_LNX_EPS = 64e-5
HG_HEAD = 128
HG_HEADS = D_MODEL // HG_HEAD
GL_HEADS = 4
GL_DK = (D_MODEL // 2) // GL_HEADS
GL_DV = D_MODEL // GL_HEADS
GL_GATE_NORM = 16.0
CONV_W = 3
LN_EPS = 1e-5
RMS_EPS = 1e-5
DN_ALPHA = (2 * DEPTH) ** 0.25

LANES = 128
SUBLANES = 8
RW_GROUP = 256
RW_GROUP_HEADS = RW_GROUP // RW_HEAD
CHUNK = 32
RW_CHUNK = 64
RW_BATCH_PER_STEP = 2
RW_BATCH_PER_STEP_SHORT = 4
GLA_BATCH_PER_STEP = 2
SCAN_ROWS = 256
PRE_ROWS = 256
FFN_ROWS = 512
FFN_COLS = 256
VMEM_LIMIT = 56 * 1024 * 1024

BF16 = jnp.bfloat16
F32 = jnp.float32
ACT = BF16


def _dot(a, b):
    return jnp.dot(a, b, preferred_element_type=F32)


def _dot_nt(a, b):
    return lax.dot_general(a, b, (((1,), (1,)), ((), ())), preferred_element_type=F32)


def _dot_tn(a, b):
    return lax.dot_general(a, b, (((0,), (0,)), ((), ())), preferred_element_type=F32)


def _dot_hilo_rhs(sel, x):
    hi = x.astype(BF16)
    lo = (x - hi.astype(F32)).astype(BF16)
    return _dot(sel, hi) + _dot(sel, lo)


def _head_sums(x, ones_bd):
    hi = x.astype(BF16)
    lo = (x - hi.astype(F32)).astype(BF16)
    return _dot(hi, ones_bd) + _dot(lo, ones_bd)


def _iota(shape, dim):
    return lax.broadcasted_iota(jnp.int32, shape, dim)


def _chunk_tri(n, c):
    row, col = _iota((n, n), 0), _iota((n, n), 1)
    sh = int(math.log2(c))
    keep = (col <= row) & ((row >> sh) == (col >> sh))
    return jnp.where(keep, 1.0, 0.0).astype(BF16)


def _softplus(t):
    return jnp.maximum(t, 0.0) + jnp.log(1.0 + jnp.exp(-jnp.abs(t)))


def _ln_rows(x, w, b):
    mu = jnp.mean(x, -1, keepdims=True)
    xc = x - mu
    var = jnp.mean(xc * xc, -1, keepdims=True)
    return xc * lax.rsqrt(var + LN_EPS) * w + b


def _resident(shape):
    return pl.BlockSpec(shape, lambda i, j: (0,) * len(shape), pipeline_mode=pl.Buffered(1))


def _params():
    return pltpu.CompilerParams(dimension_semantics=("arbitrary", "arbitrary"), vmem_limit_bytes=VMEM_LIMIT)


class _Rows:
    def __init__(self, bsz, t, valid, max_rows):
        self.bsz, self.t, self.valid = bsz, t, valid
        self.long = t >= max_rows
        if self.long:
            assert t % max_rows == 0 and valid == t
            self.rows, self.grid, self.view = max_rows, (bsz, t // max_rows), (bsz, t)
        else:
            total = bsz * t
            self.rows = min(total, max_rows)
            assert t & (t - 1) == 0 and self.rows % t == 0 and total % self.rows == 0
            self.grid, self.view = (1, total // self.rows), (1, total)

    def flat(self, x):
        return x.reshape(self.view + x.shape[2:])

    def unflat(self, x):
        return x.reshape((self.bsz, self.t) + x.shape[2:])

    def seq_spec(self, width):
        return pl.BlockSpec((None, self.rows, width), lambda i, j: (i, j, 0))

    def state_rows(self, state):
        if self.long:
            return state
        return self.flat(jnp.pad(state, ((0, 0), (0, self.t - state.shape[1]), (0, 0))))

    def state_spec(self, n, width):
        if self.long:
            return pl.BlockSpec((None, n, width), lambda i, j: (i, 0, 0))
        return self.seq_spec(width)


def _prev_rows(x, shift, rl, t_idx, carry_scr, fill_ref, n_prev):
    rows, width = x.shape
    row = _iota((rows, width), 0)
    rolled = pltpu.roll(x, shift, 0)
    if rl.long:
        out = rolled
        for i in range(shift):
            src = n_prev - shift + i
            out = jnp.where(row == i, carry_scr[src:src + 1, :], out)
        return out
    fill = fill_ref[...]
    if n_prev != shift:
        fill = pltpu.roll(fill, rows - (n_prev - shift), 0)
    return jnp.where((row & (rl.t - 1)) >= shift, rolled, fill)


def _rwkv_pre_body(*refs, rl, has_vmix):
    it = iter(refs)
    x_ref, fill_ref = next(it), next(it)
    vf_ref = next(it) if has_vmix else None
    mix_ref, wr_ref, wk_ref, wv_ref, w0_ref, w1_ref, w2_ref = (next(it) for _ in range(7))
    a0_ref, a1_ref, a2_ref, g1_ref, g2_ref, kk_ref, ka_ref = (next(it) for _ in range(7))
    v0_ref, v1_ref, v2_ref = (next(it) for _ in range(3)) if has_vmix else (None, None, None)
    r_out, wl_out, k_out, v_out, a_out, b_out, g_out, carry_scr = (next(it) for _ in range(8))

    t_idx = pl.program_id(1)
    rows = rl.rows
    x = x_ref[...]
    if rl.long:
        @pl.when(t_idx == 0)
        def _load_state():
            carry_scr[0:1, :] = fill_ref[...]
    x_prev = _prev_rows(x, 1, rl, t_idx, carry_scr, fill_ref, 1)
    if rl.long:
        carry_scr[0:1, :] = x[rows - 1:rows, :]
    xx = x_prev - x
    mixed = [(x + xx * mix_ref[j:j + 1, :]).astype(BF16) for j in range(6)]
    xr, xw, xk, xv, xa, xg = mixed
    r = _dot(xr, wr_ref[...])
    k = _dot(xk, wk_ref[...])
    v = _dot(xv, wv_ref[...])
    wl = -jnp.exp(-_softplus(-(w0_ref[...] + _dot(jnp.tanh(_dot(xw, w1_ref[...])).astype(BF16), w2_ref[...]))) - 0.5)
    a = jax.nn.sigmoid(a0_ref[...] + _dot(_dot(xa, a1_ref[...]).astype(BF16), a2_ref[...]))
    if has_vmix:
        gate = jax.nn.sigmoid(v0_ref[...] + _dot(_dot(xv, v1_ref[...]).astype(BF16), v2_ref[...]))
        v = v + (vf_ref[...].astype(F32) - v) * gate
    g_out[...] = _dot(jax.nn.sigmoid(_dot(xg, g1_ref[...])).astype(BF16), g2_ref[...]).astype(g_out.dtype)

    kk = k * kk_ref[...]
    k = k * (1.0 + (a - 1.0) * ka_ref[...])
    if not rl.long and rl.valid < rl.t:
        live = (_iota((rows, D_MODEL), 0) & (rl.t - 1)) < rl.valid
        wl, k, v, kk = (jnp.where(live, z, 0.0) for z in (wl, k, v, kk))
    r_out[...] = r.astype(r_out.dtype)
    wl_out[...] = wl
    k_out[...] = k.astype(k_out.dtype)
    v_out[...] = v.astype(v_out.dtype)
    bd = (_iota((RW_GROUP, RW_GROUP), 0) >> 6) == (_iota((RW_GROUP, RW_GROUP), 1) >> 6)
    ones_bd = jnp.where(bd, 1.0, 0.0).astype(BF16)
    for g in range(D_MODEL // RW_GROUP):
        lanes = slice(g * RW_GROUP, (g + 1) * RW_GROUP)
        kkg = kk[:, lanes]
        kkn = kkg / jnp.maximum(jnp.sqrt(_head_sums(kkg * kkg, ones_bd)), 1e-12)
        a_out[:, lanes] = (-kkn).astype(a_out.dtype)
        b_out[:, lanes] = (kkn * a[:, lanes]).astype(b_out.dtype)


def _rwkv_pre(x, shift, v_first, vmix, p, valid):
    bsz, t, d = x.shape
    rl = _Rows(bsz, t, valid, PRE_ROWS)
    has_vmix = vmix is not None
    seq = rl.seq_spec(d)
    vec = lambda z: z.reshape(1, d)
    args = [rl.flat(x), rl.state_rows(shift[:, None, :])]
    specs = [seq, rl.state_spec(1, d)]
    if has_vmix:
        args.append(rl.flat(v_first))
        specs.append(seq)
    weights = [p["mix"], p["wr"], p["wk"], p["wv"], vec(p["w0"]), p["w1"], p["w2"], vec(p["a0"]), p["a1"], p["a2"],
               p["g1"], p["g2"], vec(p["k_k"]), vec(p["k_a"])]
    if has_vmix:
        weights += [vec(vmix[0]), vmix[1], vmix[2]]
    args += weights
    specs += [_resident(w.shape) for w in weights]
    outs = pl.pallas_call(
        functools.partial(_rwkv_pre_body, rl=rl, has_vmix=has_vmix),
        grid=rl.grid,
        in_specs=specs,
        out_specs=[seq] * 7,
        out_shape=[jax.ShapeDtypeStruct(rl.view + (d,), F32 if i == 1 else ACT) for i in range(7)],
        scratch_shapes=[pltpu.VMEM((SUBLANES, d), F32)],
        compiler_params=_params(),
        name="rwkv_pre",
    )(*args)
    return [rl.unflat(o) for o in outs]


def _rwkv_body(r_ref, wl_ref, k_ref, v_ref, a_ref, b_ref, g_ref, s0_ref, rk_ref, lw_ref, lb_ref,
               y_ref, st_ref, s_scr, l_scr, y_scr, *, chunk, tile, nb, zero_init):
    c = chunk
    gh = RW_GROUP_HEADS
    n_groups = D_MODEL // RW_GROUP
    t_idx = pl.program_id(1)
    log2c = int(math.log2(c))

    bd_r, bd_c = _iota((RW_GROUP, RW_GROUP), 0), _iota((RW_GROUP, RW_GROUP), 1)
    bd256 = (bd_r >> 6) == (bd_c >> 6)
    ones_bd = jnp.where(bd256, 1.0, 0.0).astype(BF16)
    hm = (_iota((gh * c, RW_GROUP), 0) >> log2c) == (_iota((gh * c, RW_GROUP), 1) >> 6)
    hm_b = jnp.where(hm, 1.0, 0.0).astype(BF16)
    bdm = (_iota((gh * c, gh * c), 0) >> log2c) == (_iota((gh * c, gh * c), 1) >> log2c)
    bdm_b = jnp.where(bdm, 1.0, 0.0).astype(BF16)
    trow = _iota((c, gh * c), 0)
    tcol = _iota((c, gh * c), 1) & (c - 1)
    strict = tcol < trow
    incl = tcol <= trow
    chains = [(n, g) for n in range(nb) for g in range(n_groups)]

    def head_block(i, n, g, h):
        blk = slice(h * RW_HEAD, (h + 1) * RW_HEAD)
        return (i, blk, blk), (n, slice(g * RW_GROUP + h * RW_HEAD, g * RW_GROUP + (h + 1) * RW_HEAD), slice(None))

    @pl.when(t_idx == 0)
    def _init():
        for i, (n, g) in enumerate(chains):
            s_scr[i] = jnp.zeros((RW_GROUP, RW_GROUP), F32)
            if not zero_init:
                for h in range(gh):
                    dst, src = head_block(i, n, g, h)
                    s_scr[dst] = s0_ref[src]

    tri = _chunk_tri(tile, c)
    for n in range(nb):
        l_scr[n] = _dot_hilo_rhs(tri, wl_ref[n])

    def stack(x_b):
        return jnp.concatenate([x_b] * gh, axis=0) * hm_b

    def chunk_step(ci, carry):
        rows = pl.ds(pl.multiple_of(ci * c, c), c)
        ld = []
        for n, g in chains:
            lanes = slice(g * RW_GROUP, (g + 1) * RW_GROUP)
            f32 = lambda ref: ref[n, rows, lanes].astype(F32)
            ld.append(dict(r=f32(r_ref), wl=wl_ref[n, rows, lanes], k=f32(k_ref), v=v_ref[n, rows, lanes].astype(BF16),
                           a=f32(a_ref), b=f32(b_ref), lc=l_scr[n, rows, lanes]))
        for i, x in enumerate(ld):
            lc = x["lc"]
            x["lend"] = lc[c - 1:c, :]
            w_inv = jnp.exp(-lc)
            at = x["a"] * jnp.exp(lc - x["wl"])
            rt = x["r"] * jnp.exp(lc)
            x["ar"] = jnp.concatenate([at, rt], axis=0).astype(BF16)
            x["bts"] = stack((x["b"] * w_inv).astype(BF16))
            x["kts"] = stack((x["k"] * w_inv).astype(BF16))
            x["vs"] = stack(x["v"])
            x["sg"] = s_scr[i]
        for x in ld:
            x["sb"] = _dot_nt(x["ar"], x["bts"])
            x["sk"] = _dot_nt(x["ar"], x["kts"])
            x["x2"] = _dot_nt(x["ar"], x["sg"].astype(BF16))
        for x in ld:
            x["p"] = jnp.where(strict, x["sb"][:c], 0.0).astype(BF16)
            x["arb"] = jnp.where(incl, x["sb"][c:], 0.0).astype(BF16)
            mak = jnp.where(strict, x["sk"][:c], 0.0).astype(BF16)
            x["ark"] = jnp.where(incl, x["sk"][c:], 0.0).astype(BF16)
            x["u"] = x["x2"][:c] + _dot(mak, x["vs"])
        for it in range(log2c):
            for x in ld:
                x["u"] = x["u"] + _dot(x["p"], stack(x["u"].astype(BF16)))
            if it < log2c - 1:
                for x in ld:
                    x["p"] = _dot(x["p"], jnp.concatenate([x["p"]] * gh, axis=0) * bdm_b).astype(BF16)
        for x, (n, g) in zip(ld, chains):
            y = x["x2"][c:] + _dot(x["arb"], stack(x["u"].astype(BF16))) + _dot(x["ark"], x["vs"])
            y_scr[n, rows, g * RW_GROUP:(g + 1) * RW_GROUP] = y
        for i, x in enumerate(ld):
            w_end = jnp.exp(x["lend"] - x["lc"])
            uv = jnp.concatenate([x["u"].astype(BF16), x["v"]], axis=0)
            bk = jnp.concatenate([x["b"] * w_end, x["k"] * w_end], axis=0).astype(BF16)
            s_scr[i] = x["sg"] * jnp.exp(x["lend"]) + jnp.where(bd256, _dot_tn(uv, bk), 0.0)
        return carry

    lax.fori_loop(0, tile // c, chunk_step, 0)

    inv_n = 1.0 / RW_HEAD
    for n, g in chains:
        lanes = slice(g * RW_GROUP, (g + 1) * RW_GROUP)
        y = y_scr[n, :, lanes]
        f32 = lambda ref: ref[n, :, lanes].astype(F32)
        yc = y - _head_sums(y, ones_bd) * inv_n
        yn = yc * lax.rsqrt(_head_sums(yc * yc, ones_bd) * inv_n + RW_LNX_EPS) * lw_ref[:, lanes] + lb_ref[:, lanes]
        bonus = _head_sums(f32(r_ref) * f32(k_ref) * rk_ref[:, lanes], ones_bd) * f32(v_ref)
        y_ref[n, :, lanes] = ((yn + bonus) * f32(g_ref)).astype(y_ref.dtype)

    @pl.when(t_idx == pl.num_programs(1) - 1)
    def _final():
        for i, (n, g) in enumerate(chains):
            for h in range(gh):
                src, dst = head_block(i, n, g, h)
                st_ref[dst] = s_scr[src]


def _rwkv_scan(r, wl, k, v, a, b, g, s0, p, zero_init):
    bsz, t, d = r.shape
    c = min(RW_CHUNK, t)
    tile = min(t, SCAN_ROWS)
    nb = RW_BATCH_PER_STEP if t > c else RW_BATCH_PER_STEP_SHORT
    assert t % tile == 0 and tile % c == 0 and c % SUBLANES == 0 and bsz % nb == 0
    s0r = s0.reshape(bsz, d, RW_HEAD)
    seq = pl.BlockSpec((nb, tile, d), lambda i, j: (i, j, 0))
    st = pl.BlockSpec((nb, d, RW_HEAD), lambda i, j: (i, 0, 0))
    vec = _resident((1, d))
    y, s_t = pl.pallas_call(
        functools.partial(_rwkv_body, chunk=c, tile=tile, nb=nb, zero_init=zero_init),
        grid=(bsz // nb, t // tile),
        in_specs=[seq] * 7 + [st, vec, vec, vec],
        out_specs=[seq, st],
        out_shape=[jax.ShapeDtypeStruct((bsz, t, d), ACT), jax.ShapeDtypeStruct((bsz, d, RW_HEAD), F32)],
        scratch_shapes=[pltpu.VMEM((nb * (d // RW_GROUP), RW_GROUP, RW_GROUP), F32),
                        pltpu.VMEM((nb, tile, d), F32), pltpu.VMEM((nb, tile, d), F32)],
        compiler_params=_params(),
        name="rwkv_scan",
    )(r, wl, k, v, a, b, g, s0r, p["r_k"].reshape(1, d), p["lnx_w"].reshape(1, d), p["lnx_b"].reshape(1, d))
    return y, s_t.reshape(bsz, RW_HEADS, RW_HEAD, RW_HEAD)


def _hgrn_pre_body(x_ref, wq_ref, wf_ref, wi_ref, wg_ref, llb_ref, l1m_ref, omlb_ref,
                   q_out, k_out, g_out, i_out, gate_out, *, rl):
    xb = x_ref[...].astype(BF16)
    qz = _dot(xb, wq_ref[...])
    q_out[...] = (qz * jax.nn.sigmoid(qz) * HG_HEAD ** -0.5).astype(q_out.dtype)
    f = _dot(xb, wf_ref[...])
    la = llb_ref[...]
    lb = l1m_ref[...] - _softplus(-f)
    log_g = jnp.maximum(la, lb) + jnp.log(1.0 + jnp.exp(-jnp.abs(la - lb)))
    k = omlb_ref[...] * jax.nn.sigmoid(-f)
    if not rl.long and rl.valid < rl.t:
        live = (_iota(f.shape, 0) & (rl.t - 1)) < rl.valid
        log_g, k = jnp.where(live, log_g, 0.0), jnp.where(live, k, 0.0)
    g_out[...] = log_g
    k_out[...] = k.astype(k_out.dtype)
    i_out[...] = _dot(xb, wi_ref[...]).astype(i_out.dtype)
    gate_out[...] = _dot(xb, wg_ref[...]).astype(gate_out.dtype)


def _hgrn_pre(x, lb, p, valid):
    bsz, t, d = x.shape
    rl = _Rows(bsz, t, valid, PRE_ROWS)
    seq = rl.seq_spec(d)
    vecs = [jnp.log(lb).reshape(1, d), jnp.log1p(-lb).reshape(1, d), (1.0 - lb).reshape(1, d)]
    weights = [p["wq"], p["wf"], p["wi"], p["wg"]] + vecs
    outs = pl.pallas_call(
        functools.partial(_hgrn_pre_body, rl=rl),
        grid=rl.grid,
        in_specs=[seq] + [_resident(w.shape) for w in weights],
        out_specs=[seq] * 5,
        out_shape=[jax.ShapeDtypeStruct(rl.view + (d,), F32 if i == 2 else ACT) for i in range(5)],
        compiler_params=_params(),
        name="hgrn_pre",
    )(rl.flat(x), *weights)
    return [rl.unflat(o) for o in outs]


def _gla_pre_body(x_ref, wq_ref, wk_ref, wv_ref, wg_ref, gk1_ref, gk2_ref, gkb_ref,
                  q_out, k_out, g_out, v_out, gate_out, *, rl):
    xb = x_ref[...].astype(BF16)
    q_out[...] = (_dot(xb, wq_ref[...]) * GL_DK ** -0.5).astype(q_out.dtype)
    k = _dot(xb, wk_ref[...])
    gk = _dot(_dot(xb, gk1_ref[...]).astype(BF16), gk2_ref[...]) + gkb_ref[...]
    log_g = -_softplus(-gk) * (1.0 / GL_GATE_NORM)
    if not rl.long and rl.valid < rl.t:
        live = (_iota(k.shape, 0) & (rl.t - 1)) < rl.valid
        log_g, k = jnp.where(live, log_g, 0.0), jnp.where(live, k, 0.0)
    g_out[...] = log_g
    k_out[...] = k.astype(k_out.dtype)
    v_out[...] = _dot(xb, wv_ref[...]).astype(v_out.dtype)
    gate_out[...] = _dot(xb, wg_ref[...]).astype(gate_out.dtype)


def _gla_pre(x, p, valid):
    bsz, t, d = x.shape
    dkk = GL_HEADS * GL_DK
    rl = _Rows(bsz, t, valid, PRE_ROWS)
    rank = p["gk1"].shape[1]
    gk1 = jnp.pad(p["gk1"], ((0, 0), (0, LANES - rank)))
    gk2 = jnp.pad(p["gk2"], ((0, LANES - rank), (0, 0)))
    weights = [p["wq"], p["wk"], p["wv"], p["wg"], gk1, gk2, p["gk_b"].reshape(1, dkk)]
    kseq, vseq = rl.seq_spec(dkk), rl.seq_spec(d)
    kshape, vshape = jax.ShapeDtypeStruct(rl.view + (dkk,), ACT), jax.ShapeDtypeStruct(rl.view + (d,), ACT)
    outs = pl.pallas_call(
        functools.partial(_gla_pre_body, rl=rl),
        grid=rl.grid,
        in_specs=[vseq] + [_resident(w.shape) for w in weights],
        out_specs=[kseq, kseq, kseq, vseq, vseq],
        out_shape=[kshape, kshape, jax.ShapeDtypeStruct(rl.view + (dkk,), F32), vshape, vshape],
        compiler_params=_params(),
        name="gla_pre",
    )(rl.flat(x), *weights)
    return [rl.unflat(o) for o in outs]


def _gla_body(q_ref, k_ref, g_ref, v_ref, gate_ref, s0_ref, gain_ref, o_ref, st_ref, s_scr, l_scr, o_scr,
              *, chunk, tile, nb, heads, dk, dv, zero_init):
    c = chunk
    ref_row = (c - 1) // 2
    t_idx = pl.program_id(1)
    incl = _iota((c, c), 1) <= _iota((c, c), 0)
    chains = [(n, h) for n in range(nb) for h in range(heads)]

    @pl.when(t_idx == 0)
    def _init():
        for i, (n, h) in enumerate(chains):
            s_scr[i] = jnp.zeros((dv, dk), F32) if zero_init else s0_ref[n, h].T

    tri = _chunk_tri(tile, c)
    for n in range(nb):
        l_scr[n] = _dot_hilo_rhs(tri, g_ref[n])

    def chunk_step(ci, carry):
        rows = pl.ds(pl.multiple_of(ci * c, c), c)
        ld = []
        for i, (n, h) in enumerate(chains):
            kl = slice(h * dk, (h + 1) * dk)
            q, k = q_ref[n, rows, kl].astype(F32), k_ref[n, rows, kl].astype(F32)
            bc = l_scr[n, rows, kl]
            b_ref = bc[ref_row:ref_row + 1, :]
            b_last = bc[c - 1:c, :]
            ld.append(dict(qi=(q * jnp.exp(bc - b_ref)).astype(BF16), ki=(k * jnp.exp(b_ref - bc)).astype(BF16),
                           qd=(q * jnp.exp(bc)).astype(BF16), kd=(k * jnp.exp(b_last - bc)).astype(BF16),
                           v=v_ref[n, rows, h * dv:(h + 1) * dv].astype(BF16), w_last=jnp.exp(b_last),
                           st=s_scr[i]))
        for x in ld:
            x["att"] = _dot_nt(x["qi"], x["ki"])
            x["os"] = _dot_nt(x["qd"], x["st"].astype(BF16))
            x["ds"] = _dot_tn(x["v"], x["kd"])
        for i, (x, (n, h)) in enumerate(zip(ld, chains)):
            o_scr[n, rows, h * dv:(h + 1) * dv] = x["os"] + _dot(jnp.where(incl, x["att"], 0.0).astype(BF16), x["v"])
            s_scr[i] = x["st"] * x["w_last"] + x["ds"]
        return carry

    lax.fori_loop(0, tile // c, chunk_step, 0)

    for n, h in chains:
        vl = slice(h * dv, (h + 1) * dv)
        o = o_scr[n, :, vl]
        gate = gate_ref[n, :, vl].astype(F32)
        on = o * lax.rsqrt(jnp.mean(o * o, -1, keepdims=True) + RMS_EPS) * gain_ref[...]
        o_ref[n, :, vl] = (on * (gate * jax.nn.sigmoid(gate))).astype(o_ref.dtype)

    @pl.when(t_idx == pl.num_programs(1) - 1)
    def _final():
        for i, (n, h) in enumerate(chains):
            st_ref[n, h] = s_scr[i].T


def _gla_scan(q, k, g, v, gate, s0, gain, heads, dk, dv, zero_init):
    bsz, t, _ = q.shape
    c = min(CHUNK, t)
    tile = min(t, SCAN_ROWS)
    nb = GLA_BATCH_PER_STEP
    assert t % tile == 0 and tile % c == 0 and c % SUBLANES == 0 and bsz % nb == 0
    kseq = pl.BlockSpec((nb, tile, heads * dk), lambda i, j: (i, j, 0))
    vseq = pl.BlockSpec((nb, tile, heads * dv), lambda i, j: (i, j, 0))
    st = pl.BlockSpec((nb, heads, dk, dv), lambda i, j: (i, 0, 0, 0))
    return pl.pallas_call(
        functools.partial(_gla_body, chunk=c, tile=tile, nb=nb, heads=heads, dk=dk, dv=dv, zero_init=zero_init),
        grid=(bsz // nb, t // tile),
        in_specs=[kseq, kseq, kseq, vseq, vseq, st, _resident((1, dv))],
        out_specs=[vseq, st],
        out_shape=[jax.ShapeDtypeStruct((bsz, t, heads * dv), ACT),
                   jax.ShapeDtypeStruct((bsz, heads, dk, dv), F32)],
        scratch_shapes=[pltpu.VMEM((nb * heads, dv, dk), F32), pltpu.VMEM((nb, tile, heads * dk), F32),
                        pltpu.VMEM((nb, tile, heads * dv), F32)],
        compiler_params=_params(),
        name="gla_scan",
    )(q, k, g, v, gate, s0, gain.reshape(1, dv))


def _cf_body(x_ref, o_ref, fill_ref, wo_ref, l1w_ref, l1b_ref, wu_ref, wg_ref, wd_ref, cw_ref, cb_ref,
             l2w_ref, l2b_ref, out_ref, z_ref, h_scr, carry_scr, *, rl):
    n_state = CONV_W - 1
    t_idx = pl.program_id(1)
    rows = rl.rows
    x = x_ref[...]
    x1 = _ln_rows(DN_ALPHA * x + _dot(o_ref[...].astype(BF16), wo_ref[...]), l1w_ref[...], l1b_ref[...])
    x1b = x1.astype(BF16)
    if rl.long:
        @pl.when(t_idx == 0)
        def _load_state():
            carry_scr[...] = fill_ref[...]

    for j in range(wu_ref.shape[1] // FFN_COLS):
        cs = slice(j * FFN_COLS, (j + 1) * FFN_COLS)
        zc = _dot(x1b, wg_ref[:, cs])
        uc = _dot(x1b, wu_ref[:, cs])
        carry_c = carry_scr.at[:, cs] if rl.long else None
        fill_c = None if rl.long else fill_ref.at[:, cs]
        zp1 = _prev_rows(zc, 1, rl, t_idx, carry_c, fill_c, n_state)
        zp2 = _prev_rows(zc, 2, rl, t_idx, carry_c, fill_c, n_state)
        if rl.long:
            carry_scr[:, cs] = zc[rows - n_state:rows, :]
        else:
            z_ref[:, cs] = zc
        cw = cw_ref[:, cs]
        pre = cb_ref[:, cs] + cw[0:1] * zp2 + cw[1:2] * zp1 + cw[2:3] * zc
        h_scr[:, cs] = (pre * jax.nn.sigmoid(pre) * uc).astype(BF16)

    out_ref[...] = _ln_rows(DN_ALPHA * x1 + _dot(h_scr[...], wd_ref[...]), l2w_ref[...], l2b_ref[...])
    if rl.long:
        @pl.when(t_idx == pl.num_programs(1) - 1)
        def _store_state():
            z_ref[...] = carry_scr[...]


def _proj_ffn(x, o, buf, wo, ln1, p, ln2, valid):
    bsz, t, d = x.shape
    f = p["wu"].shape[1]
    n_state = CONV_W - 1
    rl = _Rows(bsz, t, valid, FFN_ROWS if t >= FFN_ROWS else FFN_ROWS // 2)
    assert f % FFN_COLS == 0 and valid >= n_state
    seq = rl.seq_spec(d)
    vec = lambda n: _resident((1, n))
    if rl.long:
        z_spec, z_shape = pl.BlockSpec((None, n_state, f), lambda i, j: (i, 0, 0)), (bsz, n_state, f)
    else:
        z_spec, z_shape = rl.seq_spec(f), rl.view + (f,)
    out, z = pl.pallas_call(
        functools.partial(_cf_body, rl=rl),
        grid=rl.grid,
        in_specs=[seq, seq, rl.state_spec(n_state, f),
                  _resident((d, d)), vec(d), vec(d), _resident((d, f)), _resident((d, f)), _resident((f, d)),
                  _resident((CONV_W, f)), vec(f), vec(d), vec(d)],
        out_specs=[seq, z_spec],
        out_shape=[jax.ShapeDtypeStruct(rl.view + (d,), F32), jax.ShapeDtypeStruct(z_shape, F32)],
        scratch_shapes=[pltpu.VMEM((rl.rows, f), BF16), pltpu.VMEM((n_state, f), F32)],
        compiler_params=_params(),
        name="proj_ffn",
    )(rl.flat(x), rl.flat(o), rl.state_rows(buf), wo, ln1[0].reshape(1, d), ln1[1].reshape(1, d),
      p["wu"], p["wg"], p["wd"], p["conv_w"], p["conv_b"].reshape(1, f), ln2[0].reshape(1, d), ln2[1].reshape(1, d))
    if rl.long:
        return out, z
    return rl.unflat(out), rl.unflat(z)[:, valid - n_state:valid]


def _run_trunk(x, valid, st_rw, st_shift, st_hg, st_gl, st_conv, rw, rw_vmix, hg, lower_bounds, gl, ffn, ln, zero_init):
    new_rw, new_shift, new_hg, new_gl, new_conv = [], [], [], [], []
    v_first = None
    for i in range(DEPTH):
        j = i // N_MIXERS
        kind = i % N_MIXERS
        if kind == 0:
            p = {n: q[j] for n, q in rw.items()}
            vm = None if j == 0 else tuple(q[j - 1] for q in rw_vmix)
            r, wl, k, v, a, b, g = _rwkv_pre(x, st_shift[j], v_first, vm, p, valid)
            if vm is None:
                v_first = v
            o, s_t = _rwkv_scan(r, wl, k, v, a, b, g, st_rw[j], p, zero_init)
            new_rw.append(s_t)
            new_shift.append(x[:, valid - 1])
        elif kind == 1:
            p = {n: q[j] for n, q in hg.items()}
            q_, k, log_g, val, gate = _hgrn_pre(x, lower_bounds[i], p, valid)
            o, s_t = _gla_scan(q_, k, log_g, val, gate, st_hg[j], p["norm_w"], HG_HEADS, HG_HEAD, HG_HEAD, zero_init)
            new_hg.append(s_t)
        else:
            p = {n: q[j] for n, q in gl.items()}
            q_, k, log_g, val, gate = _gla_pre(x, p, valid)
            o, s_t = _gla_scan(q_, k, log_g, val, gate, st_gl[j], p["norm_w"], GL_HEADS, GL_DK, GL_DV, zero_init)
            new_gl.append(s_t)
        x, buf = _proj_ffn(x, o, st_conv[i], p["wo"], (ln["ln1_w"][i], ln["ln1_b"][i]),
                           {n: q[i] for n, q in ffn.items()}, (ln["ln2_w"][i], ln["ln2_b"][i]), valid)
        new_conv.append(buf)
    return (x[:, :valid], jnp.stack(new_rw), jnp.stack(new_shift), jnp.stack(new_hg), jnp.stack(new_gl),
            jnp.stack(new_conv))


def kernel(x_prompt, x_sample, state_rwkv, state_rwkv_shift, state_hgrn, state_gla, state_ffn_conv, rw_mix, rw_wr, rw_wk, rw_wv, rw_wo, rw_w0, rw_w1, rw_w2, rw_a0, rw_a1, rw_a2, rw_g1, rw_g2, rw_k_k, rw_k_a, rw_r_k, rw_lnx_w, rw_lnx_b, rw_v0, rw_v1, rw_v2, hg_wq, hg_wf, hg_wi, hg_wg, hg_wo, hg_norm_w, hg_lb_param, gl_wq, gl_wk, gl_wv, gl_wg, gl_gk1, gl_gk2, gl_gk_b, gl_wo, gl_norm_w, ffn_wu, ffn_wg, ffn_conv_w, ffn_conv_b, ffn_wd, ln1_w, ln1_b, ln2_w, ln2_b):
    bf = lambda w: w.astype(BF16)
    rw = dict(mix=rw_mix, wr=bf(rw_wr), wk=bf(rw_wk), wv=bf(rw_wv), wo=bf(rw_wo), w0=rw_w0, w1=bf(rw_w1),
              w2=bf(rw_w2), a0=rw_a0, a1=bf(rw_a1), a2=bf(rw_a2), g1=bf(rw_g1), g2=bf(rw_g2), k_k=rw_k_k,
              k_a=rw_k_a, r_k=rw_r_k, lnx_w=rw_lnx_w, lnx_b=rw_lnx_b)
    rw_vmix = (rw_v0, bf(rw_v1), bf(rw_v2))
    hg = dict(wq=bf(hg_wq), wf=bf(hg_wf), wi=bf(hg_wi), wg=bf(hg_wg), wo=bf(hg_wo), norm_w=hg_norm_w)
    gl = dict(wq=bf(gl_wq), wk=bf(gl_wk), wv=bf(gl_wv), wg=bf(gl_wg), gk1=bf(gl_gk1), gk2=bf(gl_gk2),
              gk_b=gl_gk_b, wo=bf(gl_wo), norm_w=gl_norm_w)
    ffn = dict(wu=bf(ffn_wu), wg=bf(ffn_wg), conv_w=ffn_conv_w, conv_b=ffn_conv_b, wd=bf(ffn_wd))
    ln = dict(ln1_w=ln1_w, ln1_b=ln1_b, ln2_w=ln2_w, ln2_b=ln2_b)
    lb_soft = jax.nn.softmax(hg_lb_param, axis=0)
    lower_bounds = jnp.cumsum(lb_soft, axis=0) - lb_soft[0]

    nb = x_prompt.shape[0]
    zeros = lambda s: jnp.zeros((s.shape[0], nb) + s.shape[2:], s.dtype)
    p_out = _run_trunk(x_prompt, x_prompt.shape[1], zeros(state_rwkv), zeros(state_rwkv_shift), zeros(state_hgrn),
                       zeros(state_gla), zeros(state_ffn_conv), rw, rw_vmix, hg, lower_bounds, gl, ffn, ln, True)
    t_s = x_sample.shape[1]
    t_pad = max(SUBLANES, 1 << (t_s - 1).bit_length())
    xs = jnp.pad(x_sample, ((0, 0), (0, t_pad - t_s), (0, 0)))
    s_out = _run_trunk(xs, t_s, state_rwkv, state_rwkv_shift, state_hgrn, state_gla, state_ffn_conv,
                       rw, rw_vmix, hg, lower_bounds, gl, ffn, ln, False)
    return (p_out[0], s_out[0]) + tuple(p_out[1:]) + tuple(s_out[1:])
```

```python
import functools
import math

import jax
import jax.numpy as jnp
from jax import lax
from jax.experimental import pallas as pl
from jax.experimental.pallas import tpu as pltpu

D_MODEL = 1024
DEPTH = 4
N_MIXERS = 3
RW_HEAD = 64
RW_HEADS = D_MODEL // RW_HEAD
RW_LNX_EPS = 64e-5
HG_HEAD = 128
HG_HEADS = D_MODEL // HG_HEAD
GL_HEADS = 4
GL_DK = (D_MODEL // 2) // GL_HEADS
GL_DV = D_MODEL // GL_HEADS
GL_GATE_NORM = 16.0
CONV_W = 3
LN_EPS = 1e-5
RMS_EPS = 1e-5
DN_ALPHA = (2 * DEPTH) ** 0.25

LANES = 128
SUBLANES = 8
RW_GROUP = 256
RW_GROUP_HEADS = RW_GROUP // RW_HEAD
CHUNK = 32
RW_CHUNK = 64
RW_BATCH_PER_STEP = 2
RW_BATCH_PER_STEP_SHORT = 4
GLA_BATCH_PER_STEP = 2
SCAN_ROWS = 256
PRE_ROWS = 512
FFN_ROWS = 512
SUB_ROWS = 256
FFN_COLS = 256
VMEM_LIMIT = 56 * 1024 * 1024

BF16 = jnp.bfloat16
F32 = jnp.float32
ACT = BF16


def _dot(a, b):
    return jnp.dot(a, b, preferred_element_type=F32)


def _dot_nt(a, b):
    return lax.dot_general(a, b, (((1,), (1,)), ((), ())), preferred_element_type=F32)


def _dot_tn(a, b):
    return lax.dot_general(a, b, (((0,), (0,)), ((), ())), preferred_element_type=F32)


def _dot_hilo_rhs(sel, x):
    hi = x.astype(BF16)
    lo = (x - hi.astype(F32)).astype(BF16)
    return _dot(sel, hi) + _dot(sel, lo)


def _head_sums(x, ones_bd):
    return _dot(x.astype(BF16), ones_bd)


def _iota(shape, dim):
    return lax.broadcasted_iota(jnp.int32, shape, dim)


def _chunk_tri(n, c):
    row, col = _iota((n, n), 0), _iota((n, n), 1)
    sh = int(math.log2(c))
    keep = (col <= row) & ((row >> sh) == (col >> sh))
    return jnp.where(keep, 1.0, 0.0).astype(BF16)


def _softplus(t):
    return jnp.maximum(t, 0.0) + jnp.log(1.0 + jnp.exp(-jnp.abs(t)))


def _ln_rows(x, w, b):
    mu = jnp.mean(x, -1, keepdims=True)
    xc = x - mu
    var = jnp.mean(xc * xc, -1, keepdims=True)
    return xc * lax.rsqrt(var + LN_EPS) * w + b


def _resident(shape):
    return pl.BlockSpec(shape, lambda i, j: (0,) * len(shape), pipeline_mode=pl.Buffered(1))


def _sub_blocks(rows):
    sub = min(rows, SUB_ROWS)
    assert rows % sub == 0
    return [slice(s * sub, (s + 1) * sub) for s in range(rows // sub)]


def _params():
    return pltpu.CompilerParams(dimension_semantics=("arbitrary", "arbitrary"), vmem_limit_bytes=VMEM_LIMIT)


class _Rows:
    def __init__(self, bsz, t, valid, max_rows):
        self.bsz, self.t, self.valid = bsz, t, valid
        self.long = t >= max_rows
        if self.long:
            assert t % max_rows == 0 and valid == t
            self.rows, self.grid, self.view = max_rows, (bsz, t // max_rows), (bsz, t)
        else:
            total = bsz * t
            self.rows = min(total, max_rows)
            assert t & (t - 1) == 0 and self.rows % t == 0 and total % self.rows == 0
            self.grid, self.view = (1, total // self.rows), (1, total)

    def flat(self, x):
        return x.reshape(self.view + x.shape[2:])

    def unflat(self, x):
        return x.reshape((self.bsz, self.t) + x.shape[2:])

    def seq_spec(self, width):
        return pl.BlockSpec((None, self.rows, width), lambda i, j: (i, j, 0))

    def state_rows(self, state):
        if self.long:
            return state
        return self.flat(jnp.pad(state, ((0, 0), (0, self.t - state.shape[1]), (0, 0))))

    def state_spec(self, n, width):
        if self.long:
            return pl.BlockSpec((None, n, width), lambda i, j: (i, 0, 0))
        return self.seq_spec(width)


def _prev_rows(x, shift, rl, t_idx, carry_scr, fill_ref, n_prev):
    rows, width = x.shape
    row = _iota((rows, width), 0)
    rolled = pltpu.roll(x, shift, 0)
    if rl.long:
        out = rolled
        for i in range(shift):
            src = n_prev - shift + i
            out = jnp.where(row == i, carry_scr[src:src + 1, :], out)
        return out
    fill = fill_ref[...]
    if n_prev != shift:
        fill = pltpu.roll(fill, rows - (n_prev - shift), 0)
    return jnp.where((row & (rl.t - 1)) >= shift, rolled, fill)


def _rwkv_pre_body(*refs, rl, has_vmix):
    it = iter(refs)
    x_ref, fill_ref = next(it), next(it)
    vf_ref = next(it) if has_vmix else None
    mix_ref, wr_ref, wk_ref, wv_ref, w0_ref, w1_ref, w2_ref = (next(it) for _ in range(7))
    a0_ref, a1_ref, a2_ref, g1_ref, g2_ref, kk_ref, ka_ref = (next(it) for _ in range(7))
    v0_ref, v1_ref, v2_ref = (next(it) for _ in range(3)) if has_vmix else (None, None, None)
    r_out, wl_out, k_out, v_out, a_out, b_out, g_out, carry_scr = (next(it) for _ in range(8))

    t_idx = pl.program_id(1)
    if rl.long:
        @pl.when(t_idx == 0)
        def _load_state():
            carry_scr[0:1, :] = fill_ref[...]
    bd = (_iota((RW_GROUP, RW_GROUP), 0) >> 6) == (_iota((RW_GROUP, RW_GROUP), 1) >> 6)
    ones_bd = jnp.where(bd, 1.0, 0.0).astype(BF16)

    subs = _sub_blocks(rl.rows)
    stage1 = []
    for rs in subs:
        rows = rs.stop - rs.start
        x = x_ref[rs, :]
        x_prev = _prev_rows(x, 1, rl, t_idx, carry_scr, None if rl.long else fill_ref.at[rs, :], 1)
        if rl.long:
            carry_scr[0:1, :] = x[rows - 1:rows, :]
        xx = x_prev - x
        xr, xw, xk, xv, xa, xg = [(x + xx * mix_ref[j:j + 1, :]).astype(BF16) for j in range(6)]
        stage1.append(dict(r=_dot(xr, wr_ref[...]), k=_dot(xk, wk_ref[...]), v=_dot(xv, wv_ref[...]),
                           lw=_dot(xw, w1_ref[...]), la=_dot(xa, a1_ref[...]), lg=_dot(xg, g1_ref[...]),
                           lv=_dot(xv, v1_ref[...]) if has_vmix else None))
    for rs, s1 in zip(subs, stage1):
        rows = rs.stop - rs.start
        r, k, v = s1["r"], s1["k"], s1["v"]
        lora_w = _dot(jnp.tanh(s1["lw"]).astype(BF16), w2_ref[...])
        wl = -jnp.exp(-_softplus(-(w0_ref[...] + lora_w)) - 0.5)
        a = jax.nn.sigmoid(a0_ref[...] + _dot(s1["la"].astype(BF16), a2_ref[...]))
        if has_vmix:
            gate = jax.nn.sigmoid(v0_ref[...] + _dot(s1["lv"].astype(BF16), v2_ref[...]))
            v = v + (vf_ref[rs, :].astype(F32) - v) * gate
        g_out[rs, :] = _dot(jax.nn.sigmoid(s1["lg"]).astype(BF16), g2_ref[...]).astype(g_out.dtype)

        kk = k * kk_ref[...]
        k = k * (1.0 + (a - 1.0) * ka_ref[...])
        if not rl.long and rl.valid < rl.t:
            live = (_iota((rows, D_MODEL), 0) & (rl.t - 1)) < rl.valid
            wl, k, v, kk = (jnp.where(live, z, 0.0) for z in (wl, k, v, kk))
        r_out[rs, :] = r.astype(r_out.dtype)
        wl_out[rs, :] = wl
        k_out[rs, :] = k.astype(k_out.dtype)
        v_out[rs, :] = v.astype(v_out.dtype)
        for g in range(D_MODEL // RW_GROUP):
            lanes = slice(g * RW_GROUP, (g + 1) * RW_GROUP)
            kkg = kk[:, lanes]
            kkn = kkg / jnp.maximum(jnp.sqrt(_head_sums(kkg * kkg, ones_bd)), 1e-12)
            a_out[rs, lanes] = (-kkn).astype(a_out.dtype)
            b_out[rs, lanes] = (kkn * a[:, lanes]).astype(b_out.dtype)


def _rwkv_pre(x, shift, v_first, vmix, p, valid):
    bsz, t, d = x.shape
    rl = _Rows(bsz, t, valid, PRE_ROWS)
    has_vmix = vmix is not None
    seq = rl.seq_spec(d)
    vec = lambda z: z.reshape(1, d)
    args = [rl.flat(x), rl.state_rows(shift[:, None, :])]
    specs = [seq, rl.state_spec(1, d)]
    if has_vmix:
        args.append(rl.flat(v_first))
        specs.append(seq)
    weights = [p["mix"], p["wr"], p["wk"], p["wv"], vec(p["w0"]), p["w1"], p["w2"], vec(p["a0"]), p["a1"], p["a2"],
               p["g1"], p["g2"], vec(p["k_k"]), vec(p["k_a"])]
    if has_vmix:
        weights += [vec(vmix[0]), vmix[1], vmix[2]]
    args += weights
    specs += [_resident(w.shape) for w in weights]
    outs = pl.pallas_call(
        functools.partial(_rwkv_pre_body, rl=rl, has_vmix=has_vmix),
        grid=rl.grid,
        in_specs=specs,
        out_specs=[seq] * 7,
        out_shape=[jax.ShapeDtypeStruct(rl.view + (d,), F32 if i == 1 else ACT) for i in range(7)],
        scratch_shapes=[pltpu.VMEM((SUBLANES, d), F32)],
        compiler_params=_params(),
        name="rwkv_pre",
    )(*args)
    return [rl.unflat(o) for o in outs]


def _rwkv_body(r_ref, wl_ref, k_ref, v_ref, a_ref, b_ref, g_ref, s0_ref, rk_ref, lw_ref, lb_ref,
               y_ref, st_ref, s_scr, l_scr, y_scr, *, chunk, tile, nb, zero_init):
    c = chunk
    gh = RW_GROUP_HEADS
    n_groups = D_MODEL // RW_GROUP
    t_idx = pl.program_id(1)
    log2c = int(math.log2(c))

    bd_r, bd_c = _iota((RW_GROUP, RW_GROUP), 0), _iota((RW_GROUP, RW_GROUP), 1)
    bd256 = (bd_r >> 6) == (bd_c >> 6)
    ones_bd = jnp.where(bd256, 1.0, 0.0).astype(BF16)
    hm = (_iota((gh * c, RW_GROUP), 0) >> log2c) == (_iota((gh * c, RW_GROUP), 1) >> 6)
    hm_b = jnp.where(hm, 1.0, 0.0).astype(BF16)
    bdm = (_iota((gh * c, gh * c), 0) >> log2c) == (_iota((gh * c, gh * c), 1) >> log2c)
    bdm_b = jnp.where(bdm, 1.0, 0.0).astype(BF16)
    trow = _iota((c, gh * c), 0)
    tcol = _iota((c, gh * c), 1) & (c - 1)
    strict = tcol < trow
    incl = tcol <= trow
    chains = [(n, g) for n in range(nb) for g in range(n_groups)]

    def head_block(i, n, g, h):
        blk = slice(h * RW_HEAD, (h + 1) * RW_HEAD)
        return (i, blk, blk), (n, slice(g * RW_GROUP + h * RW_HEAD, g * RW_GROUP + (h + 1) * RW_HEAD), slice(None))

    @pl.when(t_idx == 0)
    def _init():
        for i, (n, g) in enumerate(chains):
            s_scr[i] = jnp.zeros((RW_GROUP, RW_GROUP), F32)
            if not zero_init:
                for h in range(gh):
                    dst, src = head_block(i, n, g, h)
                    s_scr[dst] = s0_ref[src]

    tri = _chunk_tri(tile, c)
    for n in range(nb):
        l_scr[n] = _dot_hilo_rhs(tri, wl_ref[n])

    def stack(x_b):
        return jnp.concatenate([x_b] * gh, axis=0) * hm_b

    def chunk_step(ci, carry):
        rows = pl.ds(pl.multiple_of(ci * c, c), c)
        ld = []
        for n, g in chains:
            lanes = slice(g * RW_GROUP, (g + 1) * RW_GROUP)
            f32 = lambda ref: ref[n, rows, lanes].astype(F32)
            ld.append(dict(r=f32(r_ref), wl=wl_ref[n, rows, lanes], k=f32(k_ref), v=v_ref[n, rows, lanes].astype(BF16),
                           a=f32(a_ref), b=f32(b_ref), lc=l_scr[n, rows, lanes]))
        for i, x in enumerate(ld):
            lc = x["lc"]
            x["lend"] = lc[c - 1:c, :]
            w_inv = jnp.exp(-lc)
            at = x["a"] * jnp.exp(lc - x["wl"])
            rt = x["r"] * jnp.exp(lc)
            x["ar"] = jnp.concatenate([at, rt], axis=0).astype(BF16)
            x["bts"] = stack((x["b"] * w_inv).astype(BF16))
            x["kts"] = stack((x["k"] * w_inv).astype(BF16))
            x["vs"] = stack(x["v"])
            x["sg"] = s_scr[i]
        for x in ld:
            x["sb"] = _dot_nt(x["ar"], x["bts"])
            x["sk"] = _dot_nt(x["ar"], x["kts"])
            x["x2"] = _dot_nt(x["ar"], x["sg"].astype(BF16))
        for x in ld:
            x["p"] = jnp.where(strict, x["sb"][:c], 0.0).astype(BF16)
            x["arb"] = jnp.where(incl, x["sb"][c:], 0.0).astype(BF16)
            mak = jnp.where(strict, x["sk"][:c], 0.0).astype(BF16)
            x["ark"] = jnp.where(incl, x["sk"][c:], 0.0).astype(BF16)
            x["u"] = x["x2"][:c] + _dot(mak, x["vs"])
        for it in range(log2c):
            for x in ld:
                x["u"] = x["u"] + _dot(x["p"], stack(x["u"].astype(BF16)))
            if it < log2c - 1:
                for x in ld:
                    x["p"] = _dot(x["p"], jnp.concatenate([x["p"]] * gh, axis=0) * bdm_b).astype(BF16)
        for x, (n, g) in zip(ld, chains):
            y = x["x2"][c:] + _dot(x["arb"], stack(x["u"].astype(BF16))) + _dot(x["ark"], x["vs"])
            y_scr[n, rows, g * RW_GROUP:(g + 1) * RW_GROUP] = y
        for i, x in enumerate(ld):
            w_end = jnp.exp(x["lend"] - x["lc"])
            uv = jnp.concatenate([x["u"].astype(BF16), x["v"]], axis=0)
            bk = jnp.concatenate([x["b"] * w_end, x["k"] * w_end], axis=0).astype(BF16)
            s_scr[i] = x["sg"] * jnp.exp(x["lend"]) + jnp.where(bd256, _dot_tn(uv, bk), 0.0)
        return carry

    lax.fori_loop(0, tile // c, chunk_step, 0)

    inv_n = 1.0 / RW_HEAD
    for n, g in chains:
        lanes = slice(g * RW_GROUP, (g + 1) * RW_GROUP)
        y = y_scr[n, :, lanes]
        f32 = lambda ref: ref[n, :, lanes].astype(F32)
        yc = y - _head_sums(y, ones_bd) * inv_n
        yn = yc * lax.rsqrt(_head_sums(yc * yc, ones_bd) * inv_n + RW_LNX_EPS) * lw_ref[:, lanes] + lb_ref[:, lanes]
        bonus = _head_sums(f32(r_ref) * f32(k_ref) * rk_ref[:, lanes], ones_bd) * f32(v_ref)
        y_ref[n, :, lanes] = ((yn + bonus) * f32(g_ref)).astype(y_ref.dtype)

    @pl.when(t_idx == pl.num_programs(1) - 1)
    def _final():
        for i, (n, g) in enumerate(chains):
            for h in range(gh):
                src, dst = head_block(i, n, g, h)
                st_ref[dst] = s_scr[src]


def _rwkv_scan(r, wl, k, v, a, b, g, s0, p, zero_init):
    bsz, t, d = r.shape
    c = min(RW_CHUNK, t)
    tile = min(t, SCAN_ROWS)
    nb = RW_BATCH_PER_STEP if t > c else RW_BATCH_PER_STEP_SHORT
    assert t % tile == 0 and tile % c == 0 and c % SUBLANES == 0 and bsz % nb == 0
    s0r = s0.reshape(bsz, d, RW_HEAD)
    seq = pl.BlockSpec((nb, tile, d), lambda i, j: (i, j, 0))
    st = pl.BlockSpec((nb, d, RW_HEAD), lambda i, j: (i, 0, 0))
    vec = _resident((1, d))
    y, s_t = pl.pallas_call(
        functools.partial(_rwkv_body, chunk=c, tile=tile, nb=nb, zero_init=zero_init),
        grid=(bsz // nb, t // tile),
        in_specs=[seq] * 7 + [st, vec, vec, vec],
        out_specs=[seq, st],
        out_shape=[jax.ShapeDtypeStruct((bsz, t, d), ACT), jax.ShapeDtypeStruct((bsz, d, RW_HEAD), F32)],
        scratch_shapes=[pltpu.VMEM((nb * (d // RW_GROUP), RW_GROUP, RW_GROUP), F32),
                        pltpu.VMEM((nb, tile, d), F32), pltpu.VMEM((nb, tile, d), F32)],
        compiler_params=_params(),
        name="rwkv_scan",
    )(r, wl, k, v, a, b, g, s0r, p["r_k"].reshape(1, d), p["lnx_w"].reshape(1, d), p["lnx_b"].reshape(1, d))
    return y, s_t.reshape(bsz, RW_HEADS, RW_HEAD, RW_HEAD)


def _hgrn_pre_body(x_ref, w_ref, llb_ref, l1m_ref, omlb_ref, q_out, k_out, g_out, i_out, gate_out, *, rl):
    d = D_MODEL
    subs = _sub_blocks(rl.rows)
    ys = [_dot(x_ref[rs, :].astype(BF16), w_ref[...]) for rs in subs]
    for rs, y in reversed(list(zip(subs, ys))):
        qz, f = y[:, 0:d], y[:, d:2 * d]
        i_out[rs, :] = y[:, 2 * d:3 * d].astype(i_out.dtype)
        gate_out[rs, :] = y[:, 3 * d:4 * d].astype(gate_out.dtype)
        q_out[rs, :] = (qz * jax.nn.sigmoid(qz) * HG_HEAD ** -0.5).astype(q_out.dtype)
        la = llb_ref[...]
        lb = l1m_ref[...] - _softplus(-f)
        log_g = jnp.maximum(la, lb) + jnp.log(1.0 + jnp.exp(-jnp.abs(la - lb)))
        k = omlb_ref[...] * jax.nn.sigmoid(-f)
        if not rl.long and rl.valid < rl.t:
            live = (_iota(f.shape, 0) & (rl.t - 1)) < rl.valid
            log_g, k = jnp.where(live, log_g, 0.0), jnp.where(live, k, 0.0)
        g_out[rs, :] = log_g
        k_out[rs, :] = k.astype(k_out.dtype)


def _hgrn_pre(x, lb, p, valid):
    bsz, t, d = x.shape
    rl = _Rows(bsz, t, valid, PRE_ROWS)
    seq = rl.seq_spec(d)
    vecs = [jnp.log(lb).reshape(1, d), jnp.log1p(-lb).reshape(1, d), (1.0 - lb).reshape(1, d)]
    weights = [jnp.concatenate([p["wq"], p["wf"], p["wi"], p["wg"]], axis=1)] + vecs
    outs = pl.pallas_call(
        functools.partial(_hgrn_pre_body, rl=rl),
        grid=rl.grid,
        in_specs=[seq] + [_resident(w.shape) for w in weights],
        out_specs=[seq] * 5,
        out_shape=[jax.ShapeDtypeStruct(rl.view + (d,), F32 if i == 2 else ACT) for i in range(5)],
        compiler_params=_params(),
        name="hgrn_pre",
    )(rl.flat(x), *weights)
    return [rl.unflat(o) for o in outs]


def _gla_pre_body(x_ref, w_ref, gk2_ref, gkb_ref, q_out, k_out, g_out, v_out, gate_out, *, rl):
    d, dkk = D_MODEL, GL_HEADS * GL_DK
    subs = _sub_blocks(rl.rows)
    ys = [_dot(x_ref[rs, :].astype(BF16), w_ref[...]) for rs in subs]
    for rs, y in zip(subs, ys):
        q_out[rs, :] = (y[:, 0:dkk] * GL_DK ** -0.5).astype(q_out.dtype)
        k = y[:, dkk:2 * dkk]
        v_out[rs, :] = y[:, 2 * dkk:2 * dkk + d].astype(v_out.dtype)
        gate_out[rs, :] = y[:, 2 * dkk + d:2 * dkk + 2 * d].astype(gate_out.dtype)
        gk = _dot(y[:, 2 * dkk + 2 * d:].astype(BF16), gk2_ref[...]) + gkb_ref[...]
        log_g = -_softplus(-gk) * (1.0 / GL_GATE_NORM)
        if not rl.long and rl.valid < rl.t:
            live = (_iota(k.shape, 0) & (rl.t - 1)) < rl.valid
            log_g, k = jnp.where(live, log_g, 0.0), jnp.where(live, k, 0.0)
        g_out[rs, :] = log_g
        k_out[rs, :] = k.astype(k_out.dtype)


def _gla_pre(x, p, valid):
    bsz, t, d = x.shape
    dkk = GL_HEADS * GL_DK
    rl = _Rows(bsz, t, valid, PRE_ROWS)
    rank = p["gk1"].shape[1]
    gk1 = jnp.pad(p["gk1"], ((0, 0), (0, LANES - rank)))
    gk2 = jnp.pad(p["gk2"], ((0, LANES - rank), (0, 0)))
    weights = [jnp.concatenate([p["wq"], p["wk"], p["wv"], p["wg"], gk1], axis=1), gk2, p["gk_b"].reshape(1, dkk)]
    kseq, vseq = rl.seq_spec(dkk), rl.seq_spec(d)
    kshape, vshape = jax.ShapeDtypeStruct(rl.view + (dkk,), ACT), jax.ShapeDtypeStruct(rl.view + (d,), ACT)
    outs = pl.pallas_call(
        functools.partial(_gla_pre_body, rl=rl),
        grid=rl.grid,
        in_specs=[vseq] + [_resident(w.shape) for w in weights],
        out_specs=[kseq, kseq, kseq, vseq, vseq],
        out_shape=[kshape, kshape, jax.ShapeDtypeStruct(rl.view + (dkk,), F32), vshape, vshape],
        compiler_params=_params(),
        name="gla_pre",
    )(rl.flat(x), *weights)
    return [rl.unflat(o) for o in outs]


def _gla_body(q_ref, k_ref, g_ref, v_ref, gate_ref, s0_ref, gain_ref, o_ref, st_ref, s_scr, l_scr, o_scr,
              *, chunk, tile, nb, heads, dk, dv, zero_init):
    c = chunk
    ref_row = (c - 1) // 2
    t_idx = pl.program_id(1)
    incl = _iota((c, c), 1) <= _iota((c, c), 0)
    chains = [(n, h) for n in range(nb) for h in range(heads)]

    @pl.when(t_idx == 0)
    def _init():
        for i, (n, h) in enumerate(chains):
            s_scr[i] = jnp.zeros((dv, dk), F32) if zero_init else s0_ref[n, h].T

    tri = _chunk_tri(tile, c)
    for n in range(nb):
        l_scr[n] = _dot_hilo_rhs(tri, g_ref[n])

    def chunk_step(ci, carry):
        rows = pl.ds(pl.multiple_of(ci * c, c), c)
        ld = []
        for i, (n, h) in enumerate(chains):
            kl = slice(h * dk, (h + 1) * dk)
            q, k = q_ref[n, rows, kl].astype(F32), k_ref[n, rows, kl].astype(F32)
            bc = l_scr[n, rows, kl]
            b_ref = bc[ref_row:ref_row + 1, :]
            b_last = bc[c - 1:c, :]
            ld.append(dict(qi=(q * jnp.exp(bc - b_ref)).astype(BF16), ki=(k * jnp.exp(b_ref - bc)).astype(BF16),
                           qd=(q * jnp.exp(bc)).astype(BF16), kd=(k * jnp.exp(b_last - bc)).astype(BF16),
                           v=v_ref[n, rows, h * dv:(h + 1) * dv].astype(BF16), w_last=jnp.exp(b_last),
                           st=s_scr[i]))
        for x in ld:
            x["att"] = _dot_nt(x["qi"], x["ki"])
            x["os"] = _dot_nt(x["qd"], x["st"].astype(BF16))
            x["ds"] = _dot_tn(x["v"], x["kd"])
        for i, (x, (n, h)) in enumerate(zip(ld, chains)):
            o_scr[n, rows, h * dv:(h + 1) * dv] = x["os"] + _dot(jnp.where(incl, x["att"], 0.0).astype(BF16), x["v"])
            s_scr[i] = x["st"] * x["w_last"] + x["ds"]
        return carry

    lax.fori_loop(0, tile // c, chunk_step, 0)

    for n, h in chains:
        vl = slice(h * dv, (h + 1) * dv)
        o = o_scr[n, :, vl]
        gate = gate_ref[n, :, vl].astype(F32)
        on = o * lax.rsqrt(jnp.mean(o * o, -1, keepdims=True) + RMS_EPS) * gain_ref[...]
        o_ref[n, :, vl] = (on * (gate * jax.nn.sigmoid(gate))).astype(o_ref.dtype)

    @pl.when(t_idx == pl.num_programs(1) - 1)
    def _final():
        for i, (n, h) in enumerate(chains):
            st_ref[n, h] = s_scr[i].T


def _gla_scan(q, k, g, v, gate, s0, gain, heads, dk, dv, zero_init):
    bsz, t, _ = q.shape
    c = min(CHUNK, t)
    tile = min(t, SCAN_ROWS)
    nb = GLA_BATCH_PER_STEP
    assert t % tile == 0 and tile % c == 0 and c % SUBLANES == 0 and bsz % nb == 0
    kseq = pl.BlockSpec((nb, tile, heads * dk), lambda i, j: (i, j, 0))
    vseq = pl.BlockSpec((nb, tile, heads * dv), lambda i, j: (i, j, 0))
    st = pl.BlockSpec((nb, heads, dk, dv), lambda i, j: (i, 0, 0, 0))
    return pl.pallas_call(
        functools.partial(_gla_body, chunk=c, tile=tile, nb=nb, heads=heads, dk=dk, dv=dv, zero_init=zero_init),
        grid=(bsz // nb, t // tile),
        in_specs=[kseq, kseq, kseq, vseq, vseq, st, _resident((1, dv))],
        out_specs=[vseq, st],
        out_shape=[jax.ShapeDtypeStruct((bsz, t, heads * dv), ACT),
                   jax.ShapeDtypeStruct((bsz, heads, dk, dv), F32)],
        scratch_shapes=[pltpu.VMEM((nb * heads, dv, dk), F32), pltpu.VMEM((nb, tile, heads * dk), F32),
                        pltpu.VMEM((nb, tile, heads * dv), F32)],
        compiler_params=_params(),
        name="gla_scan",
    )(q, k, g, v, gate, s0, gain.reshape(1, dv))


def _cf_body(x_ref, o_ref, fill_ref, wo_ref, l1w_ref, l1b_ref, wu_ref, wg_ref, wd_ref, cw_ref, cb_ref,
             l2w_ref, l2b_ref, out_ref, z_ref, h_scr, carry_scr, *, rl):
    n_state = CONV_W - 1
    t_idx = pl.program_id(1)
    subs = _sub_blocks(rl.rows)
    sub = subs[0].stop
    if rl.long:
        @pl.when(t_idx == 0)
        def _load_state():
            carry_scr[...] = fill_ref[...]

    x1s = [_ln_rows(DN_ALPHA * x_ref[rs, :] + _dot(o_ref[rs, :].astype(BF16), wo_ref[...]), l1w_ref[...], l1b_ref[...])
           for rs in subs]
    for rs, x1 in zip(subs, x1s):
        x1b = x1.astype(BF16)
        for j in range(wu_ref.shape[1] // FFN_COLS):
            cs = slice(j * FFN_COLS, (j + 1) * FFN_COLS)
            zc = _dot(x1b, wg_ref[:, cs])
            uc = _dot(x1b, wu_ref[:, cs])
            carry_c = carry_scr.at[:, cs] if rl.long else None
            fill_c = None if rl.long else fill_ref.at[rs, cs]
            zp1 = _prev_rows(zc, 1, rl, t_idx, carry_c, fill_c, n_state)
            zp2 = _prev_rows(zc, 2, rl, t_idx, carry_c, fill_c, n_state)
            if rl.long:
                carry_scr[:, cs] = zc[sub - n_state:sub, :]
            else:
                z_ref[rs, cs] = zc
            cw = cw_ref[:, cs]
            pre = cb_ref[:, cs] + cw[0:1] * zp2 + cw[1:2] * zp1 + cw[2:3] * zc
            h_scr[rs, cs] = (pre * jax.nn.sigmoid(pre) * uc).astype(BF16)
        out_ref[rs, :] = _ln_rows(DN_ALPHA * x1 + _dot(h_scr[rs, :], wd_ref[...]), l2w_ref[...], l2b_ref[...])
    if rl.long:
        @pl.when(t_idx == pl.num_programs(1) - 1)
        def _store_state():
            z_ref[...] = carry_scr[...]


def _proj_ffn(x, o, buf, wo, ln1, p, ln2, valid):
    bsz, t, d = x.shape
    f = p["wu"].shape[1]
    n_state = CONV_W - 1
    rl = _Rows(bsz, t, valid, FFN_ROWS if t >= FFN_ROWS else FFN_ROWS // 2)
    assert f % FFN_COLS == 0 and valid >= n_state
    seq = rl.seq_spec(d)
    vec = lambda n: _resident((1, n))
    if rl.long:
        z_spec, z_shape = pl.BlockSpec((None, n_state, f), lambda i, j: (i, 0, 0)), (bsz, n_state, f)
    else:
        z_spec, z_shape = rl.seq_spec(f), rl.view + (f,)
    out, z = pl.pallas_call(
        functools.partial(_cf_body, rl=rl),
        grid=rl.grid,
        in_specs=[seq, seq, rl.state_spec(n_state, f),
                  _resident((d, d)), vec(d), vec(d), _resident((d, f)), _resident((d, f)), _resident((f, d)),
                  _resident((CONV_W, f)), vec(f), vec(d), vec(d)],
        out_specs=[seq, z_spec],
        out_shape=[jax.ShapeDtypeStruct(rl.view + (d,), F32), jax.ShapeDtypeStruct(z_shape, F32)],
        scratch_shapes=[pltpu.VMEM((rl.rows, f), BF16), pltpu.VMEM((n_state, f), F32)],
        compiler_params=_params(),
        name="proj_ffn",
    )(rl.flat(x), rl.flat(o), rl.state_rows(buf), wo, ln1[0].reshape(1, d), ln1[1].reshape(1, d),
      p["wu"], p["wg"], p["wd"], p["conv_w"], p["conv_b"].reshape(1, f), ln2[0].reshape(1, d), ln2[1].reshape(1, d))
    if rl.long:
        return out, z
    return rl.unflat(out), rl.unflat(z)[:, valid - n_state:valid]


def _run_trunk(x, valid, st_rw, st_shift, st_hg, st_gl, st_conv, rw, rw_vmix, hg, lower_bounds, gl, ffn, ln, zero_init):
    new_rw, new_shift, new_hg, new_gl, new_conv = [], [], [], [], []
    v_first = None
    for i in range(DEPTH):
        j = i // N_MIXERS
        kind = i % N_MIXERS
        if kind == 0:
            p = {n: q[j] for n, q in rw.items()}
            vm = None if j == 0 else tuple(q[j - 1] for q in rw_vmix)
            r, wl, k, v, a, b, g = _rwkv_pre(x, st_shift[j], v_first, vm, p, valid)
            if vm is None:
                v_first = v
            o, s_t = _rwkv_scan(r, wl, k, v, a, b, g, st_rw[j], p, zero_init)
            new_rw.append(s_t)
            new_shift.append(x[:, valid - 1])
        elif kind == 1:
            p = {n: q[j] for n, q in hg.items()}
            q_, k, log_g, val, gate = _hgrn_pre(x, lower_bounds[i], p, valid)
            o, s_t = _gla_scan(q_, k, log_g, val, gate, st_hg[j], p["norm_w"], HG_HEADS, HG_HEAD, HG_HEAD, zero_init)
            new_hg.append(s_t)
        else:
            p = {n: q[j] for n, q in gl.items()}
            q_, k, log_g, val, gate = _gla_pre(x, p, valid)
            o, s_t = _gla_scan(q_, k, log_g, val, gate, st_gl[j], p["norm_w"], GL_HEADS, GL_DK, GL_DV, zero_init)
            new_gl.append(s_t)
        x, buf = _proj_ffn(x, o, st_conv[i], p["wo"], (ln["ln1_w"][i], ln["ln1_b"][i]),
                           {n: q[i] for n, q in ffn.items()}, (ln["ln2_w"][i], ln["ln2_b"][i]), valid)
        new_conv.append(buf)
    return (x[:, :valid], jnp.stack(new_rw), jnp.stack(new_shift), jnp.stack(new_hg), jnp.stack(new_gl),
            jnp.stack(new_conv))


def kernel(x_prompt, x_sample, state_rwkv, state_rwkv_shift, state_hgrn, state_gla, state_ffn_conv, rw_mix, rw_wr, rw_wk, rw_wv, rw_wo, rw_w0, rw_w1, rw_w2, rw_a0, rw_a1, rw_a2, rw_g1, rw_g2, rw_k_k, rw_k_a, rw_r_k, rw_lnx_w, rw_lnx_b, rw_v0, rw_v1, rw_v2, hg_wq, hg_wf, hg_wi, hg_wg, hg_wo, hg_norm_w, hg_lb_param, gl_wq, gl_wk, gl_wv, gl_wg, gl_gk1, gl_gk2, gl_gk_b, gl_wo, gl_norm_w, ffn_wu, ffn_wg, ffn_conv_w, ffn_conv_b, ffn_wd, ln1_w, ln1_b, ln2_w, ln2_b):
    bf = lambda w: w.astype(BF16)
    rw = dict(mix=rw_mix, wr=bf(rw_wr), wk=bf(rw_wk), wv=bf(rw_wv), wo=bf(rw_wo), w0=rw_w0, w1=bf(rw_w1),
              w2=bf(rw_w2), a0=rw_a0, a1=bf(rw_a1), a2=bf(rw_a2), g1=bf(rw_g1), g2=bf(rw_g2), k_k=rw_k_k,
              k_a=rw_k_a, r_k=rw_r_k, lnx_w=rw_lnx_w, lnx_b=rw_lnx_b)
    rw_vmix = (rw_v0, bf(rw_v1), bf(rw_v2))
    hg = dict(wq=bf(hg_wq), wf=bf(hg_wf), wi=bf(hg_wi), wg=bf(hg_wg), wo=bf(hg_wo), norm_w=hg_norm_w)
    gl = dict(wq=bf(gl_wq), wk=bf(gl_wk), wv=bf(gl_wv), wg=bf(gl_wg), gk1=bf(gl_gk1), gk2=bf(gl_gk2),
              gk_b=gl_gk_b, wo=bf(gl_wo), norm_w=gl_norm_w)
    ffn = dict(wu=bf(ffn_wu), wg=bf(ffn_wg), conv_w=ffn_conv_w, conv_b=ffn_conv_b, wd=bf(ffn_wd))
    ln = dict(ln1_w=ln1_w, ln1_b=ln1_b, ln2_w=ln2_w, ln2_b=ln2_b)
    lb_soft = jax.nn.softmax(hg_lb_param, axis=0)
    lower_bounds = jnp.cumsum(lb_soft, axis=0) - lb_soft[0]

    nb = x_prompt.shape[0]
    zeros = lambda s: jnp.zeros((s.shape[0], nb) + s.shape[2:], s.dtype)
    p_out = _run_trunk(x_prompt, x_prompt.shape[1], zeros(state_rwkv), zeros(state_rwkv_shift), zeros(state_hgrn),
                       zeros(state_gla), zeros(state_ffn_conv), rw, rw_vmix, hg, lower_bounds, gl, ffn, ln, True)
    t_s = x_sample.shape[1]
    t_pad = max(SUBLANES, 1 << (t_s - 1).bit_length())
    xs = jnp.pad(x_sample, ((0, 0), (0, t_pad - t_s), (0, 0)))
    s_out = _run_trunk(xs, t_s, state_rwkv, state_rwkv_shift, state_hgrn, state_gla, state_ffn_conv,
                       rw, rw_vmix, hg, lower_bounds, gl, ffn, ln, False)
    return (p_out[0], s_out[0]) + tuple(p_out[1:]) + tuple(s_out[1:])
```

```python
import functools
import math

import jax
import jax.numpy as jnp
from jax import lax
from jax.experimental import pallas as pl
from jax.experimental.pallas import tpu as pltpu

D_MODEL = 1024
DEPTH = 4
N_MIXERS = 3
RW_HEAD = 64
RW_HEADS = D_MODEL // RW_HEAD
RW_LNX_EPS = 64e-5
HG_HEAD = 128
HG_HEADS = D_MODEL // HG_HEAD
GL_HEADS = 4
GL_DK = (D_MODEL // 2) // GL_HEADS
GL_DV = D_MODEL // GL_HEADS
GL_GATE_NORM = 16.0
CONV_W = 3
LN_EPS = 1e-5
RMS_EPS = 1e-5
DN_ALPHA = (2 * DEPTH) ** 0.25

LANES = 128
SUBLANES = 8
RW_GROUP = 256
RW_GROUP_HEADS = RW_GROUP // RW_HEAD
CHUNK = 32
RW_CHUNK = 64
RW_BATCH_PER_STEP = 2
RW_BATCH_PER_STEP_SHORT = 4
GLA_BATCH_PER_STEP = 2
SCAN_ROWS = 256
PRE_ROWS = 512
FFN_ROWS = 512
SUB_ROWS = 256
FFN_COLS = 256
VMEM_LIMIT = 56 * 1024 * 1024

BF16 = jnp.bfloat16
F32 = jnp.float32
ACT = BF16


def _dot(a, b):
    return jnp.dot(a, b, preferred_element_type=F32)


def _dot_nt(a, b):
    return lax.dot_general(a, b, (((1,), (1,)), ((), ())), preferred_element_type=F32)


def _dot_tn(a, b):
    return lax.dot_general(a, b, (((0,), (0,)), ((), ())), preferred_element_type=F32)


def _dot_hilo_rhs(sel, x):
    hi = x.astype(BF16)
    lo = (x - hi.astype(F32)).astype(BF16)
    return _dot(sel, hi) + _dot(sel, lo)


def _head_sums(x, ones_bd):
    return _dot(x.astype(BF16), ones_bd)


def _iota(shape, dim):
    return lax.broadcasted_iota(jnp.int32, shape, dim)


def _chunk_tri(n, c):
    row, col = _iota((n, n), 0), _iota((n, n), 1)
    sh = int(math.log2(c))
    keep = (col <= row) & ((row >> sh) == (col >> sh))
    return jnp.where(keep, 1.0, 0.0).astype(BF16)


def _softplus(t):
    return jnp.maximum(t, 0.0) + jnp.log(1.0 + jnp.exp(-jnp.abs(t)))


def _ln_rows(x, w, b):
    mu = jnp.mean(x, -1, keepdims=True)
    xc = x - mu
    var = jnp.mean(xc * xc, -1, keepdims=True)
    return xc * lax.rsqrt(var + LN_EPS) * w + b


def _resident(shape):
    return pl.BlockSpec(shape, lambda i, j: (0,) * len(shape), pipeline_mode=pl.Buffered(1))


def _sub_blocks(rows):
    sub = min(rows, SUB_ROWS)
    assert rows % sub == 0
    return [slice(s * sub, (s + 1) * sub) for s in range(rows // sub)]


def _params():
    return pltpu.CompilerParams(dimension_semantics=("arbitrary", "arbitrary"), vmem_limit_bytes=VMEM_LIMIT)


class _Rows:
    def __init__(self, bsz, t, valid, max_rows):
        self.bsz, self.t, self.valid = bsz, t, valid
        self.long = t >= max_rows
        if self.long:
            assert t % max_rows == 0 and valid == t
            self.rows, self.grid, self.view = max_rows, (bsz, t // max_rows), (bsz, t)
        else:
            total = bsz * t
            self.rows = min(total, max_rows)
            assert t & (t - 1) == 0 and self.rows % t == 0 and total % self.rows == 0
            self.grid, self.view = (1, total // self.rows), (1, total)

    def flat(self, x):
        return x.reshape(self.view + x.shape[2:])

    def unflat(self, x):
        return x.reshape((self.bsz, self.t) + x.shape[2:])

    def seq_spec(self, width):
        return pl.BlockSpec((None, self.rows, width), lambda i, j: (i, j, 0))

    def state_rows(self, state):
        if self.long:
            return state
        return self.flat(jnp.pad(state, ((0, 0), (0, self.t - state.shape[1]), (0, 0))))

    def state_spec(self, n, width):
        if self.long:
            return pl.BlockSpec((None, n, width), lambda i, j: (i, 0, 0))
        return self.seq_spec(width)


def _prev_rows(x, shift, rl, t_idx, carry_scr, fill_ref, n_prev):
    rows, width = x.shape
    row = _iota((rows, width), 0)
    rolled = pltpu.roll(x, shift, 0)
    if rl.long:
        out = rolled
        for i in range(shift):
            src = n_prev - shift + i
            out = jnp.where(row == i, carry_scr[src:src + 1, :], out)
        return out
    fill = fill_ref[...]
    if n_prev != shift:
        fill = pltpu.roll(fill, rows - (n_prev - shift), 0)
    return jnp.where((row & (rl.t - 1)) >= shift, rolled, fill)


def _rwkv_pre_body(*refs, rl, has_vmix):
    it = iter(refs)
    x_ref, fill_ref = next(it), next(it)
    vf_ref = next(it) if has_vmix else None
    mix_ref, wr_ref, wk_ref, wv_ref, w0_ref, w1_ref, w2_ref = (next(it) for _ in range(7))
    a0_ref, a1_ref, a2_ref, g1_ref, g2_ref, kk_ref, ka_ref = (next(it) for _ in range(7))
    v0_ref, v1_ref, v2_ref = (next(it) for _ in range(3)) if has_vmix else (None, None, None)
    r_out, wl_out, k_out, v_out, a_out, b_out, g_out, carry_scr = (next(it) for _ in range(8))

    t_idx = pl.program_id(1)
    if rl.long:
        @pl.when(t_idx == 0)
        def _load_state():
            carry_scr[0:1, :] = fill_ref[...]
    bd = (_iota((RW_GROUP, RW_GROUP), 0) >> 6) == (_iota((RW_GROUP, RW_GROUP), 1) >> 6)
    ones_bd = jnp.where(bd, 1.0, 0.0).astype(BF16)

    subs = _sub_blocks(rl.rows)
    stage1 = []
    for rs in subs:
        rows = rs.stop - rs.start
        x = x_ref[rs, :]
        x_prev = _prev_rows(x, 1, rl, t_idx, carry_scr, None if rl.long else fill_ref.at[rs, :], 1)
        if rl.long:
            carry_scr[0:1, :] = x[rows - 1:rows, :]
        xx = x_prev - x
        xr, xw, xk, xv, xa, xg = [(x + xx * mix_ref[j:j + 1, :]).astype(BF16) for j in range(6)]
        stage1.append(dict(r=_dot(xr, wr_ref[...]), k=_dot(xk, wk_ref[...]), v=_dot(xv, wv_ref[...]),
                           lw=_dot(xw, w1_ref[...]), la=_dot(xa, a1_ref[...]), lg=_dot(xg, g1_ref[...]),
                           lv=_dot(xv, v1_ref[...]) if has_vmix else None))
    for rs, s1 in zip(subs, stage1):
        rows = rs.stop - rs.start
        r, k, v = s1["r"], s1["k"], s1["v"]
        lora_w = _dot(jnp.tanh(s1["lw"]).astype(BF16), w2_ref[...])
        wl = -jnp.exp(-_softplus(-(w0_ref[...] + lora_w)) - 0.5)
        a = jax.nn.sigmoid(a0_ref[...] + _dot(s1["la"].astype(BF16), a2_ref[...]))
        if has_vmix:
            gate = jax.nn.sigmoid(v0_ref[...] + _dot(s1["lv"].astype(BF16), v2_ref[...]))
            v = v + (vf_ref[rs, :].astype(F32) - v) * gate
        g_out[rs, :] = _dot(jax.nn.sigmoid(s1["lg"]).astype(BF16), g2_ref[...]).astype(g_out.dtype)

        kk = k * kk_ref[...]
        k = k * (1.0 + (a - 1.0) * ka_ref[...])
        if not rl.long and rl.valid < rl.t:
            live = (_iota((rows, D_MODEL), 0) & (rl.t - 1)) < rl.valid
            wl, k, v, kk = (jnp.where(live, z, 0.0) for z in (wl, k, v, kk))
        r_out[rs, :] = r.astype(r_out.dtype)
        wl_out[rs, :] = wl
        k_out[rs, :] = k.astype(k_out.dtype)
        v_out[rs, :] = v.astype(v_out.dtype)
        for g in range(D_MODEL // RW_GROUP):
            lanes = slice(g * RW_GROUP, (g + 1) * RW_GROUP)
            kkg = kk[:, lanes]
            kkn = kkg / jnp.maximum(jnp.sqrt(_head_sums(kkg * kkg, ones_bd)), 1e-12)
            a_out[rs, lanes] = (-kkn).astype(a_out.dtype)
            b_out[rs, lanes] = (kkn * a[:, lanes]).astype(b_out.dtype)


def _rwkv_pre(x, shift, v_first, vmix, p, valid):
    bsz, t, d = x.shape
    rl = _Rows(bsz, t, valid, PRE_ROWS)
    has_vmix = vmix is not None
    seq = rl.seq_spec(d)
    vec = lambda z: z.reshape(1, d)
    args = [rl.flat(x), rl.state_rows(shift[:, None, :])]
    specs = [seq, rl.state_spec(1, d)]
    if has_vmix:
        args.append(rl.flat(v_first))
        specs.append(seq)
    weights = [p["mix"], p["wr"], p["wk"], p["wv"], vec(p["w0"]), p["w1"], p["w2"], vec(p["a0"]), p["a1"], p["a2"],
               p["g1"], p["g2"], vec(p["k_k"]), vec(p["k_a"])]
    if has_vmix:
        weights += [vec(vmix[0]), vmix[1], vmix[2]]
    args += weights
    specs += [_resident(w.shape) for w in weights]
    outs = pl.pallas_call(
        functools.partial(_rwkv_pre_body, rl=rl, has_vmix=has_vmix),
        grid=rl.grid,
        in_specs=specs,
        out_specs=[seq] * 7,
        out_shape=[jax.ShapeDtypeStruct(rl.view + (d,), F32 if i == 1 else ACT) for i in range(7)],
        scratch_shapes=[pltpu.VMEM((SUBLANES, d), F32)],
        compiler_params=_params(),
        name="rwkv_pre",
    )(*args)
    return [rl.unflat(o) for o in outs]


def _rwkv_body(r_ref, wl_ref, k_ref, v_ref, a_ref, b_ref, g_ref, s0_ref, rk_ref, lw_ref, lb_ref,
               y_ref, st_ref, s_scr, l_scr, *, chunk, tile, nb, zero_init):
    c = chunk
    gh = RW_GROUP_HEADS
    n_groups = D_MODEL // RW_GROUP
    t_idx = pl.program_id(1)
    log2c = int(math.log2(c))

    bd_r, bd_c = _iota((RW_GROUP, RW_GROUP), 0), _iota((RW_GROUP, RW_GROUP), 1)
    bd256 = (bd_r >> 6) == (bd_c >> 6)
    ones_bd = jnp.where(bd256, 1.0, 0.0).astype(BF16)
    hm = (_iota((gh * c, RW_GROUP), 0) >> log2c) == (_iota((gh * c, RW_GROUP), 1) >> 6)
    hm_b = jnp.where(hm, 1.0, 0.0).astype(BF16)
    bdm = (_iota((gh * c, gh * c), 0) >> log2c) == (_iota((gh * c, gh * c), 1) >> log2c)
    bdm_b = jnp.where(bdm, 1.0, 0.0).astype(BF16)
    trow = _iota((c, gh * c), 0)
    tcol = _iota((c, gh * c), 1) & (c - 1)
    strict = tcol < trow
    incl = tcol <= trow
    chains = [(n, g) for n in range(nb) for g in range(n_groups)]

    def head_block(i, n, g, h):
        blk = slice(h * RW_HEAD, (h + 1) * RW_HEAD)
        return (i, blk, blk), (n, slice(g * RW_GROUP + h * RW_HEAD, g * RW_GROUP + (h + 1) * RW_HEAD), slice(None))

    @pl.when(t_idx == 0)
    def _init():
        for i, (n, g) in enumerate(chains):
            s_scr[i] = jnp.zeros((RW_GROUP, RW_GROUP), F32)
            if not zero_init:
                for h in range(gh):
                    dst, src = head_block(i, n, g, h)
                    s_scr[dst] = s0_ref[src]

    tri = _chunk_tri(tile, c)
    for n in range(nb):
        l_scr[n] = _dot_hilo_rhs(tri, wl_ref[n])

    def stack(x_b):
        return jnp.concatenate([x_b] * gh, axis=0) * hm_b

    def bd_stack(m_b):
        return jnp.concatenate([m_b] * gh, axis=0) * bdm_b

    def chunk_step(ci, carry):
        rows = pl.ds(pl.multiple_of(ci * c, c), c)
        ld = []
        for n, g in chains:
            lanes = slice(g * RW_GROUP, (g + 1) * RW_GROUP)
            f32 = lambda ref: ref[n, rows, lanes].astype(F32)
            ld.append(dict(r=f32(r_ref), wl=wl_ref[n, rows, lanes], k=f32(k_ref), v=v_ref[n, rows, lanes].astype(BF16),
                           a=f32(a_ref), b=f32(b_ref), lc=l_scr[n, rows, lanes]))
        for i, x in enumerate(ld):
            lc = x["lc"]
            x["lend"] = lc[c - 1:c, :]
            w_inv = jnp.exp(-lc)
            at = x["a"] * jnp.exp(lc - x["wl"])
            rt = x["r"] * jnp.exp(lc)
            x["ar"] = jnp.concatenate([at, rt], axis=0).astype(BF16)
            x["bts"] = stack((x["b"] * w_inv).astype(BF16))
            x["kts"] = stack((x["k"] * w_inv).astype(BF16))
            x["vs"] = stack(x["v"])
            x["sg"] = s_scr[i]
        for x in ld:
            x["sb"] = _dot_nt(x["ar"], x["bts"])
            x["sk"] = _dot_nt(x["ar"], x["kts"])
            x["x2"] = _dot_nt(x["ar"], x["sg"].astype(BF16))
        for x, (n, g) in zip(ld, chains):
            lanes = slice(g * RW_GROUP, (g + 1) * RW_GROUP)
            x["bonus"] = _head_sums(x["r"] * x["k"] * rk_ref[:, lanes], ones_bd) * x["v"].astype(F32)
        for x in ld:
            x["p"] = jnp.where(strict, x["sb"][:c], 0.0).astype(BF16)
            x["arb"] = jnp.where(incl, x["sb"][c:], 0.0).astype(BF16)
            mak = jnp.where(strict, x["sk"][:c], 0.0).astype(BF16)
            x["ark"] = jnp.where(incl, x["sk"][c:], 0.0).astype(BF16)
            x["u"] = x["x2"][:c] + _dot(mak, x["vs"])
        for it in range(log2c):
            for x in ld:
                x["u"] = x["u"] + _dot(x["p"], stack(x["u"].astype(BF16)))
            if it < log2c - 1:
                for x in ld:
                    x["p"] = _dot(x["p"], bd_stack(x["p"])).astype(BF16)
        for x in ld:
            x["y"] = x["x2"][c:] + _dot(x["arb"], stack(x["u"].astype(BF16))) + _dot(x["ark"], x["vs"])
        for i, x in enumerate(ld):
            w_end = jnp.exp(x["lend"] - x["lc"])
            uv = jnp.concatenate([x["u"].astype(BF16), x["v"]], axis=0)
            bk = jnp.concatenate([x["b"] * w_end, x["k"] * w_end], axis=0).astype(BF16)
            s_scr[i] = x["sg"] * jnp.exp(x["lend"]) + jnp.where(bd256, _dot_tn(uv, bk), 0.0)
        inv_n = 1.0 / RW_HEAD
        for x in ld:
            x["yc"] = x["y"] - _head_sums(x["y"], ones_bd) * inv_n
        for x, (n, g) in zip(ld, chains):
            lanes = slice(g * RW_GROUP, (g + 1) * RW_GROUP)
            var = _head_sums(x["yc"] * x["yc"], ones_bd) * inv_n
            yn = x["yc"] * lax.rsqrt(var + RW_LNX_EPS) * lw_ref[:, lanes] + lb_ref[:, lanes]
            y_ref[n, rows, lanes] = ((yn + x["bonus"]) * g_ref[n, rows, lanes].astype(F32)).astype(y_ref.dtype)
        return carry

    lax.fori_loop(0, tile // c, chunk_step, 0)

    @pl.when(t_idx == pl.num_programs(1) - 1)
    def _final():
        for i, (n, g) in enumerate(chains):
            for h in range(gh):
                src, dst = head_block(i, n, g, h)
                st_ref[dst] = s_scr[src]


def _rwkv_scan(r, wl, k, v, a, b, g, s0, p, zero_init):
    bsz, t, d = r.shape
    c = min(RW_CHUNK, t)
    tile = min(t, SCAN_ROWS)
    nb = RW_BATCH_PER_STEP if t > c else RW_BATCH_PER_STEP_SHORT
    assert t % tile == 0 and tile % c == 0 and c % SUBLANES == 0 and bsz % nb == 0
    s0r = s0.reshape(bsz, d, RW_HEAD)
    seq = pl.BlockSpec((nb, tile, d), lambda i, j: (i, j, 0))
    st = pl.BlockSpec((nb, d, RW_HEAD), lambda i, j: (i, 0, 0))
    vec = _resident((1, d))
    y, s_t = pl.pallas_call(
        functools.partial(_rwkv_body, chunk=c, tile=tile, nb=nb, zero_init=zero_init),
        grid=(bsz // nb, t // tile),
        in_specs=[seq] * 7 + [st, vec, vec, vec],
        out_specs=[seq, st],
        out_shape=[jax.ShapeDtypeStruct((bsz, t, d), ACT), jax.ShapeDtypeStruct((bsz, d, RW_HEAD), F32)],
        scratch_shapes=[pltpu.VMEM((nb * (d // RW_GROUP), RW_GROUP, RW_GROUP), F32),
                        pltpu.VMEM((nb, tile, d), F32)],
        compiler_params=_params(),
        name="rwkv_scan",
    )(r, wl, k, v, a, b, g, s0r, p["r_k"].reshape(1, d), p["lnx_w"].reshape(1, d), p["lnx_b"].reshape(1, d))
    return y, s_t.reshape(bsz, RW_HEADS, RW_HEAD, RW_HEAD)


def _hgrn_pre_body(x_ref, w_ref, llb_ref, l1m_ref, omlb_ref, q_out, k_out, g_out, i_out, gate_out, *, rl):
    d = D_MODEL
    subs = _sub_blocks(rl.rows)
    ys = [_dot(x_ref[rs, :].astype(BF16), w_ref[...]) for rs in subs]
    for rs, y in reversed(list(zip(subs, ys))):
        qz, f = y[:, 0:d], y[:, d:2 * d]
        i_out[rs, :] = y[:, 2 * d:3 * d].astype(i_out.dtype)
        gate_out[rs, :] = y[:, 3 * d:4 * d].astype(gate_out.dtype)
        q_out[rs, :] = (qz * jax.nn.sigmoid(qz) * HG_HEAD ** -0.5).astype(q_out.dtype)
        la = llb_ref[...]
        lb = l1m_ref[...] - _softplus(-f)
        log_g = jnp.maximum(la, lb) + jnp.log(1.0 + jnp.exp(-jnp.abs(la - lb)))
        k = omlb_ref[...] * jax.nn.sigmoid(-f)
        if not rl.long and rl.valid < rl.t:
            live = (_iota(f.shape, 0) & (rl.t - 1)) < rl.valid
            log_g, k = jnp.where(live, log_g, 0.0), jnp.where(live, k, 0.0)
        g_out[rs, :] = log_g
        k_out[rs, :] = k.astype(k_out.dtype)


def _hgrn_pre(x, lb, p, valid):
    bsz, t, d = x.shape
    rl = _Rows(bsz, t, valid, PRE_ROWS)
    seq = rl.seq_spec(d)
    vecs = [jnp.log(lb).reshape(1, d), jnp.log1p(-lb).reshape(1, d), (1.0 - lb).reshape(1, d)]
    weights = [jnp.concatenate([p["wq"], p["wf"], p["wi"], p["wg"]], axis=1)] + vecs
    outs = pl.pallas_call(
        functools.partial(_hgrn_pre_body, rl=rl),
        grid=rl.grid,
        in_specs=[seq] + [_resident(w.shape) for w in weights],
        out_specs=[seq] * 5,
        out_shape=[jax.ShapeDtypeStruct(rl.view + (d,), F32 if i == 2 else ACT) for i in range(5)],
        compiler_params=_params(),
        name="hgrn_pre",
    )(rl.flat(x), *weights)
    return [rl.unflat(o) for o in outs]


def _gla_pre_body(x_ref, w_ref, gk2_ref, gkb_ref, q_out, k_out, g_out, v_out, gate_out, *, rl):
    d, dkk = D_MODEL, GL_HEADS * GL_DK
    subs = _sub_blocks(rl.rows)
    ys = [_dot(x_ref[rs, :].astype(BF16), w_ref[...]) for rs in subs]
    for rs, y in zip(subs, ys):
        q_out[rs, :] = (y[:, 0:dkk] * GL_DK ** -0.5).astype(q_out.dtype)
        k = y[:, dkk:2 * dkk]
        v_out[rs, :] = y[:, 2 * dkk:2 * dkk + d].astype(v_out.dtype)
        gate_out[rs, :] = y[:, 2 * dkk + d:2 * dkk + 2 * d].astype(gate_out.dtype)
        gk = _dot(y[:, 2 * dkk + 2 * d:].astype(BF16), gk2_ref[...]) + gkb_ref[...]
        log_g = -_softplus(-gk) * (1.0 / GL_GATE_NORM)
        if not rl.long and rl.valid < rl.t:
            live = (_iota(k.shape, 0) & (rl.t - 1)) < rl.valid
            log_g, k = jnp.where(live, log_g, 0.0), jnp.where(live, k, 0.0)
        g_out[rs, :] = log_g
        k_out[rs, :] = k.astype(k_out.dtype)


def _gla_pre(x, p, valid):
    bsz, t, d = x.shape
    dkk = GL_HEADS * GL_DK
    rl = _Rows(bsz, t, valid, PRE_ROWS)
    rank = p["gk1"].shape[1]
    gk1 = jnp.pad(p["gk1"], ((0, 0), (0, LANES - rank)))
    gk2 = jnp.pad(p["gk2"], ((0, LANES - rank), (0, 0)))
    weights = [jnp.concatenate([p["wq"], p["wk"], p["wv"], p["wg"], gk1], axis=1), gk2, p["gk_b"].reshape(1, dkk)]
    kseq, vseq = rl.seq_spec(dkk), rl.seq_spec(d)
    kshape, vshape = jax.ShapeDtypeStruct(rl.view + (dkk,), ACT), jax.ShapeDtypeStruct(rl.view + (d,), ACT)
    outs = pl.pallas_call(
        functools.partial(_gla_pre_body, rl=rl),
        grid=rl.grid,
        in_specs=[vseq] + [_resident(w.shape) for w in weights],
        out_specs=[kseq, kseq, kseq, vseq, vseq],
        out_shape=[kshape, kshape, jax.ShapeDtypeStruct(rl.view + (dkk,), F32), vshape, vshape],
        compiler_params=_params(),
        name="gla_pre",
    )(rl.flat(x), *weights)
    return [rl.unflat(o) for o in outs]


def _gla_body(q_ref, k_ref, g_ref, v_ref, gate_ref, s0_ref, gain_ref, o_ref, st_ref, s_scr, l_scr,
              *, chunk, tile, nb, heads, dk, dv, zero_init):
    c = chunk
    ref_row = (c - 1) // 2
    t_idx = pl.program_id(1)
    incl = _iota((c, c), 1) <= _iota((c, c), 0)
    chains = [(n, h) for n in range(nb) for h in range(heads)]

    @pl.when(t_idx == 0)
    def _init():
        for i, (n, h) in enumerate(chains):
            s_scr[i] = jnp.zeros((dv, dk), F32) if zero_init else s0_ref[n, h].T

    tri = _chunk_tri(tile, c)
    for n in range(nb):
        l_scr[n] = _dot_hilo_rhs(tri, g_ref[n])

    def chunk_step(ci, carry):
        rows = pl.ds(pl.multiple_of(ci * c, c), c)
        ld = []
        for i, (n, h) in enumerate(chains):
            kl = slice(h * dk, (h + 1) * dk)
            q, k = q_ref[n, rows, kl].astype(F32), k_ref[n, rows, kl].astype(F32)
            bc = l_scr[n, rows, kl]
            b_ref = bc[ref_row:ref_row + 1, :]
            b_last = bc[c - 1:c, :]
            ld.append(dict(qi=(q * jnp.exp(bc - b_ref)).astype(BF16), ki=(k * jnp.exp(b_ref - bc)).astype(BF16),
                           qd=(q * jnp.exp(bc)).astype(BF16), kd=(k * jnp.exp(b_last - bc)).astype(BF16),
                           v=v_ref[n, rows, h * dv:(h + 1) * dv].astype(BF16), w_last=jnp.exp(b_last),
                           st=s_scr[i]))
        for x in ld:
            x["att"] = _dot_nt(x["qi"], x["ki"])
            x["os"] = _dot_nt(x["qd"], x["st"].astype(BF16))
            x["ds"] = _dot_tn(x["v"], x["kd"])
        for i, (x, (n, h)) in enumerate(zip(ld, chains)):
            vl = slice(h * dv, (h + 1) * dv)
            s_scr[i] = x["st"] * x["w_last"] + x["ds"]
            o = x["os"] + _dot(jnp.where(incl, x["att"], 0.0).astype(BF16), x["v"])
            gate = gate_ref[n, rows, vl].astype(F32)
            on = o * lax.rsqrt(jnp.mean(o * o, -1, keepdims=True) + RMS_EPS) * gain_ref[...]
            o_ref[n, rows, vl] = (on * (gate * jax.nn.sigmoid(gate))).astype(o_ref.dtype)
        return carry

    lax.fori_loop(0, tile // c, chunk_step, 0)

    @pl.when(t_idx == pl.num_programs(1) - 1)
    def _final():
        for i, (n, h) in enumerate(chains):
            st_ref[n, h] = s_scr[i].T


def _gla_scan(q, k, g, v, gate, s0, gain, heads, dk, dv, zero_init):
    bsz, t, _ = q.shape
    c = min(CHUNK, t)
    tile = min(t, SCAN_ROWS)
    nb = GLA_BATCH_PER_STEP
    assert t % tile == 0 and tile % c == 0 and c % SUBLANES == 0 and bsz % nb == 0
    kseq = pl.BlockSpec((nb, tile, heads * dk), lambda i, j: (i, j, 0))
    vseq = pl.BlockSpec((nb, tile, heads * dv), lambda i, j: (i, j, 0))
    st = pl.BlockSpec((nb, heads, dk, dv), lambda i, j: (i, 0, 0, 0))
    return pl.pallas_call(
        functools.partial(_gla_body, chunk=c, tile=tile, nb=nb, heads=heads, dk=dk, dv=dv, zero_init=zero_init),
        grid=(bsz // nb, t // tile),
        in_specs=[kseq, kseq, kseq, vseq, vseq, st, _resident((1, dv))],
        out_specs=[vseq, st],
        out_shape=[jax.ShapeDtypeStruct((bsz, t, heads * dv), ACT),
                   jax.ShapeDtypeStruct((bsz, heads, dk, dv), F32)],
        scratch_shapes=[pltpu.VMEM((nb * heads, dv, dk), F32), pltpu.VMEM((nb, tile, heads * dk), F32)],
        compiler_params=_params(),
        name="gla_scan",
    )(q, k, g, v, gate, s0, gain.reshape(1, dv))


def _cf_body(x_ref, o_ref, fill_ref, wo_ref, l1w_ref, l1b_ref, wu_ref, wg_ref, wd_ref, cw_ref, cb_ref,
             l2w_ref, l2b_ref, out_ref, z_ref, h_scr, carry_scr, *, rl):
    n_state = CONV_W - 1
    t_idx = pl.program_id(1)
    subs = _sub_blocks(rl.rows)
    sub = subs[0].stop
    if rl.long:
        @pl.when(t_idx == 0)
        def _load_state():
            carry_scr[...] = fill_ref[...]

    x1s = [_ln_rows(DN_ALPHA * x_ref[rs, :] + _dot(o_ref[rs, :].astype(BF16), wo_ref[...]), l1w_ref[...], l1b_ref[...])
           for rs in subs]
    for rs, x1 in zip(subs, x1s):
        x1b = x1.astype(BF16)
        for j in range(wu_ref.shape[1] // FFN_COLS):
            cs = slice(j * FFN_COLS, (j + 1) * FFN_COLS)
            zc = _dot(x1b, wg_ref[:, cs])
            uc = _dot(x1b, wu_ref[:, cs])
            carry_c = carry_scr.at[:, cs] if rl.long else None
            fill_c = None if rl.long else fill_ref.at[rs, cs]
            zp1 = _prev_rows(zc, 1, rl, t_idx, carry_c, fill_c, n_state)
            zp2 = _prev_rows(zc, 2, rl, t_idx, carry_c, fill_c, n_state)
            if rl.long:
                carry_scr[:, cs] = zc[sub - n_state:sub, :]
            else:
                z_ref[rs, cs] = zc
            cw = cw_ref[:, cs]
            pre = cb_ref[:, cs] + cw[0:1] * zp2 + cw[1:2] * zp1 + cw[2:3] * zc
            h_scr[rs, cs] = (pre * jax.nn.sigmoid(pre) * uc).astype(BF16)
        out_ref[rs, :] = _ln_rows(DN_ALPHA * x1 + _dot(h_scr[rs, :], wd_ref[...]), l2w_ref[...], l2b_ref[...])
    if rl.long:
        @pl.when(t_idx == pl.num_programs(1) - 1)
        def _store_state():
            z_ref[...] = carry_scr[...]


def _proj_ffn(x, o, buf, wo, ln1, p, ln2, valid):
    bsz, t, d = x.shape
    f = p["wu"].shape[1]
    n_state = CONV_W - 1
    rl = _Rows(bsz, t, valid, FFN_ROWS if t >= FFN_ROWS else FFN_ROWS // 2)
    assert f % FFN_COLS == 0 and valid >= n_state
    seq = rl.seq_spec(d)
    vec = lambda n: _resident((1, n))
    if rl.long:
        z_spec, z_shape = pl.BlockSpec((None, n_state, f), lambda i, j: (i, 0, 0)), (bsz, n_state, f)
    else:
        z_spec, z_shape = rl.seq_spec(f), rl.view + (f,)
    out, z = pl.pallas_call(
        functools.partial(_cf_body, rl=rl),
        grid=rl.grid,
        in_specs=[seq, seq, rl.state_spec(n_state, f),
                  _resident((d, d)), vec(d), vec(d), _resident((d, f)), _resident((d, f)), _resident((f, d)),
                  _resident((CONV_W, f)), vec(f), vec(d), vec(d)],
        out_specs=[seq, z_spec],
        out_shape=[jax.ShapeDtypeStruct(rl.view + (d,), F32), jax.ShapeDtypeStruct(z_shape, F32)],
        scratch_shapes=[pltpu.VMEM((rl.rows, f), BF16), pltpu.VMEM((n_state, f), F32)],
        compiler_params=_params(),
        name="proj_ffn",
    )(rl.flat(x), rl.flat(o), rl.state_rows(buf), wo, ln1[0].reshape(1, d), ln1[1].reshape(1, d),
      p["wu"], p["wg"], p["wd"], p["conv_w"], p["conv_b"].reshape(1, f), ln2[0].reshape(1, d), ln2[1].reshape(1, d))
    if rl.long:
        return out, z
    return rl.unflat(out), rl.unflat(z)[:, valid - n_state:valid]


def _run_trunk(x, valid, st_rw, st_shift, st_hg, st_gl, st_conv, rw, rw_vmix, hg, lower_bounds, gl, ffn, ln, zero_init):
    new_rw, new_shift, new_hg, new_gl, new_conv = [], [], [], [], []
    v_first = None
    for i in range(DEPTH):
        j = i // N_MIXERS
        kind = i % N_MIXERS
        if kind == 0:
            p = {n: q[j] for n, q in rw.items()}
            vm = None if j == 0 else tuple(q[j - 1] for q in rw_vmix)
            r, wl, k, v, a, b, g = _rwkv_pre(x, st_shift[j], v_first, vm, p, valid)
            if vm is None:
                v_first = v
            o, s_t = _rwkv_scan(r, wl, k, v, a, b, g, st_rw[j], p, zero_init)
            new_rw.append(s_t)
            new_shift.append(x[:, valid - 1])
        elif kind == 1:
            p = {n: q[j] for n, q in hg.items()}
            q_, k, log_g, val, gate = _hgrn_pre(x, lower_bounds[i], p, valid)
            o, s_t = _gla_scan(q_, k, log_g, val, gate, st_hg[j], p["norm_w"], HG_HEADS, HG_HEAD, HG_HEAD, zero_init)
            new_hg.append(s_t)
        else:
            p = {n: q[j] for n, q in gl.items()}
            q_, k, log_g, val, gate = _gla_pre(x, p, valid)
            o, s_t = _gla_scan(q_, k, log_g, val, gate, st_gl[j], p["norm_w"], GL_HEADS, GL_DK, GL_DV, zero_init)
            new_gl.append(s_t)
        x, buf = _proj_ffn(x, o, st_conv[i], p["wo"], (ln["ln1_w"][i], ln["ln1_b"][i]),
                           {n: q[i] for n, q in ffn.items()}, (ln["ln2_w"][i], ln["ln2_b"][i]), valid)
        new_conv.append(buf)
    return (x[:, :valid], jnp.stack(new_rw), jnp.stack(new_shift), jnp.stack(new_hg), jnp.stack(new_gl),
            jnp.stack(new_conv))


def kernel(x_prompt, x_sample, state_rwkv, state_rwkv_shift, state_hgrn, state_gla, state_ffn_conv, rw_mix, rw_wr, rw_wk, rw_wv, rw_wo, rw_w0, rw_w1, rw_w2, rw_a0, rw_a1, rw_a2, rw_g1, rw_g2, rw_k_k, rw_k_a, rw_r_k, rw_lnx_w, rw_lnx_b, rw_v0, rw_v1, rw_v2, hg_wq, hg_wf, hg_wi, hg_wg, hg_wo, hg_norm_w, hg_lb_param, gl_wq, gl_wk, gl_wv, gl_wg, gl_gk1, gl_gk2, gl_gk_b, gl_wo, gl_norm_w, ffn_wu, ffn_wg, ffn_conv_w, ffn_conv_b, ffn_wd, ln1_w, ln1_b, ln2_w, ln2_b):
    bf = lambda w: w.astype(BF16)
    rw = dict(mix=rw_mix, wr=bf(rw_wr), wk=bf(rw_wk), wv=bf(rw_wv), wo=bf(rw_wo), w0=rw_w0, w1=bf(rw_w1),
              w2=bf(rw_w2), a0=rw_a0, a1=bf(rw_a1), a2=bf(rw_a2), g1=bf(rw_g1), g2=bf(rw_g2), k_k=rw_k_k,
              k_a=rw_k_a, r_k=rw_r_k, lnx_w=rw_lnx_w, lnx_b=rw_lnx_b)
    rw_vmix = (rw_v0, bf(rw_v1), bf(rw_v2))
    hg = dict(wq=bf(hg_wq), wf=bf(hg_wf), wi=bf(hg_wi), wg=bf(hg_wg), wo=bf(hg_wo), norm_w=hg_norm_w)
    gl = dict(wq=bf(gl_wq), wk=bf(gl_wk), wv=bf(gl_wv), wg=bf(gl_wg), gk1=bf(gl_gk1), gk2=bf(gl_gk2),
              gk_b=gl_gk_b, wo=bf(gl_wo), norm_w=gl_norm_w)
    ffn = dict(wu=bf(ffn_wu), wg=bf(ffn_wg), conv_w=ffn_conv_w, conv_b=ffn_conv_b, wd=bf(ffn_wd))
    ln = dict(ln1_w=ln1_w, ln1_b=ln1_b, ln2_w=ln2_w, ln2_b=ln2_b)
    lb_soft = jax.nn.softmax(hg_lb_param, axis=0)
    lower_bounds = jnp.cumsum(lb_soft, axis=0) - lb_soft[0]

    nb = x_prompt.shape[0]
    zeros = lambda s: jnp.zeros((s.shape[0], nb) + s.shape[2:], s.dtype)
    p_out = _run_trunk(x_prompt, x_prompt.shape[1], zeros(state_rwkv), zeros(state_rwkv_shift), zeros(state_hgrn),
                       zeros(state_gla), zeros(state_ffn_conv), rw, rw_vmix, hg, lower_bounds, gl, ffn, ln, True)
    t_s = x_sample.shape[1]
    t_pad = max(SUBLANES, 1 << (t_s - 1).bit_length())
    xs = jnp.pad(x_sample, ((0, 0), (0, t_pad - t_s), (0, 0)))
    s_out = _run_trunk(xs, t_s, state_rwkv, state_rwkv_shift, state_hgrn, state_gla, state_ffn_conv,
                       rw, rw_vmix, hg, lower_bounds, gl, ffn, ln, False)
    return (p_out[0], s_out[0]) + tuple(p_out[1:]) + tuple(s_out[1:])
```

```python
import functools
import math

import jax
import jax.numpy as jnp
from jax import lax
from jax.experimental import pallas as pl
from jax.experimental.pallas import tpu as pltpu

D_MODEL = 1024
DEPTH = 4
N_MIXERS = 3
RW_HEAD = 64
RW_HEADS = D_MODEL // RW_HEAD
RW_HEAD_SHIFT = RW_HEAD.bit_length() - 1
RW_LNX_EPS = 64e-5
HG_HEAD = 128
HG_HEADS = D_MODEL // HG_HEAD
GL_HEADS = 4
GL_DK = (D_MODEL // 2) // GL_HEADS
GL_DV = D_MODEL // GL_HEADS
GL_GATE_NORM = 16.0
CONV_W = 3
LN_EPS = 1e-5
RMS_EPS = 1e-5
DN_ALPHA = (2 * DEPTH) ** 0.25

LANES = 128
SUBLANES = 8
RW_GROUP = 256
RW_GROUP_HEADS = RW_GROUP // RW_HEAD
CHUNK = 32
RW_CHUNK = 64
RW_BATCH_PER_STEP = 2
RW_BATCH_PER_STEP_SHORT = 4
GLA_BATCH_PER_STEP = 4
SCAN_ROWS = 256
PRE_ROWS = 512
FFN_ROWS = 512
SUB_ROWS = 256
FFN_COLS = 256
VMEM_LIMIT = 56 * 1024 * 1024

BF16 = jnp.bfloat16
F32 = jnp.float32
ACT = BF16


def _dot(a, b):
    return jnp.dot(a, b, preferred_element_type=F32)


def _dot_nt(a, b):
    return lax.dot_general(a, b, (((1,), (1,)), ((), ())), preferred_element_type=F32)


def _dot_tn(a, b):
    return lax.dot_general(a, b, (((0,), (0,)), ((), ())), preferred_element_type=F32)


def _dot_hilo_rhs(sel, x):
    hi = x.astype(BF16)
    lo = (x - hi.astype(F32)).astype(BF16)
    return _dot(sel, hi) + _dot(sel, lo)


def _head_sums(x, ones_bd):
    return _dot(x.astype(BF16), ones_bd)


def _iota(shape, dim):
    return lax.broadcasted_iota(jnp.int32, shape, dim)


def _chunk_tri(n, c):
    row, col = _iota((n, n), 0), _iota((n, n), 1)
    sh = int(math.log2(c))
    keep = (col <= row) & ((row >> sh) == (col >> sh))
    return jnp.where(keep, 1.0, 0.0).astype(BF16)


def _softplus(t):
    return jnp.maximum(t, 0.0) + jnp.log(1.0 + jnp.exp(-jnp.abs(t)))


def _ln_rows(x, w, b):
    mu = jnp.mean(x, -1, keepdims=True)
    xc = x - mu
    var = jnp.mean(xc * xc, -1, keepdims=True)
    return xc * lax.rsqrt(var + LN_EPS) * w + b


def _resident(shape):
    return pl.BlockSpec(shape, lambda i, j: (0,) * len(shape), pipeline_mode=pl.Buffered(1))


def _sub_blocks(rows):
    sub = min(rows, SUB_ROWS)
    assert rows % sub == 0
    return [slice(s * sub, (s + 1) * sub) for s in range(rows // sub)]


def _params():
    return pltpu.CompilerParams(dimension_semantics=("arbitrary", "arbitrary"), vmem_limit_bytes=VMEM_LIMIT)


class _Rows:
    def __init__(self, bsz, t, valid, max_rows):
        self.bsz, self.t, self.valid = bsz, t, valid
        self.long = t >= max_rows
        if self.long:
            assert t % max_rows == 0 and valid == t
            self.rows, self.grid, self.view = max_rows, (bsz, t // max_rows), (bsz, t)
        else:
            total = bsz * t
            self.rows = min(total, max_rows)
            assert t & (t - 1) == 0 and self.rows % t == 0 and total % self.rows == 0
            self.grid, self.view = (1, total // self.rows), (1, total)

    def flat(self, x):
        return x.reshape(self.view + x.shape[2:])

    def unflat(self, x):
        return x.reshape((self.bsz, self.t) + x.shape[2:])

    def seq_spec(self, width):
        return pl.BlockSpec((None, self.rows, width), lambda i, j: (i, j, 0))

    def state_rows(self, state):
        if self.long:
            return state
        return self.flat(jnp.pad(state, ((0, 0), (0, self.t - state.shape[1]), (0, 0))))

    def state_spec(self, n, width):
        if self.long:
            return pl.BlockSpec((None, n, width), lambda i, j: (i, 0, 0))
        return self.seq_spec(width)


def _prev_rows(x, shift, rl, t_idx, carry_scr, fill_ref, n_prev):
    rows, width = x.shape
    row = _iota((rows, width), 0)
    rolled = pltpu.roll(x, shift, 0)
    if rl.long:
        out = rolled
        for i in range(shift):
            src = n_prev - shift + i
            out = jnp.where(row == i, carry_scr[src:src + 1, :], out)
        return out
    fill = fill_ref[...]
    if n_prev != shift:
        fill = pltpu.roll(fill, rows - (n_prev - shift), 0)
    return jnp.where((row & (rl.t - 1)) >= shift, rolled, fill)


def _rwkv_pre_body(*refs, rl, has_vmix):
    it = iter(refs)
    x_ref, fill_ref = next(it), next(it)
    vf_ref = next(it) if has_vmix else None
    mix_ref, wr_ref, wk_ref, wv_ref, w0_ref, w1_ref, w2_ref = (next(it) for _ in range(7))
    a0_ref, a1_ref, a2_ref, g1_ref, g2_ref, kk_ref, ka_ref = (next(it) for _ in range(7))
    v0_ref, v1_ref, v2_ref = (next(it) for _ in range(3)) if has_vmix else (None, None, None)
    r_out, wl_out, k_out, v_out, a_out, b_out, g_out, carry_scr = (next(it) for _ in range(8))

    t_idx = pl.program_id(1)
    if rl.long:
        @pl.when(t_idx == 0)
        def _load_state():
            carry_scr[0:1, :] = fill_ref[...]
    bd = (_iota((RW_GROUP, RW_GROUP), 0) >> RW_HEAD_SHIFT) == (_iota((RW_GROUP, RW_GROUP), 1) >> RW_HEAD_SHIFT)
    ones_bd = jnp.where(bd, 1.0, 0.0).astype(BF16)

    subs = _sub_blocks(rl.rows)
    stage1 = []
    for rs in subs:
        rows = rs.stop - rs.start
        x = x_ref[rs, :]
        x_prev = _prev_rows(x, 1, rl, t_idx, carry_scr, None if rl.long else fill_ref.at[rs, :], 1)
        if rl.long:
            carry_scr[0:1, :] = x[rows - 1:rows, :]
        xx = x_prev - x
        xr, xw, xk, xv, xa, xg = [(x + xx * mix_ref[j:j + 1, :]).astype(BF16) for j in range(6)]
        stage1.append(dict(r=_dot(xr, wr_ref[...]), k=_dot(xk, wk_ref[...]), v=_dot(xv, wv_ref[...]),
                           lw=_dot(xw, w1_ref[...]), la=_dot(xa, a1_ref[...]), lg=_dot(xg, g1_ref[...]),
                           lv=_dot(xv, v1_ref[...]) if has_vmix else None))
    for rs, s1 in zip(subs, stage1):
        rows = rs.stop - rs.start
        r, k, v = s1["r"], s1["k"], s1["v"]
        lora_w = _dot(jnp.tanh(s1["lw"]).astype(BF16), w2_ref[...])
        wl = -jnp.exp(-_softplus(-(w0_ref[...] + lora_w)) - 0.5)
        a = jax.nn.sigmoid(a0_ref[...] + _dot(s1["la"].astype(BF16), a2_ref[...]))
        if has_vmix:
            gate = jax.nn.sigmoid(v0_ref[...] + _dot(s1["lv"].astype(BF16), v2_ref[...]))
            v = v + (vf_ref[rs, :].astype(F32) - v) * gate
        g_out[rs, :] = _dot(jax.nn.sigmoid(s1["lg"]).astype(BF16), g2_ref[...]).astype(g_out.dtype)

        kk = k * kk_ref[...]
        k = k * (1.0 + (a - 1.0) * ka_ref[...])
        if not rl.long and rl.valid < rl.t:
            live = (_iota((rows, D_MODEL), 0) & (rl.t - 1)) < rl.valid
            wl, k, v, kk = (jnp.where(live, z, 0.0) for z in (wl, k, v, kk))
        r_out[rs, :] = r.astype(r_out.dtype)
        wl_out[rs, :] = wl
        k_out[rs, :] = k.astype(k_out.dtype)
        v_out[rs, :] = v.astype(v_out.dtype)
        for g in range(D_MODEL // RW_GROUP):
            lanes = slice(g * RW_GROUP, (g + 1) * RW_GROUP)
            kkg = kk[:, lanes]
            kkn = kkg / jnp.maximum(jnp.sqrt(_head_sums(kkg * kkg, ones_bd)), 1e-12)
            a_out[rs, lanes] = (-kkn).astype(a_out.dtype)
            b_out[rs, lanes] = (kkn * a[:, lanes]).astype(b_out.dtype)


def _rwkv_pre(x, shift, v_first, vmix, p, valid):
    bsz, t, d = x.shape
    rl = _Rows(bsz, t, valid, PRE_ROWS)
    has_vmix = vmix is not None
    seq = rl.seq_spec(d)
    vec = lambda z: z.reshape(1, d)
    args = [rl.flat(x), rl.state_rows(shift[:, None, :])]
    specs = [seq, rl.state_spec(1, d)]
    if has_vmix:
        args.append(rl.flat(v_first))
        specs.append(seq)
    weights = [p["mix"], p["wr"], p["wk"], p["wv"], vec(p["w0"]), p["w1"], p["w2"], vec(p["a0"]), p["a1"], p["a2"],
               p["g1"], p["g2"], vec(p["k_k"]), vec(p["k_a"])]
    if has_vmix:
        weights += [vec(vmix[0]), vmix[1], vmix[2]]
    args += weights
    specs += [_resident(w.shape) for w in weights]
    outs = pl.pallas_call(
        functools.partial(_rwkv_pre_body, rl=rl, has_vmix=has_vmix),
        grid=rl.grid,
        in_specs=specs,
        out_specs=[seq] * 7,
        out_shape=[jax.ShapeDtypeStruct(rl.view + (d,), F32 if i == 1 else ACT) for i in range(7)],
        scratch_shapes=[pltpu.VMEM((SUBLANES, d), F32)],
        compiler_params=_params(),
        name="rwkv_pre",
    )(*args)
    return [rl.unflat(o) for o in outs]


def _rwkv_body(r_ref, wl_ref, k_ref, v_ref, a_ref, b_ref, g_ref, s0_ref, rk_ref, lw_ref, lb_ref,
               y_ref, st_ref, s_scr, l_scr, *, chunk, tile, nb, zero_init):
    c = chunk
    gh = RW_GROUP_HEADS
    n_groups = D_MODEL // RW_GROUP
    t_idx = pl.program_id(1)
    log2c = int(math.log2(c))

    bd_r, bd_c = _iota((RW_GROUP, RW_GROUP), 0), _iota((RW_GROUP, RW_GROUP), 1)
    bd256 = (bd_r >> RW_HEAD_SHIFT) == (bd_c >> RW_HEAD_SHIFT)
    ones_bd = jnp.where(bd256, 1.0, 0.0).astype(BF16)
    hm = (_iota((gh * c, RW_GROUP), 0) >> log2c) == (_iota((gh * c, RW_GROUP), 1) >> RW_HEAD_SHIFT)
    hm_b = jnp.where(hm, 1.0, 0.0).astype(BF16)
    bdm = (_iota((gh * c, gh * c), 0) >> log2c) == (_iota((gh * c, gh * c), 1) >> log2c)
    bdm_b = jnp.where(bdm, 1.0, 0.0).astype(BF16)
    trow = _iota((c, gh * c), 0)
    tcol = _iota((c, gh * c), 1) & (c - 1)
    strict = tcol < trow
    incl = tcol <= trow
    chains = [(n, g) for n in range(nb) for g in range(n_groups)]

    def head_block(i, n, g, h):
        blk = slice(h * RW_HEAD, (h + 1) * RW_HEAD)
        return (i, blk, blk), (n, slice(g * RW_GROUP + h * RW_HEAD, g * RW_GROUP + (h + 1) * RW_HEAD), slice(None))

    @pl.when(t_idx == 0)
    def _init():
        for i, (n, g) in enumerate(chains):
            s_scr[i] = jnp.zeros((RW_GROUP, RW_GROUP), F32)
            if not zero_init:
                for h in range(gh):
                    dst, src = head_block(i, n, g, h)
                    s_scr[dst] = s0_ref[src]

    tri = _chunk_tri(tile, c)
    for n in range(nb):
        l_scr[n] = _dot_hilo_rhs(tri, wl_ref[n])

    def stack(x_b):
        return jnp.concatenate([x_b] * gh, axis=0) * hm_b

    def bd_stack(m_b):
        return jnp.concatenate([m_b] * gh, axis=0) * bdm_b

    def chunk_step(ci, carry):
        rows = pl.ds(pl.multiple_of(ci * c, c), c)
        ld = []
        for n, g in chains:
            lanes = slice(g * RW_GROUP, (g + 1) * RW_GROUP)
            f32 = lambda ref: ref[n, rows, lanes].astype(F32)
            ld.append(dict(r=f32(r_ref), wl=wl_ref[n, rows, lanes], k=f32(k_ref), v=v_ref[n, rows, lanes].astype(BF16),
                           a=f32(a_ref), b=f32(b_ref), lc=l_scr[n, rows, lanes]))
        for i, x in enumerate(ld):
            lc = x["lc"]
            x["lend"] = lc[c - 1:c, :]
            w_inv = jnp.exp(-lc)
            at = x["a"] * jnp.exp(lc - x["wl"])
            rt = x["r"] * jnp.exp(lc)
            x["ar"] = jnp.concatenate([at, rt], axis=0).astype(BF16)
            x["bts"] = stack((x["b"] * w_inv).astype(BF16))
            x["kts"] = stack((x["k"] * w_inv).astype(BF16))
            x["vs"] = stack(x["v"])
            x["sg"] = s_scr[i]
        for x in ld:
            x["sb"] = _dot_nt(x["ar"], x["bts"])
            x["sk"] = _dot_nt(x["ar"], x["kts"])
            x["x2"] = _dot_nt(x["ar"], x["sg"].astype(BF16))
        for x, (n, g) in zip(ld, chains):
            lanes = slice(g * RW_GROUP, (g + 1) * RW_GROUP)
            x["bonus"] = _head_sums(x["r"] * x["k"] * rk_ref[:, lanes], ones_bd) * x["v"].astype(F32)
        for x in ld:
            x["p"] = jnp.where(strict, x["sb"][:c], 0.0).astype(BF16)
            x["arb"] = jnp.where(incl, x["sb"][c:], 0.0).astype(BF16)
            kv = _dot(jnp.where(jnp.concatenate([strict, incl], axis=0), x["sk"], 0.0).astype(BF16), x["vs"])
            x["u"] = x["x2"][:c] + kv[:c]
            x["ykv"] = x["x2"][c:] + kv[c:]
        for it in range(log2c):
            for x in ld:
                x["u"] = x["u"] + _dot(x["p"], stack(x["u"].astype(BF16)))
            if it < log2c - 1:
                for x in ld:
                    x["p"] = _dot(x["p"], bd_stack(x["p"])).astype(BF16)
        for x in ld:
            x["y"] = x["ykv"] + _dot(x["arb"], stack(x["u"].astype(BF16)))
        for i, x in enumerate(ld):
            w_end = jnp.exp(x["lend"] - x["lc"])
            uv = jnp.concatenate([x["u"].astype(BF16), x["v"]], axis=0)
            bk = jnp.concatenate([x["b"] * w_end, x["k"] * w_end], axis=0).astype(BF16)
            s_scr[i] = x["sg"] * jnp.exp(x["lend"]) + jnp.where(bd256, _dot_tn(uv, bk), 0.0)
        inv_n = 1.0 / RW_HEAD
        for x in ld:
            x["yc"] = x["y"] - _head_sums(x["y"], ones_bd) * inv_n
        for x, (n, g) in zip(ld, chains):
            lanes = slice(g * RW_GROUP, (g + 1) * RW_GROUP)
            var = _head_sums(x["yc"] * x["yc"], ones_bd) * inv_n
            yn = x["yc"] * lax.rsqrt(var + RW_LNX_EPS) * lw_ref[:, lanes] + lb_ref[:, lanes]
            y_ref[n, rows, lanes] = ((yn + x["bonus"]) * g_ref[n, rows, lanes].astype(F32)).astype(y_ref.dtype)
        return carry

    lax.fori_loop(0, tile // c, chunk_step, 0)

    @pl.when(t_idx == pl.num_programs(1) - 1)
    def _final():
        for i, (n, g) in enumerate(chains):
            for h in range(gh):
                src, dst = head_block(i, n, g, h)
                st_ref[dst] = s_scr[src]


def _rwkv_scan(r, wl, k, v, a, b, g, s0, p, zero_init):
    bsz, t, d = r.shape
    c = min(RW_CHUNK, t)
    tile = min(t, SCAN_ROWS)
    nb = RW_BATCH_PER_STEP if t > c else RW_BATCH_PER_STEP_SHORT
    assert t % tile == 0 and tile % c == 0 and c % SUBLANES == 0 and bsz % nb == 0
    s0r = s0.reshape(bsz, d, RW_HEAD)
    seq = pl.BlockSpec((nb, tile, d), lambda i, j: (i, j, 0))
    st = pl.BlockSpec((nb, d, RW_HEAD), lambda i, j: (i, 0, 0))
    vec = _resident((1, d))
    y, s_t = pl.pallas_call(
        functools.partial(_rwkv_body, chunk=c, tile=tile, nb=nb, zero_init=zero_init),
        grid=(bsz // nb, t // tile),
        in_specs=[seq] * 7 + [st, vec, vec, vec],
        out_specs=[seq, st],
        out_shape=[jax.ShapeDtypeStruct((bsz, t, d), ACT), jax.ShapeDtypeStruct((bsz, d, RW_HEAD), F32)],
        scratch_shapes=[pltpu.VMEM((nb * (d // RW_GROUP), RW_GROUP, RW_GROUP), F32),
                        pltpu.VMEM((nb, tile, d), F32)],
        compiler_params=_params(),
        name="rwkv_scan",
    )(r, wl, k, v, a, b, g, s0r, p["r_k"].reshape(1, d), p["lnx_w"].reshape(1, d), p["lnx_b"].reshape(1, d))
    return y, s_t.reshape(bsz, RW_HEADS, RW_HEAD, RW_HEAD)


def _hgrn_pre_body(x_ref, w_ref, llb_ref, l1m_ref, omlb_ref, q_out, k_out, g_out, i_out, gate_out, *, rl):
    d = D_MODEL
    subs = _sub_blocks(rl.rows)
    ys = [_dot(x_ref[rs, :].astype(BF16), w_ref[...]) for rs in subs]
    for rs, y in reversed(list(zip(subs, ys))):
        qz, f = y[:, 0:d], y[:, d:2 * d]
        i_out[rs, :] = y[:, 2 * d:3 * d].astype(i_out.dtype)
        gate_out[rs, :] = y[:, 3 * d:4 * d].astype(gate_out.dtype)
        q_out[rs, :] = (qz * jax.nn.sigmoid(qz) * HG_HEAD ** -0.5).astype(q_out.dtype)
        la = llb_ref[...]
        lb = l1m_ref[...] - _softplus(-f)
        log_g = jnp.maximum(la, lb) + jnp.log(1.0 + jnp.exp(-jnp.abs(la - lb)))
        k = omlb_ref[...] * jax.nn.sigmoid(-f)
        if not rl.long and rl.valid < rl.t:
            live = (_iota(f.shape, 0) & (rl.t - 1)) < rl.valid
            log_g, k = jnp.where(live, log_g, 0.0), jnp.where(live, k, 0.0)
        g_out[rs, :] = log_g
        k_out[rs, :] = k.astype(k_out.dtype)


def _hgrn_pre(x, lb, p, valid):
    bsz, t, d = x.shape
    rl = _Rows(bsz, t, valid, PRE_ROWS)
    seq = rl.seq_spec(d)
    vecs = [jnp.log(lb).reshape(1, d), jnp.log1p(-lb).reshape(1, d), (1.0 - lb).reshape(1, d)]
    weights = [jnp.concatenate([p["wq"], p["wf"], p["wi"], p["wg"]], axis=1)] + vecs
    outs = pl.pallas_call(
        functools.partial(_hgrn_pre_body, rl=rl),
        grid=rl.grid,
        in_specs=[seq] + [_resident(w.shape) for w in weights],
        out_specs=[seq] * 5,
        out_shape=[jax.ShapeDtypeStruct(rl.view + (d,), F32 if i == 2 else ACT) for i in range(5)],
        compiler_params=_params(),
        name="hgrn_pre",
    )(rl.flat(x), *weights)
    return [rl.unflat(o) for o in outs]


def _gla_pre_body(x_ref, w_ref, gk2_ref, gkb_ref, q_out, k_out, g_out, v_out, gate_out, *, rl):
    d, dkk = D_MODEL, GL_HEADS * GL_DK
    subs = _sub_blocks(rl.rows)
    ys = [_dot(x_ref[rs, :].astype(BF16), w_ref[...]) for rs in subs]
    for rs, y in zip(subs, ys):
        q_out[rs, :] = (y[:, 0:dkk] * GL_DK ** -0.5).astype(q_out.dtype)
        k = y[:, dkk:2 * dkk]
        v_out[rs, :] = y[:, 2 * dkk:2 * dkk + d].astype(v_out.dtype)
        gate_out[rs, :] = y[:, 2 * dkk + d:2 * dkk + 2 * d].astype(gate_out.dtype)
        gk = _dot(y[:, 2 * dkk + 2 * d:].astype(BF16), gk2_ref[...]) + gkb_ref[...]
        log_g = -_softplus(-gk) * (1.0 / GL_GATE_NORM)
        if not rl.long and rl.valid < rl.t:
            live = (_iota(k.shape, 0) & (rl.t - 1)) < rl.valid
            log_g, k = jnp.where(live, log_g, 0.0), jnp.where(live, k, 0.0)
        g_out[rs, :] = log_g
        k_out[rs, :] = k.astype(k_out.dtype)


def _gla_pre(x, p, valid):
    bsz, t, d = x.shape
    dkk = GL_HEADS * GL_DK
    rl = _Rows(bsz, t, valid, PRE_ROWS)
    rank = p["gk1"].shape[1]
    gk1 = jnp.pad(p["gk1"], ((0, 0), (0, LANES - rank)))
    gk2 = jnp.pad(p["gk2"], ((0, LANES - rank), (0, 0)))
    weights = [jnp.concatenate([p["wq"], p["wk"], p["wv"], p["wg"], gk1], axis=1), gk2, p["gk_b"].reshape(1, dkk)]
    kseq, vseq = rl.seq_spec(dkk), rl.seq_spec(d)
    kshape, vshape = jax.ShapeDtypeStruct(rl.view + (dkk,), ACT), jax.ShapeDtypeStruct(rl.view + (d,), ACT)
    outs = pl.pallas_call(
        functools.partial(_gla_pre_body, rl=rl),
        grid=rl.grid,
        in_specs=[vseq] + [_resident(w.shape) for w in weights],
        out_specs=[kseq, kseq, kseq, vseq, vseq],
        out_shape=[kshape, kshape, jax.ShapeDtypeStruct(rl.view + (dkk,), F32), vshape, vshape],
        compiler_params=_params(),
        name="gla_pre",
    )(rl.flat(x), *weights)
    return [rl.unflat(o) for o in outs]


def _gla_body(q_ref, k_ref, g_ref, v_ref, gate_ref, s0_ref, gain_ref, o_ref, st_ref, s_scr, l_scr,
              *, chunk, tile, nb, heads, dk, dv, zero_init):
    c = chunk
    ref_row = (c - 1) // 2
    t_idx = pl.program_id(1)
    incl = _iota((c, c), 1) <= _iota((c, c), 0)
    chains = [(n, h) for n in range(nb) for h in range(heads)]

    @pl.when(t_idx == 0)
    def _init():
        for i, (n, h) in enumerate(chains):
            s_scr[i] = jnp.zeros((dv, dk), F32) if zero_init else s0_ref[n, h].T

    tri = _chunk_tri(tile, c)
    for n in range(nb):
        l_scr[n] = _dot_hilo_rhs(tri, g_ref[n])

    def chunk_step(ci, carry):
        rows = pl.ds(pl.multiple_of(ci * c, c), c)
        ld = []
        for i, (n, h) in enumerate(chains):
            kl = slice(h * dk, (h + 1) * dk)
            q, k = q_ref[n, rows, kl].astype(F32), k_ref[n, rows, kl].astype(F32)
            bc = l_scr[n, rows, kl]
            b_ref = bc[ref_row:ref_row + 1, :]
            b_last = bc[c - 1:c, :]
            ld.append(dict(qi=(q * jnp.exp(bc - b_ref)).astype(BF16), ki=(k * jnp.exp(b_ref - bc)).astype(BF16),
                           qd=(q * jnp.exp(bc)).astype(BF16), kd=(k * jnp.exp(b_last - bc)).astype(BF16),
                           v=v_ref[n, rows, h * dv:(h + 1) * dv].astype(BF16), w_last=jnp.exp(b_last),
                           st=s_scr[i]))
        for x in ld:
            x["att"] = _dot_nt(x["qi"], x["ki"])
            x["os"] = _dot_nt(x["qd"], x["st"].astype(BF16))
            x["ds"] = _dot_tn(x["v"], x["kd"])
        for i, (x, (n, h)) in enumerate(zip(ld, chains)):
            vl = slice(h * dv, (h + 1) * dv)
            s_scr[i] = x["st"] * x["w_last"] + x["ds"]
            o = x["os"] + _dot(jnp.where(incl, x["att"], 0.0).astype(BF16), x["v"])
            gate = gate_ref[n, rows, vl].astype(F32)
            on = o * lax.rsqrt(jnp.mean(o * o, -1, keepdims=True) + RMS_EPS) * gain_ref[...]
            o_ref[n, rows, vl] = (on * (gate * jax.nn.sigmoid(gate))).astype(o_ref.dtype)
        return carry

    lax.fori_loop(0, tile // c, chunk_step, 0)

    @pl.when(t_idx == pl.num_programs(1) - 1)
    def _final():
        for i, (n, h) in enumerate(chains):
            st_ref[n, h] = s_scr[i].T


def _gla_scan(q, k, g, v, gate, s0, gain, heads, dk, dv, zero_init):
    bsz, t, _ = q.shape
    c = min(CHUNK, t)
    tile = min(t, SCAN_ROWS)
    nb = GLA_BATCH_PER_STEP
    assert t % tile == 0 and tile % c == 0 and c % SUBLANES == 0 and bsz % nb == 0
    kseq = pl.BlockSpec((nb, tile, heads * dk), lambda i, j: (i, j, 0))
    vseq = pl.BlockSpec((nb, tile, heads * dv), lambda i, j: (i, j, 0))
    st = pl.BlockSpec((nb, heads, dk, dv), lambda i, j: (i, 0, 0, 0))
    return pl.pallas_call(
        functools.partial(_gla_body, chunk=c, tile=tile, nb=nb, heads=heads, dk=dk, dv=dv, zero_init=zero_init),
        grid=(bsz // nb, t // tile),
        in_specs=[kseq, kseq, kseq, vseq, vseq, st, _resident((1, dv))],
        out_specs=[vseq, st],
        out_shape=[jax.ShapeDtypeStruct((bsz, t, heads * dv), ACT),
                   jax.ShapeDtypeStruct((bsz, heads, dk, dv), F32)],
        scratch_shapes=[pltpu.VMEM((nb * heads, dv, dk), F32), pltpu.VMEM((nb, tile, heads * dk), F32)],
        compiler_params=_params(),
        name="gla_scan",
    )(q, k, g, v, gate, s0, gain.reshape(1, dv))


def _cf_body(x_ref, o_ref, fill_ref, wo_ref, l1w_ref, l1b_ref, wu_ref, wg_ref, wd_ref, cw_ref, cb_ref,
             l2w_ref, l2b_ref, out_ref, z_ref, h_scr, carry_scr, *, rl):
    n_state = CONV_W - 1
    t_idx = pl.program_id(1)
    subs = _sub_blocks(rl.rows)
    sub = subs[0].stop
    if rl.long:
        @pl.when(t_idx == 0)
        def _load_state():
            carry_scr[...] = fill_ref[...]

    x1s = [_ln_rows(DN_ALPHA * x_ref[rs, :] + _dot(o_ref[rs, :], wo_ref[...]), l1w_ref[...], l1b_ref[...])
           for rs in subs]
    for rs, x1 in zip(subs, x1s):
        x1b = x1.astype(BF16)
        for j in range(wu_ref.shape[1] // FFN_COLS):
            cs = slice(j * FFN_COLS, (j + 1) * FFN_COLS)
            zc = _dot(x1b, wg_ref[:, cs])
            uc = _dot(x1b, wu_ref[:, cs])
            carry_c = carry_scr.at[:, cs] if rl.long else None
            fill_c = None if rl.long else fill_ref.at[rs, cs]
            zp1 = _prev_rows(zc, 1, rl, t_idx, carry_c, fill_c, n_state)
            zp2 = _prev_rows(zc, 2, rl, t_idx, carry_c, fill_c, n_state)
            if rl.long:
                carry_scr[:, cs] = zc[sub - n_state:sub, :]
            else:
                z_ref[rs, cs] = zc
            cw = cw_ref[:, cs]
            pre = cb_ref[:, cs] + cw[0:1] * zp2 + cw[1:2] * zp1 + cw[2:3] * zc
            h_scr[rs, cs] = (pre * jax.nn.sigmoid(pre) * uc).astype(BF16)
        out_ref[rs, :] = _ln_rows(DN_ALPHA * x1 + _dot(h_scr[rs, :], wd_ref[...]), l2w_ref[...], l2b_ref[...])
    if rl.long:
        @pl.when(t_idx == pl.num_programs(1) - 1)
        def _store_state():
            z_ref[...] = carry_scr[...]


def _proj_ffn(x, o, buf, wo, ln1, p, ln2, valid):
    bsz, t, d = x.shape
    f = p["wu"].shape[1]
    n_state = CONV_W - 1
    rl = _Rows(bsz, t, valid, FFN_ROWS if t >= FFN_ROWS else FFN_ROWS // 2)
    assert f % FFN_COLS == 0 and valid >= n_state
    seq = rl.seq_spec(d)
    vec = lambda n: _resident((1, n))
    if rl.long:
        z_spec, z_shape = pl.BlockSpec((None, n_state, f), lambda i, j: (i, 0, 0)), (bsz, n_state, f)
    else:
        z_spec, z_shape = rl.seq_spec(f), rl.view + (f,)
    out, z = pl.pallas_call(
        functools.partial(_cf_body, rl=rl),
        grid=rl.grid,
        in_specs=[seq, seq, rl.state_spec(n_state, f),
                  _resident((d, d)), vec(d), vec(d), _resident((d, f)), _resident((d, f)), _resident((f, d)),
                  _resident((CONV_W, f)), vec(f), vec(d), vec(d)],
        out_specs=[seq, z_spec],
        out_shape=[jax.ShapeDtypeStruct(rl.view + (d,), F32), jax.ShapeDtypeStruct(z_shape, F32)],
        scratch_shapes=[pltpu.VMEM((rl.rows, f), BF16), pltpu.VMEM((n_state, f), F32)],
        compiler_params=_params(),
        name="proj_ffn",
    )(rl.flat(x), rl.flat(o), rl.state_rows(buf), wo, ln1[0].reshape(1, d), ln1[1].reshape(1, d),
      p["wu"], p["wg"], p["wd"], p["conv_w"], p["conv_b"].reshape(1, f), ln2[0].reshape(1, d), ln2[1].reshape(1, d))
    if rl.long:
        return out, z
    return rl.unflat(out), rl.unflat(z)[:, valid - n_state:valid]


def _run_trunk(x, valid, st_rw, st_shift, st_hg, st_gl, st_conv, rw, rw_vmix, hg, lower_bounds, gl, ffn, ln, zero_init):
    new_rw, new_shift, new_hg, new_gl, new_conv = [], [], [], [], []
    v_first = None
    for i in range(DEPTH):
        j = i // N_MIXERS
        kind = i % N_MIXERS
        if kind == 0:
            p = {n: q[j] for n, q in rw.items()}
            vm = None if j == 0 else tuple(q[j - 1] for q in rw_vmix)
            r, wl, k, v, a, b, g = _rwkv_pre(x, st_shift[j], v_first, vm, p, valid)
            if vm is None:
                v_first = v
            o, s_t = _rwkv_scan(r, wl, k, v, a, b, g, st_rw[j], p, zero_init)
            new_rw.append(s_t)
            new_shift.append(x[:, valid - 1])
        elif kind == 1:
            p = {n: q[j] for n, q in hg.items()}
            q_, k, log_g, val, gate = _hgrn_pre(x, lower_bounds[i], p, valid)
            o, s_t = _gla_scan(q_, k, log_g, val, gate, st_hg[j], p["norm_w"], HG_HEADS, HG_HEAD, HG_HEAD, zero_init)
            new_hg.append(s_t)
        else:
            p = {n: q[j] for n, q in gl.items()}
            q_, k, log_g, val, gate = _gla_pre(x, p, valid)
            o, s_t = _gla_scan(q_, k, log_g, val, gate, st_gl[j], p["norm_w"], GL_HEADS, GL_DK, GL_DV, zero_init)
            new_gl.append(s_t)
        x, buf = _proj_ffn(x, o, st_conv[i], p["wo"], (ln["ln1_w"][i], ln["ln1_b"][i]),
                           {n: q[i] for n, q in ffn.items()}, (ln["ln2_w"][i], ln["ln2_b"][i]), valid)
        new_conv.append(buf)
    return (x[:, :valid], jnp.stack(new_rw), jnp.stack(new_shift), jnp.stack(new_hg), jnp.stack(new_gl),
            jnp.stack(new_conv))


def kernel(x_prompt, x_sample, state_rwkv, state_rwkv_shift, state_hgrn, state_gla, state_ffn_conv, rw_mix, rw_wr, rw_wk, rw_wv, rw_wo, rw_w0, rw_w1, rw_w2, rw_a0, rw_a1, rw_a2, rw_g1, rw_g2, rw_k_k, rw_k_a, rw_r_k, rw_lnx_w, rw_lnx_b, rw_v0, rw_v1, rw_v2, hg_wq, hg_wf, hg_wi, hg_wg, hg_wo, hg_norm_w, hg_lb_param, gl_wq, gl_wk, gl_wv, gl_wg, gl_gk1, gl_gk2, gl_gk_b, gl_wo, gl_norm_w, ffn_wu, ffn_wg, ffn_conv_w, ffn_conv_b, ffn_wd, ln1_w, ln1_b, ln2_w, ln2_b):
    bf = lambda w: w.astype(BF16)
    rw = dict(mix=rw_mix, wr=bf(rw_wr), wk=bf(rw_wk), wv=bf(rw_wv), wo=bf(rw_wo), w0=rw_w0, w1=bf(rw_w1),
              w2=bf(rw_w2), a0=rw_a0, a1=bf(rw_a1), a2=bf(rw_a2), g1=bf(rw_g1), g2=bf(rw_g2), k_k=rw_k_k,
              k_a=rw_k_a, r_k=rw_r_k, lnx_w=rw_lnx_w, lnx_b=rw_lnx_b)
    rw_vmix = (rw_v0, bf(rw_v1), bf(rw_v2))
    hg = dict(wq=bf(hg_wq), wf=bf(hg_wf), wi=bf(hg_wi), wg=bf(hg_wg), wo=bf(hg_wo), norm_w=hg_norm_w)
    gl = dict(wq=bf(gl_wq), wk=bf(gl_wk), wv=bf(gl_wv), wg=bf(gl_wg), gk1=bf(gl_gk1), gk2=bf(gl_gk2),
              gk_b=gl_gk_b, wo=bf(gl_wo), norm_w=gl_norm_w)
    ffn = dict(wu=bf(ffn_wu), wg=bf(ffn_wg), conv_w=ffn_conv_w, conv_b=ffn_conv_b, wd=bf(ffn_wd))
    ln = dict(ln1_w=ln1_w, ln1_b=ln1_b, ln2_w=ln2_w, ln2_b=ln2_b)
    lb_soft = jax.nn.softmax(hg_lb_param, axis=0)
    lower_bounds = jnp.cumsum(lb_soft, axis=0) - lb_soft[0]

    nb = x_prompt.shape[0]
    zeros = lambda s: jnp.zeros((s.shape[0], nb) + s.shape[2:], s.dtype)
    p_out = _run_trunk(x_prompt, x_prompt.shape[1], zeros(state_rwkv), zeros(state_rwkv_shift), zeros(state_hgrn),
                       zeros(state_gla), zeros(state_ffn_conv), rw, rw_vmix, hg, lower_bounds, gl, ffn, ln, True)
    t_s = x_sample.shape[1]
    t_pad = max(SUBLANES, 1 << (t_s - 1).bit_length())
    xs = jnp.pad(x_sample, ((0, 0), (0, t_pad - t_s), (0, 0)))
    s_out = _run_trunk(xs, t_s, state_rwkv, state_rwkv_shift, state_hgrn, state_gla, state_ffn_conv,
                       rw, rw_vmix, hg, lower_bounds, gl, ffn, ln, False)
    return (p_out[0], s_out[0]) + tuple(p_out[1:]) + tuple(s_out[1:])
```

```python
import functools
import math

import jax
import jax.numpy as jnp
from jax import lax
from jax.experimental import pallas as pl
from jax.experimental.pallas import tpu as pltpu

D_MODEL = 1024
DEPTH = 4
N_MIXERS = 3
RW_HEAD = 64
RW_HEADS = D_MODEL // RW_HEAD
RW_HEAD_SHIFT = RW_HEAD.bit_length() - 1
RW_LNX_EPS = 64e-5
HG_HEAD = 128
HG_HEADS = D_MODEL // HG_HEAD
GL_HEADS = 4
GL_DK = (D_MODEL // 2) // GL_HEADS
GL_DV = D_MODEL // GL_HEADS
GL_GATE_NORM = 16.0
CONV_W = 3
LN_EPS = 1e-5
RMS_EPS = 1e-5
DN_ALPHA = (2 * DEPTH) ** 0.25

LANES = 128
SUBLANES = 8
RW_GROUP = 256
RW_GROUP_HEADS = RW_GROUP // RW_HEAD
CHUNK = 32
RW_CHUNK = 64
RW_BATCH_PER_STEP = 2
RW_BATCH_PER_STEP_SHORT = 4
GLA_BATCH_PER_STEP = 4
SCAN_ROWS = 256
PRE_ROWS = 512
FFN_ROWS = 512
SUB_ROWS = 256
FFN_COLS = 256
VMEM_LIMIT = 56 * 1024 * 1024

BF16 = jnp.bfloat16
F32 = jnp.float32
ACT = BF16


def _dot(a, b):
    return jnp.dot(a, b, preferred_element_type=F32)


def _dot_nt(a, b):
    return lax.dot_general(a, b, (((1,), (1,)), ((), ())), preferred_element_type=F32)


def _dot_tn(a, b):
    return lax.dot_general(a, b, (((0,), (0,)), ((), ())), preferred_element_type=F32)


def _dot_hilo_rhs(sel, x):
    hi = x.astype(BF16)
    lo = (x - hi.astype(F32)).astype(BF16)
    return _dot(sel, hi) + _dot(sel, lo)


def _head_sums(x, ones_bd):
    return _dot(x.astype(BF16), ones_bd)


def _iota(shape, dim):
    return lax.broadcasted_iota(jnp.int32, shape, dim)


def _chunk_tri(n, c):
    row, col = _iota((n, n), 0), _iota((n, n), 1)
    sh = int(math.log2(c))
    keep = (col <= row) & ((row >> sh) == (col >> sh))
    return jnp.where(keep, 1.0, 0.0).astype(BF16)


def _softplus(t):
    return jnp.maximum(t, 0.0) + jnp.log(1.0 + jnp.exp(-jnp.abs(t)))


def _ln_rows(x, w, b):
    mu = jnp.mean(x, -1, keepdims=True)
    xc = x - mu
    var = jnp.mean(xc * xc, -1, keepdims=True)
    return xc * lax.rsqrt(var + LN_EPS) * w + b


def _resident(shape):
    return pl.BlockSpec(shape, lambda i, j: (0,) * len(shape), pipeline_mode=pl.Buffered(1))


def _sub_blocks(rows):
    sub = min(rows, SUB_ROWS)
    assert rows % sub == 0
    return [slice(s * sub, (s + 1) * sub) for s in range(rows // sub)]


def _params():
    return pltpu.CompilerParams(dimension_semantics=("arbitrary", "arbitrary"), vmem_limit_bytes=VMEM_LIMIT)


class _Rows:
    def __init__(self, bsz, t, valid, max_rows):
        self.bsz, self.t, self.valid = bsz, t, valid
        self.long = t >= max_rows
        if self.long:
            assert t % max_rows == 0 and valid == t
            self.rows, self.grid, self.view = max_rows, (bsz, t // max_rows), (bsz, t)
        else:
            total = bsz * t
            self.rows = min(total, max_rows)
            assert t & (t - 1) == 0 and self.rows % t == 0 and total % self.rows == 0
            self.grid, self.view = (1, total // self.rows), (1, total)

    def flat(self, x):
        return x.reshape(self.view + x.shape[2:])

    def unflat(self, x):
        return x.reshape((self.bsz, self.t) + x.shape[2:])

    def seq_spec(self, width):
        return pl.BlockSpec((None, self.rows, width), lambda i, j: (i, j, 0))

    def state_rows(self, state):
        if self.long:
            return state
        return self.flat(jnp.pad(state, ((0, 0), (0, self.t - state.shape[1]), (0, 0))))

    def state_spec(self, n, width):
        if self.long:
            return pl.BlockSpec((None, n, width), lambda i, j: (i, 0, 0))
        return self.seq_spec(width)


def _prev_rows(x, shift, rl, t_idx, carry_scr, fill_ref, n_prev):
    rows, width = x.shape
    row = _iota((rows, width), 0)
    rolled = pltpu.roll(x, shift, 0)
    if rl.long:
        out = rolled
        for i in range(shift):
            src = n_prev - shift + i
            out = jnp.where(row == i, carry_scr[src:src + 1, :], out)
        return out
    fill = fill_ref[...]
    if n_prev != shift:
        fill = pltpu.roll(fill, rows - (n_prev - shift), 0)
    return jnp.where((row & (rl.t - 1)) >= shift, rolled, fill)


def _rwkv_pre_body(*refs, rl, has_vmix):
    it = iter(refs)
    x_ref, fill_ref = next(it), next(it)
    vf_ref = next(it) if has_vmix else None
    mix_ref, wr_ref, wk_ref, wv_ref, w0_ref, w1_ref, w2_ref = (next(it) for _ in range(7))
    a0_ref, a1_ref, a2_ref, g1_ref, g2_ref, kk_ref, ka_ref = (next(it) for _ in range(7))
    v0_ref, v1_ref, v2_ref = (next(it) for _ in range(3)) if has_vmix else (None, None, None)
    r_out, wl_out, k_out, v_out, a_out, b_out, g_out, carry_scr = (next(it) for _ in range(8))

    t_idx = pl.program_id(1)
    if rl.long:
        @pl.when(t_idx == 0)
        def _load_state():
            carry_scr[0:1, :] = fill_ref[...]
    bd = (_iota((RW_GROUP, RW_GROUP), 0) >> RW_HEAD_SHIFT) == (_iota((RW_GROUP, RW_GROUP), 1) >> RW_HEAD_SHIFT)
    ones_bd = jnp.where(bd, 1.0, 0.0).astype(BF16)

    subs = _sub_blocks(rl.rows)
    stage1 = []
    for rs in subs:
        rows = rs.stop - rs.start
        x = x_ref[rs, :]
        x_prev = _prev_rows(x, 1, rl, t_idx, carry_scr, None if rl.long else fill_ref.at[rs, :], 1)
        if rl.long:
            carry_scr[0:1, :] = x[rows - 1:rows, :]
        xx = x_prev - x
        xr, xw, xk, xv, xa, xg = [(x + xx * mix_ref[j:j + 1, :]).astype(BF16) for j in range(6)]
        stage1.append(dict(r=_dot(xr, wr_ref[...]), k=_dot(xk, wk_ref[...]), v=_dot(xv, wv_ref[...]),
                           lw=_dot(xw, w1_ref[...]), la=_dot(xa, a1_ref[...]), lg=_dot(xg, g1_ref[...]),
                           lv=_dot(xv, v1_ref[...]) if has_vmix else None))
    for rs, s1 in zip(subs, stage1):
        rows = rs.stop - rs.start
        r, k, v = s1["r"], s1["k"], s1["v"]
        lora_w = _dot(jnp.tanh(s1["lw"]).astype(BF16), w2_ref[...])
        wl = -jnp.exp(-_softplus(-(w0_ref[...] + lora_w)) - 0.5)
        a = jax.nn.sigmoid(a0_ref[...] + _dot(s1["la"].astype(BF16), a2_ref[...]))
        if has_vmix:
            gate = jax.nn.sigmoid(v0_ref[...] + _dot(s1["lv"].astype(BF16), v2_ref[...]))
            v = v + (vf_ref[rs, :].astype(F32) - v) * gate
        g_out[rs, :] = _dot(jax.nn.sigmoid(s1["lg"]).astype(BF16), g2_ref[...]).astype(g_out.dtype)

        kk = k * kk_ref[...]
        k = k * (1.0 + (a - 1.0) * ka_ref[...])
        if not rl.long and rl.valid < rl.t:
            live = (_iota((rows, D_MODEL), 0) & (rl.t - 1)) < rl.valid
            wl, k, v, kk = (jnp.where(live, z, 0.0) for z in (wl, k, v, kk))
        r_out[rs, :] = r.astype(r_out.dtype)
        wl_out[rs, :] = wl
        k_out[rs, :] = k.astype(k_out.dtype)
        v_out[rs, :] = v.astype(v_out.dtype)
        for g in range(D_MODEL // RW_GROUP):
            lanes = slice(g * RW_GROUP, (g + 1) * RW_GROUP)
            kkg = kk[:, lanes]
            kkn = kkg / jnp.maximum(jnp.sqrt(_head_sums(kkg * kkg, ones_bd)), 1e-12)
            a_out[rs, lanes] = (-kkn).astype(a_out.dtype)
            b_out[rs, lanes] = (kkn * a[:, lanes]).astype(b_out.dtype)


def _rwkv_pre(x, shift, v_first, vmix, p, valid):
    bsz, t, d = x.shape
    rl = _Rows(bsz, t, valid, PRE_ROWS)
    has_vmix = vmix is not None
    seq = rl.seq_spec(d)
    vec = lambda z: z.reshape(1, d)
    args = [rl.flat(x), rl.state_rows(shift[:, None, :])]
    specs = [seq, rl.state_spec(1, d)]
    if has_vmix:
        args.append(rl.flat(v_first))
        specs.append(seq)
    weights = [p["mix"], p["wr"], p["wk"], p["wv"], vec(p["w0"]), p["w1"], p["w2"], vec(p["a0"]), p["a1"], p["a2"],
               p["g1"], p["g2"], vec(p["k_k"]), vec(p["k_a"])]
    if has_vmix:
        weights += [vec(vmix[0]), vmix[1], vmix[2]]
    args += weights
    specs += [_resident(w.shape) for w in weights]
    outs = pl.pallas_call(
        functools.partial(_rwkv_pre_body, rl=rl, has_vmix=has_vmix),
        grid=rl.grid,
        in_specs=specs,
        out_specs=[seq] * 7,
        out_shape=[jax.ShapeDtypeStruct(rl.view + (d,), F32 if i == 1 else ACT) for i in range(7)],
        scratch_shapes=[pltpu.VMEM((SUBLANES, d), F32)],
        compiler_params=_params(),
        name="rwkv_pre",
    )(*args)
    return [rl.unflat(o) for o in outs]


def _rwkv_body(r_ref, wl_ref, k_ref, v_ref, a_ref, b_ref, g_ref, s0_ref, rk_ref, lw_ref, lb_ref,
               y_ref, st_ref, s_scr, l_scr, *, chunk, tile, nb, zero_init):
    c = chunk
    gh = RW_GROUP_HEADS
    n_groups = D_MODEL // RW_GROUP
    t_idx = pl.program_id(1)
    log2c = int(math.log2(c))

    bd_r, bd_c = _iota((RW_GROUP, RW_GROUP), 0), _iota((RW_GROUP, RW_GROUP), 1)
    bd256 = (bd_r >> RW_HEAD_SHIFT) == (bd_c >> RW_HEAD_SHIFT)
    ones_bd = jnp.where(bd256, 1.0, 0.0).astype(BF16)
    hm = (_iota((gh * c, RW_GROUP), 0) >> log2c) == (_iota((gh * c, RW_GROUP), 1) >> RW_HEAD_SHIFT)
    hm_b = jnp.where(hm, 1.0, 0.0).astype(BF16)
    bdm = (_iota((gh * c, gh * c), 0) >> log2c) == (_iota((gh * c, gh * c), 1) >> log2c)
    bdm_b = jnp.where(bdm, 1.0, 0.0).astype(BF16)
    trow = _iota((c, gh * c), 0)
    tcol = _iota((c, gh * c), 1) & (c - 1)
    strict = tcol < trow
    incl = tcol <= trow
    chains = [(n, g) for n in range(nb) for g in range(n_groups)]

    def head_block(i, n, g, h):
        blk = slice(h * RW_HEAD, (h + 1) * RW_HEAD)
        return (i, blk, blk), (n, slice(g * RW_GROUP + h * RW_HEAD, g * RW_GROUP + (h + 1) * RW_HEAD), slice(None))

    @pl.when(t_idx == 0)
    def _init():
        for i, (n, g) in enumerate(chains):
            s_scr[i] = jnp.zeros((RW_GROUP, RW_GROUP), F32)
            if not zero_init:
                for h in range(gh):
                    dst, src = head_block(i, n, g, h)
                    s_scr[dst] = s0_ref[src]

    tri = _chunk_tri(tile, c)
    for n in range(nb):
        l_scr[n] = _dot_hilo_rhs(tri, wl_ref[n])

    def stack(x_b):
        return jnp.concatenate([x_b] * gh, axis=0) * hm_b

    def bd_stack(m_b):
        return jnp.concatenate([m_b] * gh, axis=0) * bdm_b

    def chunk_step(ci, carry):
        rows = pl.ds(pl.multiple_of(ci * c, c), c)
        ld = []
        for n, g in chains:
            lanes = slice(g * RW_GROUP, (g + 1) * RW_GROUP)
            f32 = lambda ref: ref[n, rows, lanes].astype(F32)
            ld.append(dict(r=f32(r_ref), wl=wl_ref[n, rows, lanes], k=f32(k_ref), v=v_ref[n, rows, lanes].astype(BF16),
                           a=f32(a_ref), b=f32(b_ref), lc=l_scr[n, rows, lanes]))
        for i, x in enumerate(ld):
            lc = x["lc"]
            x["lend"] = lc[c - 1:c, :]
            w_inv = jnp.exp(-lc)
            at = x["a"] * jnp.exp(lc - x["wl"])
            rt = x["r"] * jnp.exp(lc)
            x["ar"] = jnp.concatenate([at, rt], axis=0).astype(BF16)
            x["bts"] = stack((x["b"] * w_inv).astype(BF16))
            x["kts"] = stack((x["k"] * w_inv).astype(BF16))
            x["vs"] = stack(x["v"])
            x["sg"] = s_scr[i]
        for x in ld:
            x["sb"] = _dot_nt(x["ar"], x["bts"])
            x["sk"] = _dot_nt(x["ar"], x["kts"])
            x["x2"] = _dot_nt(x["ar"], x["sg"].astype(BF16))
        for x, (n, g) in zip(ld, chains):
            lanes = slice(g * RW_GROUP, (g + 1) * RW_GROUP)
            x["bonus"] = _head_sums(x["r"] * x["k"] * rk_ref[:, lanes], ones_bd) * x["v"].astype(F32)
        for x in ld:
            x["p"] = jnp.where(strict, x["sb"][:c], 0.0).astype(BF16)
            x["arb"] = jnp.where(incl, x["sb"][c:], 0.0).astype(BF16)
            kv = _dot(jnp.where(jnp.concatenate([strict, incl], axis=0), x["sk"], 0.0).astype(BF16), x["vs"])
            x["u"] = x["x2"][:c] + kv[:c]
            x["ykv"] = x["x2"][c:] + kv[c:]
        for it in range(log2c):
            for x in ld:
                x["u"] = x["u"] + _dot(x["p"], stack(x["u"].astype(BF16)))
            if it < log2c - 1:
                for x in ld:
                    x["p"] = _dot(x["p"], bd_stack(x["p"])).astype(BF16)
        for x in ld:
            x["y"] = x["ykv"] + _dot(x["arb"], stack(x["u"].astype(BF16)))
        for i, x in enumerate(ld):
            w_end = jnp.exp(x["lend"] - x["lc"])
            uv = jnp.concatenate([x["u"].astype(BF16), x["v"]], axis=0)
            bk = jnp.concatenate([x["b"] * w_end, x["k"] * w_end], axis=0).astype(BF16)
            s_scr[i] = x["sg"] * jnp.exp(x["lend"]) + jnp.where(bd256, _dot_tn(uv, bk), 0.0)
        inv_n = 1.0 / RW_HEAD
        for x in ld:
            x["yc"] = x["y"] - _head_sums(x["y"], ones_bd) * inv_n
        for x, (n, g) in zip(ld, chains):
            lanes = slice(g * RW_GROUP, (g + 1) * RW_GROUP)
            var = _head_sums(x["yc"] * x["yc"], ones_bd) * inv_n
            yn = x["yc"] * lax.rsqrt(var + RW_LNX_EPS) * lw_ref[:, lanes] + lb_ref[:, lanes]
            y_ref[n, rows, lanes] = ((yn + x["bonus"]) * g_ref[n, rows, lanes].astype(F32)).astype(y_ref.dtype)
        return carry

    lax.fori_loop(0, tile // c, chunk_step, 0)

    @pl.when(t_idx == pl.num_programs(1) - 1)
    def _final():
        for i, (n, g) in enumerate(chains):
            for h in range(gh):
                src, dst = head_block(i, n, g, h)
                st_ref[dst] = s_scr[src]


def _rwkv_scan(r, wl, k, v, a, b, g, s0, p, zero_init):
    bsz, t, d = r.shape
    c = min(RW_CHUNK, t)
    tile = min(t, SCAN_ROWS)
    nb = RW_BATCH_PER_STEP if t > c else RW_BATCH_PER_STEP_SHORT
    assert t % tile == 0 and tile % c == 0 and c % SUBLANES == 0 and bsz % nb == 0
    s0r = s0.reshape(bsz, d, RW_HEAD)
    seq = pl.BlockSpec((nb, tile, d), lambda i, j: (i, j, 0))
    st = pl.BlockSpec((nb, d, RW_HEAD), lambda i, j: (i, 0, 0))
    vec = _resident((1, d))
    y, s_t = pl.pallas_call(
        functools.partial(_rwkv_body, chunk=c, tile=tile, nb=nb, zero_init=zero_init),
        grid=(bsz // nb, t // tile),
        in_specs=[seq] * 7 + [st, vec, vec, vec],
        out_specs=[seq, st],
        out_shape=[jax.ShapeDtypeStruct((bsz, t, d), ACT), jax.ShapeDtypeStruct((bsz, d, RW_HEAD), F32)],
        scratch_shapes=[pltpu.VMEM((nb * (d // RW_GROUP), RW_GROUP, RW_GROUP), F32),
                        pltpu.VMEM((nb, tile, d), F32)],
        compiler_params=_params(),
        name="rwkv_scan",
    )(r, wl, k, v, a, b, g, s0r, p["r_k"].reshape(1, d), p["lnx_w"].reshape(1, d), p["lnx_b"].reshape(1, d))
    return y, s_t.reshape(bsz, RW_HEADS, RW_HEAD, RW_HEAD)


def _hgrn_pre_body(x_ref, wq_ref, wf_ref, wi_ref, wg_ref, llb_ref, l1m_ref, omlb_ref,
                   q_out, k_out, g_out, i_out, gate_out, *, rl):
    subs = _sub_blocks(rl.rows)
    ys = []
    for rs in subs:
        xb = x_ref[rs, :].astype(BF16)
        ys.append([_dot(xb, w[...]) for w in (wq_ref, wf_ref, wi_ref, wg_ref)])
    for rs, (qz, f, val, gate) in zip(subs, ys):
        i_out[rs, :] = val.astype(i_out.dtype)
        gate_out[rs, :] = gate.astype(gate_out.dtype)
        q_out[rs, :] = (qz * jax.nn.sigmoid(qz) * HG_HEAD ** -0.5).astype(q_out.dtype)
        la = llb_ref[...]
        lb = l1m_ref[...] - _softplus(-f)
        log_g = jnp.maximum(la, lb) + jnp.log(1.0 + jnp.exp(-jnp.abs(la - lb)))
        k = omlb_ref[...] * jax.nn.sigmoid(-f)
        if not rl.long and rl.valid < rl.t:
            live = (_iota(f.shape, 0) & (rl.t - 1)) < rl.valid
            log_g, k = jnp.where(live, log_g, 0.0), jnp.where(live, k, 0.0)
        g_out[rs, :] = log_g
        k_out[rs, :] = k.astype(k_out.dtype)


def _hgrn_pre(x, lb, p, valid):
    bsz, t, d = x.shape
    rl = _Rows(bsz, t, valid, PRE_ROWS)
    seq = rl.seq_spec(d)
    vecs = [jnp.log(lb).reshape(1, d), jnp.log1p(-lb).reshape(1, d), (1.0 - lb).reshape(1, d)]
    weights = [p["wq"], p["wf"], p["wi"], p["wg"]] + vecs
    outs = pl.pallas_call(
        functools.partial(_hgrn_pre_body, rl=rl),
        grid=rl.grid,
        in_specs=[seq] + [_resident(w.shape) for w in weights],
        out_specs=[seq] * 5,
        out_shape=[jax.ShapeDtypeStruct(rl.view + (d,), F32 if i == 2 else ACT) for i in range(5)],
        compiler_params=_params(),
        name="hgrn_pre",
    )(rl.flat(x), *weights)
    return [rl.unflat(o) for o in outs]


def _gla_pre_body(x_ref, wq_ref, wk_ref, wv_ref, wg_ref, gk1_ref, gk2_ref, gkb_ref,
                  q_out, k_out, g_out, v_out, gate_out, *, rl):
    subs = _sub_blocks(rl.rows)
    ys = []
    for rs in subs:
        xb = x_ref[rs, :].astype(BF16)
        ys.append([_dot(xb, w[...]) for w in (wq_ref, wk_ref, wv_ref, wg_ref, gk1_ref)])
    for rs, (q, k, v, gate, lora) in zip(subs, ys):
        q_out[rs, :] = (q * GL_DK ** -0.5).astype(q_out.dtype)
        v_out[rs, :] = v.astype(v_out.dtype)
        gate_out[rs, :] = gate.astype(gate_out.dtype)
        gk = _dot(lora.astype(BF16), gk2_ref[...]) + gkb_ref[...]
        log_g = -_softplus(-gk) * (1.0 / GL_GATE_NORM)
        if not rl.long and rl.valid < rl.t:
            live = (_iota(k.shape, 0) & (rl.t - 1)) < rl.valid
            log_g, k = jnp.where(live, log_g, 0.0), jnp.where(live, k, 0.0)
        g_out[rs, :] = log_g
        k_out[rs, :] = k.astype(k_out.dtype)


def _gla_pre(x, p, valid):
    bsz, t, d = x.shape
    dkk = GL_HEADS * GL_DK
    rl = _Rows(bsz, t, valid, PRE_ROWS)
    rank = p["gk1"].shape[1]
    gk1 = jnp.pad(p["gk1"], ((0, 0), (0, LANES - rank)))
    gk2 = jnp.pad(p["gk2"], ((0, LANES - rank), (0, 0)))
    weights = [p["wq"], p["wk"], p["wv"], p["wg"], gk1, gk2, p["gk_b"].reshape(1, dkk)]
    kseq, vseq = rl.seq_spec(dkk), rl.seq_spec(d)
    kshape, vshape = jax.ShapeDtypeStruct(rl.view + (dkk,), ACT), jax.ShapeDtypeStruct(rl.view + (d,), ACT)
    outs = pl.pallas_call(
        functools.partial(_gla_pre_body, rl=rl),
        grid=rl.grid,
        in_specs=[vseq] + [_resident(w.shape) for w in weights],
        out_specs=[kseq, kseq, kseq, vseq, vseq],
        out_shape=[kshape, kshape, jax.ShapeDtypeStruct(rl.view + (dkk,), F32), vshape, vshape],
        compiler_params=_params(),
        name="gla_pre",
    )(rl.flat(x), *weights)
    return [rl.unflat(o) for o in outs]


def _gla_body(q_ref, k_ref, g_ref, v_ref, gate_ref, s0_ref, gain_ref, o_ref, st_ref, s_scr, l_scr,
              *, chunk, tile, nb, heads, dk, dv, zero_init):
    c = chunk
    ref_row = (c - 1) // 2
    t_idx = pl.program_id(1)
    incl = _iota((c, c), 1) <= _iota((c, c), 0)
    chains = [(n, h) for n in range(nb) for h in range(heads)]

    @pl.when(t_idx == 0)
    def _init():
        for i, (n, h) in enumerate(chains):
            s_scr[i] = jnp.zeros((dv, dk), F32) if zero_init else s0_ref[n, h].T

    tri = _chunk_tri(tile, c)
    for n in range(nb):
        l_scr[n] = _dot_hilo_rhs(tri, g_ref[n])

    def chunk_step(ci, carry):
        rows = pl.ds(pl.multiple_of(ci * c, c), c)
        ld = []
        for i, (n, h) in enumerate(chains):
            kl = slice(h * dk, (h + 1) * dk)
            q, k = q_ref[n, rows, kl].astype(F32), k_ref[n, rows, kl].astype(F32)
            bc = l_scr[n, rows, kl]
            b_ref = bc[ref_row:ref_row + 1, :]
            b_last = bc[c - 1:c, :]
            ld.append(dict(qi=(q * jnp.exp(bc - b_ref)).astype(BF16), ki=(k * jnp.exp(b_ref - bc)).astype(BF16),
                           qd=(q * jnp.exp(bc)).astype(BF16), kd=(k * jnp.exp(b_last - bc)).astype(BF16),
                           v=v_ref[n, rows, h * dv:(h + 1) * dv].astype(BF16), w_last=jnp.exp(b_last),
                           st=s_scr[i]))
        for x in ld:
            x["att"] = _dot_nt(x["qi"], x["ki"])
            x["os"] = _dot_nt(x["qd"], x["st"].astype(BF16))
            x["ds"] = _dot_tn(x["v"], x["kd"])
        for i, (x, (n, h)) in enumerate(zip(ld, chains)):
            vl = slice(h * dv, (h + 1) * dv)
            s_scr[i] = x["st"] * x["w_last"] + x["ds"]
            o = x["os"] + _dot(jnp.where(incl, x["att"], 0.0).astype(BF16), x["v"])
            gate = gate_ref[n, rows, vl].astype(F32)
            on = o * lax.rsqrt(jnp.mean(o * o, -1, keepdims=True) + RMS_EPS) * gain_ref[...]
            o_ref[n, rows, vl] = (on * (gate * jax.nn.sigmoid(gate))).astype(o_ref.dtype)
        return carry

    lax.fori_loop(0, tile // c, chunk_step, 0)

    @pl.when(t_idx == pl.num_programs(1) - 1)
    def _final():
        for i, (n, h) in enumerate(chains):
            st_ref[n, h] = s_scr[i].T


def _gla_scan(q, k, g, v, gate, s0, gain, heads, dk, dv, zero_init):
    bsz, t, _ = q.shape
    c = min(CHUNK, t)
    tile = min(t, SCAN_ROWS)
    nb = GLA_BATCH_PER_STEP
    assert t % tile == 0 and tile % c == 0 and c % SUBLANES == 0 and bsz % nb == 0
    kseq = pl.BlockSpec((nb, tile, heads * dk), lambda i, j: (i, j, 0))
    vseq = pl.BlockSpec((nb, tile, heads * dv), lambda i, j: (i, j, 0))
    st = pl.BlockSpec((nb, heads, dk, dv), lambda i, j: (i, 0, 0, 0))
    return pl.pallas_call(
        functools.partial(_gla_body, chunk=c, tile=tile, nb=nb, heads=heads, dk=dk, dv=dv, zero_init=zero_init),
        grid=(bsz // nb, t // tile),
        in_specs=[kseq, kseq, kseq, vseq, vseq, st, _resident((1, dv))],
        out_specs=[vseq, st],
        out_shape=[jax.ShapeDtypeStruct((bsz, t, heads * dv), ACT),
                   jax.ShapeDtypeStruct((bsz, heads, dk, dv), F32)],
        scratch_shapes=[pltpu.VMEM((nb * heads, dv, dk), F32), pltpu.VMEM((nb, tile, heads * dk), F32)],
        compiler_params=_params(),
        name="gla_scan",
    )(q, k, g, v, gate, s0, gain.reshape(1, dv))


def _cf_body(x_ref, o_ref, fill_ref, wo_ref, l1w_ref, l1b_ref, wu_ref, wg_ref, wd_ref, cw_ref, cb_ref,
             l2w_ref, l2b_ref, out_ref, z_ref, h_scr, carry_scr, *, rl):
    n_state = CONV_W - 1
    t_idx = pl.program_id(1)
    subs = _sub_blocks(rl.rows)
    sub = subs[0].stop
    if rl.long:
        @pl.when(t_idx == 0)
        def _load_state():
            carry_scr[...] = fill_ref[...]

    x1s = [_ln_rows(DN_ALPHA * x_ref[rs, :] + _dot(o_ref[rs, :], wo_ref[...]), l1w_ref[...], l1b_ref[...])
           for rs in subs]
    for rs, x1 in zip(subs, x1s):
        x1b = x1.astype(BF16)
        for j in range(wu_ref.shape[1] // FFN_COLS):
            cs = slice(j * FFN_COLS, (j + 1) * FFN_COLS)
            zc = _dot(x1b, wg_ref[:, cs])
            uc = _dot(x1b, wu_ref[:, cs])
            carry_c = carry_scr.at[:, cs] if rl.long else None
            fill_c = None if rl.long else fill_ref.at[rs, cs]
            zp1 = _prev_rows(zc, 1, rl, t_idx, carry_c, fill_c, n_state)
            zp2 = _prev_rows(zc, 2, rl, t_idx, carry_c, fill_c, n_state)
            if rl.long:
                carry_scr[:, cs] = zc[sub - n_state:sub, :]
            else:
                z_ref[rs, cs] = zc
            cw = cw_ref[:, cs]
            pre = cb_ref[:, cs] + cw[0:1] * zp2 + cw[1:2] * zp1 + cw[2:3] * zc
            h_scr[rs, cs] = (pre * jax.nn.sigmoid(pre) * uc).astype(BF16)
        out_ref[rs, :] = _ln_rows(DN_ALPHA * x1 + _dot(h_scr[rs, :], wd_ref[...]), l2w_ref[...], l2b_ref[...])
    if rl.long:
        @pl.when(t_idx == pl.num_programs(1) - 1)
        def _store_state():
            z_ref[...] = carry_scr[...]


def _proj_ffn(x, o, buf, wo, ln1, p, ln2, valid):
    bsz, t, d = x.shape
    f = p["wu"].shape[1]
    n_state = CONV_W - 1
    rl = _Rows(bsz, t, valid, FFN_ROWS if t >= FFN_ROWS else FFN_ROWS // 2)
    assert f % FFN_COLS == 0 and valid >= n_state
    seq = rl.seq_spec(d)
    vec = lambda n: _resident((1, n))
    if rl.long:
        z_spec, z_shape = pl.BlockSpec((None, n_state, f), lambda i, j: (i, 0, 0)), (bsz, n_state, f)
    else:
        z_spec, z_shape = rl.seq_spec(f), rl.view + (f,)
    out, z = pl.pallas_call(
        functools.partial(_cf_body, rl=rl),
        grid=rl.grid,
        in_specs=[seq, seq, rl.state_spec(n_state, f),
                  _resident((d, d)), vec(d), vec(d), _resident((d, f)), _resident((d, f)), _resident((f, d)),
                  _resident((CONV_W, f)), vec(f), vec(d), vec(d)],
        out_specs=[seq, z_spec],
        out_shape=[jax.ShapeDtypeStruct(rl.view + (d,), F32), jax.ShapeDtypeStruct(z_shape, F32)],
        scratch_shapes=[pltpu.VMEM((rl.rows, f), BF16), pltpu.VMEM((n_state, f), F32)],
        compiler_params=_params(),
        name="proj_ffn",
    )(rl.flat(x), rl.flat(o), rl.state_rows(buf), wo, ln1[0].reshape(1, d), ln1[1].reshape(1, d),
      p["wu"], p["wg"], p["wd"], p["conv_w"], p["conv_b"].reshape(1, f), ln2[0].reshape(1, d), ln2[1].reshape(1, d))
    if rl.long:
        return out, z
    return rl.unflat(out), rl.unflat(z)[:, valid - n_state:valid]


def _run_trunk(x, valid, st_rw, st_shift, st_hg, st_gl, st_conv, rw, rw_vmix, hg, lower_bounds, gl, ffn, ln, zero_init):
    new_rw, new_shift, new_hg, new_gl, new_conv = [], [], [], [], []
    v_first = None
    for i in range(DEPTH):
        j = i // N_MIXERS
        kind = i % N_MIXERS
        if kind == 0:
            p = {n: q[j] for n, q in rw.items()}
            vm = None if j == 0 else tuple(q[j - 1] for q in rw_vmix)
            r, wl, k, v, a, b, g = _rwkv_pre(x, st_shift[j], v_first, vm, p, valid)
            if vm is None:
                v_first = v
            o, s_t = _rwkv_scan(r, wl, k, v, a, b, g, st_rw[j], p, zero_init)
            new_rw.append(s_t)
            new_shift.append(x[:, valid - 1])
        elif kind == 1:
            p = {n: q[j] for n, q in hg.items()}
            q_, k, log_g, val, gate = _hgrn_pre(x, lower_bounds[i], p, valid)
            o, s_t = _gla_scan(q_, k, log_g, val, gate, st_hg[j], p["norm_w"], HG_HEADS, HG_HEAD, HG_HEAD, zero_init)
            new_hg.append(s_t)
        else:
            p = {n: q[j] for n, q in gl.items()}
            q_, k, log_g, val, gate = _gla_pre(x, p, valid)
            o, s_t = _gla_scan(q_, k, log_g, val, gate, st_gl[j], p["norm_w"], GL_HEADS, GL_DK, GL_DV, zero_init)
            new_gl.append(s_t)
        x, buf = _proj_ffn(x, o, st_conv[i], p["wo"], (ln["ln1_w"][i], ln["ln1_b"][i]),
                           {n: q[i] for n, q in ffn.items()}, (ln["ln2_w"][i], ln["ln2_b"][i]), valid)
        new_conv.append(buf)
    return (x[:, :valid], jnp.stack(new_rw), jnp.stack(new_shift), jnp.stack(new_hg), jnp.stack(new_gl),
            jnp.stack(new_conv))


def kernel(x_prompt, x_sample, state_rwkv, state_rwkv_shift, state_hgrn, state_gla, state_ffn_conv, rw_mix, rw_wr, rw_wk, rw_wv, rw_wo, rw_w0, rw_w1, rw_w2, rw_a0, rw_a1, rw_a2, rw_g1, rw_g2, rw_k_k, rw_k_a, rw_r_k, rw_lnx_w, rw_lnx_b, rw_v0, rw_v1, rw_v2, hg_wq, hg_wf, hg_wi, hg_wg, hg_wo, hg_norm_w, hg_lb_param, gl_wq, gl_wk, gl_wv, gl_wg, gl_gk1, gl_gk2, gl_gk_b, gl_wo, gl_norm_w, ffn_wu, ffn_wg, ffn_conv_w, ffn_conv_b, ffn_wd, ln1_w, ln1_b, ln2_w, ln2_b):
    bf = lambda w: w.astype(BF16)
    rw = dict(mix=rw_mix, wr=bf(rw_wr), wk=bf(rw_wk), wv=bf(rw_wv), wo=bf(rw_wo), w0=rw_w0, w1=bf(rw_w1),
              w2=bf(rw_w2), a0=rw_a0, a1=bf(rw_a1), a2=bf(rw_a2), g1=bf(rw_g1), g2=bf(rw_g2), k_k=rw_k_k,
              k_a=rw_k_a, r_k=rw_r_k, lnx_w=rw_lnx_w, lnx_b=rw_lnx_b)
    rw_vmix = (rw_v0, bf(rw_v1), bf(rw_v2))
    hg = dict(wq=bf(hg_wq), wf=bf(hg_wf), wi=bf(hg_wi), wg=bf(hg_wg), wo=bf(hg_wo), norm_w=hg_norm_w)
    gl = dict(wq=bf(gl_wq), wk=bf(gl_wk), wv=bf(gl_wv), wg=bf(gl_wg), gk1=bf(gl_gk1), gk2=bf(gl_gk2),
              gk_b=gl_gk_b, wo=bf(gl_wo), norm_w=gl_norm_w)
    ffn = dict(wu=bf(ffn_wu), wg=bf(ffn_wg), conv_w=ffn_conv_w, conv_b=ffn_conv_b, wd=bf(ffn_wd))
    ln = dict(ln1_w=ln1_w, ln1_b=ln1_b, ln2_w=ln2_w, ln2_b=ln2_b)
    lb_soft = jax.nn.softmax(hg_lb_param, axis=0)
    lower_bounds = jnp.cumsum(lb_soft, axis=0) - lb_soft[0]

    nb = x_prompt.shape[0]
    zeros = lambda s: jnp.zeros((s.shape[0], nb) + s.shape[2:], s.dtype)
    p_out = _run_trunk(x_prompt, x_prompt.shape[1], zeros(state_rwkv), zeros(state_rwkv_shift), zeros(state_hgrn),
                       zeros(state_gla), zeros(state_ffn_conv), rw, rw_vmix, hg, lower_bounds, gl, ffn, ln, True)
    t_s = x_sample.shape[1]
    t_pad = max(SUBLANES, 1 << (t_s - 1).bit_length())
    xs = jnp.pad(x_sample, ((0, 0), (0, t_pad - t_s), (0, 0)))
    s_out = _run_trunk(xs, t_s, state_rwkv, state_rwkv_shift, state_hgrn, state_gla, state_ffn_conv,
                       rw, rw_vmix, hg, lower_bounds, gl, ffn, ln, False)
    return (p_out[0], s_out[0]) + tuple(p_out[1:]) + tuple(s_out[1:])
```

```python
import functools
import math

import jax
import jax.numpy as jnp
from jax import lax
from jax.experimental import pallas as pl
from jax.experimental.pallas import tpu as pltpu

D_MODEL = 1024
DEPTH = 4
N_MIXERS = 3
RW_HEAD = 64
RW_HEADS = D_MODEL // RW_HEAD
RW_HEAD_SHIFT = RW_HEAD.bit_length() - 1
RW_LNX_EPS = 64e-5
HG_HEAD = 128
HG_HEADS = D_MODEL // HG_HEAD
GL_HEADS = 4
GL_DK = (D_MODEL // 2) // GL_HEADS
GL_DV = D_MODEL // GL_HEADS
GL_GATE_NORM = 16.0
CONV_W = 3
LN_EPS = 1e-5
RMS_EPS = 1e-5
DN_ALPHA = (2 * DEPTH) ** 0.25

LANES = 128
SUBLANES = 8
RW_GROUP = 256
RW_GROUP_HEADS = RW_GROUP // RW_HEAD
CHUNK = 32
RW_CHUNK = 64
RW_BATCH_PER_STEP = 2
RW_BATCH_PER_STEP_SHORT = 4
GLA_BATCH_PER_STEP = 4
SCAN_ROWS = 256
PRE_ROWS = 512
FFN_ROWS = 512
SUB_ROWS = 256
FFN_COLS = 256
VMEM_LIMIT = 56 * 1024 * 1024

BF16 = jnp.bfloat16
F32 = jnp.float32
ACT = BF16


def _dot(a, b):
    return jnp.dot(a, b, preferred_element_type=F32)


def _dot_nt(a, b):
    return lax.dot_general(a, b, (((1,), (1,)), ((), ())), preferred_element_type=F32)


def _dot_tn(a, b):
    return lax.dot_general(a, b, (((0,), (0,)), ((), ())), preferred_element_type=F32)


def _dot_hilo_rhs(sel, x):
    hi = x.astype(BF16)
    lo = (x - hi.astype(F32)).astype(BF16)
    return _dot(sel, hi) + _dot(sel, lo)


def _head_sums(x, ones_bd):
    return _dot(x.astype(BF16), ones_bd)


def _iota(shape, dim):
    return lax.broadcasted_iota(jnp.int32, shape, dim)


def _chunk_tri(n, c):
    row, col = _iota((n, n), 0), _iota((n, n), 1)
    sh = int(math.log2(c))
    keep = (col <= row) & ((row >> sh) == (col >> sh))
    return jnp.where(keep, 1.0, 0.0).astype(BF16)


def _softplus(t):
    return jnp.maximum(t, 0.0) + jnp.log(1.0 + jnp.exp(-jnp.abs(t)))


def _ln_rows(x, w, b):
    mu = jnp.mean(x, -1, keepdims=True)
    xc = x - mu
    var = jnp.mean(xc * xc, -1, keepdims=True)
    return xc * lax.rsqrt(var + LN_EPS) * w + b


def _resident(shape):
    return pl.BlockSpec(shape, lambda i, j: (0,) * len(shape), pipeline_mode=pl.Buffered(1))


def _sub_blocks(rows):
    sub = min(rows, SUB_ROWS)
    assert rows % sub == 0
    return [slice(s * sub, (s + 1) * sub) for s in range(rows // sub)]


def _params():
    return pltpu.CompilerParams(dimension_semantics=("arbitrary", "arbitrary"), vmem_limit_bytes=VMEM_LIMIT)


class _Rows:
    def __init__(self, bsz, t, valid, max_rows):
        self.bsz, self.t, self.valid = bsz, t, valid
        self.long = t >= max_rows
        if self.long:
            assert t % max_rows == 0 and valid == t
            self.rows, self.grid, self.view = max_rows, (bsz, t // max_rows), (bsz, t)
        else:
            total = bsz * t
            self.rows = min(total, max_rows)
            assert t & (t - 1) == 0 and self.rows % t == 0 and total % self.rows == 0
            self.grid, self.view = (1, total // self.rows), (1, total)

    def flat(self, x):
        return x.reshape(self.view + x.shape[2:])

    def unflat(self, x):
        return x.reshape((self.bsz, self.t) + x.shape[2:])

    def seq_spec(self, width):
        return pl.BlockSpec((None, self.rows, width), lambda i, j: (i, j, 0))

    def state_rows(self, state):
        if self.long:
            return state
        return self.flat(jnp.pad(state, ((0, 0), (0, self.t - state.shape[1]), (0, 0))))

    def state_spec(self, n, width):
        if self.long:
            return pl.BlockSpec((None, n, width), lambda i, j: (i, 0, 0))
        return self.seq_spec(width)


def _prev_rows(x, shift, rl, t_idx, carry_scr, fill_ref, n_prev):
    rows, width = x.shape
    row = _iota((rows, width), 0)
    rolled = pltpu.roll(x, shift, 0)
    if rl.long:
        out = rolled
        for i in range(shift):
            src = n_prev - shift + i
            out = jnp.where(row == i, carry_scr[src:src + 1, :], out)
        return out
    fill = fill_ref[...]
    if n_prev != shift:
        fill = pltpu.roll(fill, rows - (n_prev - shift), 0)
    return jnp.where((row & (rl.t - 1)) >= shift, rolled, fill)


def _rwkv_pre_body(*refs, rl, has_vmix):
    it = iter(refs)
    x_ref, fill_ref = next(it), next(it)
    vf_ref = next(it) if has_vmix else None
    mix_ref, wr_ref, wk_ref, wv_ref, w0_ref, w1_ref, w2_ref = (next(it) for _ in range(7))
    a0_ref, a1_ref, a2_ref, g1_ref, g2_ref, kk_ref, ka_ref = (next(it) for _ in range(7))
    v0_ref, v1_ref, v2_ref = (next(it) for _ in range(3)) if has_vmix else (None, None, None)
    r_out, wl_out, k_out, v_out, a_out, b_out, g_out, carry_scr = (next(it) for _ in range(8))

    t_idx = pl.program_id(1)
    if rl.long:
        @pl.when(t_idx == 0)
        def _load_state():
            carry_scr[0:1, :] = fill_ref[...]
    bd = (_iota((RW_GROUP, RW_GROUP), 0) >> RW_HEAD_SHIFT) == (_iota((RW_GROUP, RW_GROUP), 1) >> RW_HEAD_SHIFT)
    ones_bd = jnp.where(bd, 1.0, 0.0).astype(BF16)

    subs = _sub_blocks(rl.rows)
    stage1 = []
    for rs in subs:
        rows = rs.stop - rs.start
        x = x_ref[rs, :]
        x_prev = _prev_rows(x, 1, rl, t_idx, carry_scr, None if rl.long else fill_ref.at[rs, :], 1)
        if rl.long:
            carry_scr[0:1, :] = x[rows - 1:rows, :]
        xx = x_prev - x
        xr, xw, xk, xv, xa, xg = [(x + xx * mix_ref[j:j + 1, :]).astype(BF16) for j in range(6)]
        stage1.append(dict(r=_dot(xr, wr_ref[...]), k=_dot(xk, wk_ref[...]), v=_dot(xv, wv_ref[...]),
                           lw=_dot(xw, w1_ref[...]), la=_dot(xa, a1_ref[...]), lg=_dot(xg, g1_ref[...]),
                           lv=_dot(xv, v1_ref[...]) if has_vmix else None))
    for rs, s1 in zip(subs, stage1):
        rows = rs.stop - rs.start
        r, k, v = s1["r"], s1["k"], s1["v"]
        lora_w = _dot(jnp.tanh(s1["lw"]).astype(BF16), w2_ref[...])
        wl = -jnp.exp(-_softplus(-(w0_ref[...] + lora_w)) - 0.5)
        a = jax.nn.sigmoid(a0_ref[...] + _dot(s1["la"].astype(BF16), a2_ref[...]))
        if has_vmix:
            gate = jax.nn.sigmoid(v0_ref[...] + _dot(s1["lv"].astype(BF16), v2_ref[...]))
            v = v + (vf_ref[rs, :].astype(F32) - v) * gate
        g_out[rs, :] = _dot(jax.nn.sigmoid(s1["lg"]).astype(BF16), g2_ref[...]).astype(g_out.dtype)

        kk = k * kk_ref[...]
        k = k * (1.0 + (a - 1.0) * ka_ref[...])
        if not rl.long and rl.valid < rl.t:
            live = (_iota((rows, D_MODEL), 0) & (rl.t - 1)) < rl.valid
            wl, k, v, kk = (jnp.where(live, z, 0.0) for z in (wl, k, v, kk))
        r_out[rs, :] = r.astype(r_out.dtype)
        wl_out[rs, :] = wl
        k_out[rs, :] = k.astype(k_out.dtype)
        v_out[rs, :] = v.astype(v_out.dtype)
        for g in range(D_MODEL // RW_GROUP):
            lanes = slice(g * RW_GROUP, (g + 1) * RW_GROUP)
            kkg = kk[:, lanes]
            kkn = kkg / jnp.maximum(jnp.sqrt(_head_sums(kkg * kkg, ones_bd)), 1e-12)
            a_out[rs, lanes] = (-kkn).astype(a_out.dtype)
            b_out[rs, lanes] = (kkn * a[:, lanes]).astype(b_out.dtype)


def _rwkv_pre(x, shift, v_first, vmix, p, valid):
    bsz, t, d = x.shape
    rl = _Rows(bsz, t, valid, PRE_ROWS)
    has_vmix = vmix is not None
    seq = rl.seq_spec(d)
    vec = lambda z: z.reshape(1, d)
    args = [rl.flat(x), rl.state_rows(shift[:, None, :])]
    specs = [seq, rl.state_spec(1, d)]
    if has_vmix:
        args.append(rl.flat(v_first))
        specs.append(seq)
    weights = [p["mix"], p["wr"], p["wk"], p["wv"], vec(p["w0"]), p["w1"], p["w2"], vec(p["a0"]), p["a1"], p["a2"],
               p["g1"], p["g2"], vec(p["k_k"]), vec(p["k_a"])]
    if has_vmix:
        weights += [vec(vmix[0]), vmix[1], vmix[2]]
    args += weights
    specs += [_resident(w.shape) for w in weights]
    outs = pl.pallas_call(
        functools.partial(_rwkv_pre_body, rl=rl, has_vmix=has_vmix),
        grid=rl.grid,
        in_specs=specs,
        out_specs=[seq] * 7,
        out_shape=[jax.ShapeDtypeStruct(rl.view + (d,), F32 if i == 1 else ACT) for i in range(7)],
        scratch_shapes=[pltpu.VMEM((SUBLANES, d), F32)],
        compiler_params=_params(),
        name="rwkv_pre",
    )(*args)
    return [rl.unflat(o) for o in outs]


def _rwkv_body(r_ref, wl_ref, k_ref, v_ref, a_ref, b_ref, g_ref, s0_ref, rk_ref, lw_ref, lb_ref, *rest,
               chunk, tile, nb, zero_init):
    y_ref, st_ref, s_scr, l_scr = rest[-4:]
    c = chunk
    gh = RW_GROUP_HEADS
    n_groups = D_MODEL // RW_GROUP
    t_idx = pl.program_id(1)
    log2c = int(math.log2(c))

    bd_r, bd_c = _iota((RW_GROUP, RW_GROUP), 0), _iota((RW_GROUP, RW_GROUP), 1)
    bd256 = (bd_r >> RW_HEAD_SHIFT) == (bd_c >> RW_HEAD_SHIFT)
    ones_bd = jnp.where(bd256, 1.0, 0.0).astype(BF16)
    hm = (_iota((gh * c, RW_GROUP), 0) >> log2c) == (_iota((gh * c, RW_GROUP), 1) >> RW_HEAD_SHIFT)
    hm_b = jnp.where(hm, 1.0, 0.0).astype(BF16)
    bdm = (_iota((gh * c, gh * c), 0) >> log2c) == (_iota((gh * c, gh * c), 1) >> log2c)
    bdm_b = jnp.where(bdm, 1.0, 0.0).astype(BF16)
    trow = _iota((c, gh * c), 0)
    tcol = _iota((c, gh * c), 1) & (c - 1)
    strict = tcol < trow
    incl = tcol <= trow
    chains = [(n, g) for n in range(nb) for g in range(n_groups)]

    def head_block(i, n, g, h):
        blk = slice(h * RW_HEAD, (h + 1) * RW_HEAD)
        return (i, blk, blk), (n, slice(g * RW_GROUP + h * RW_HEAD, g * RW_GROUP + (h + 1) * RW_HEAD), slice(None))

    @pl.when(t_idx == 0)
    def _init():
        for i, (n, g) in enumerate(chains):
            s_scr[i] = jnp.zeros((RW_GROUP, RW_GROUP), F32)
            if not zero_init:
                for h in range(gh):
                    dst, src = head_block(i, n, g, h)
                    s_scr[dst] = s0_ref[src]

    tri = _chunk_tri(tile, c)
    for n in range(nb):
        l_scr[n] = _dot_hilo_rhs(tri, wl_ref[n])

    def stack(x_b):
        return jnp.concatenate([x_b] * gh, axis=0) * hm_b

    def bd_stack(m_b):
        return jnp.concatenate([m_b] * gh, axis=0) * bdm_b

    def chunk_step(ci, carry):
        rows = pl.ds(pl.multiple_of(ci * c, c), c)
        ld = []
        for n, g in chains:
            lanes = slice(g * RW_GROUP, (g + 1) * RW_GROUP)
            f32 = lambda ref: ref[n, rows, lanes].astype(F32)
            ld.append(dict(r=f32(r_ref), wl=wl_ref[n, rows, lanes], k=f32(k_ref), v=v_ref[n, rows, lanes].astype(BF16),
                           a=f32(a_ref), b=f32(b_ref), lc=l_scr[n, rows, lanes]))
        for i, x in enumerate(ld):
            lc = x["lc"]
            x["lend"] = lc[c - 1:c, :]
            w_inv = jnp.exp(-lc)
            at = x["a"] * jnp.exp(lc - x["wl"])
            rt = x["r"] * jnp.exp(lc)
            x["ar"] = jnp.concatenate([at, rt], axis=0).astype(BF16)
            x["bts"] = stack((x["b"] * w_inv).astype(BF16))
            x["kts"] = stack((x["k"] * w_inv).astype(BF16))
            x["vs"] = stack(x["v"])
            x["sg"] = s_scr[i]
        for x in ld:
            x["sb"] = _dot_nt(x["ar"], x["bts"])
            x["sk"] = _dot_nt(x["ar"], x["kts"])
            x["x2"] = _dot_nt(x["ar"], x["sg"].astype(BF16))
        for x, (n, g) in zip(ld, chains):
            lanes = slice(g * RW_GROUP, (g + 1) * RW_GROUP)
            x["bonus"] = _head_sums(x["r"] * x["k"] * rk_ref[:, lanes], ones_bd) * x["v"].astype(F32)
        for x in ld:
            x["p"] = jnp.where(strict, x["sb"][:c], 0.0).astype(BF16)
            x["arb"] = jnp.where(incl, x["sb"][c:], 0.0).astype(BF16)
            kv = _dot(jnp.where(jnp.concatenate([strict, incl], axis=0), x["sk"], 0.0).astype(BF16), x["vs"])
            x["u"] = x["x2"][:c] + kv[:c]
            x["ykv"] = x["x2"][c:] + kv[c:]
        for it in range(log2c):
            for x in ld:
                x["u"] = x["u"] + _dot(x["p"], stack(x["u"].astype(BF16)))
            if it < log2c - 1:
                for x in ld:
                    x["p"] = _dot(x["p"], bd_stack(x["p"])).astype(BF16)
        for x in ld:
            x["y"] = x["ykv"] + _dot(x["arb"], stack(x["u"].astype(BF16)))
        for i, x in enumerate(ld):
            w_end = jnp.exp(x["lend"] - x["lc"])
            uv = jnp.concatenate([x["u"].astype(BF16), x["v"]], axis=0)
            bk = jnp.concatenate([x["b"] * w_end, x["k"] * w_end], axis=0).astype(BF16)
            s_scr[i] = x["sg"] * jnp.exp(x["lend"]) + jnp.where(bd256, _dot_tn(uv, bk), 0.0)
        inv_n = 1.0 / RW_HEAD
        for x in ld:
            x["yc"] = x["y"] - _head_sums(x["y"], ones_bd) * inv_n
        for x, (n, g) in zip(ld, chains):
            lanes = slice(g * RW_GROUP, (g + 1) * RW_GROUP)
            var = _head_sums(x["yc"] * x["yc"], ones_bd) * inv_n
            yn = x["yc"] * lax.rsqrt(var + RW_LNX_EPS) * lw_ref[:, lanes] + lb_ref[:, lanes]
            y_ref[n, rows, lanes] = ((yn + x["bonus"]) * g_ref[n, rows, lanes].astype(F32)).astype(y_ref.dtype)
        return carry

    lax.fori_loop(0, tile // c, chunk_step, 0)

    @pl.when(t_idx == pl.num_programs(1) - 1)
    def _final():
        for i, (n, g) in enumerate(chains):
            for h in range(gh):
                src, dst = head_block(i, n, g, h)
                st_ref[dst] = s_scr[src]


def _rwkv_scan(r, wl, k, v, a, b, g, s0, p, zero_init, layer, n_layers, stacked):
    bsz, t, d = r.shape
    c = min(RW_CHUNK, t)
    tile = min(t, SCAN_ROWS)
    nb = RW_BATCH_PER_STEP if t > c else RW_BATCH_PER_STEP_SHORT
    assert t % tile == 0 and tile % c == 0 and c % SUBLANES == 0 and bsz % nb == 0
    s0r = s0.reshape(bsz, d, RW_HEAD)
    seq = pl.BlockSpec((nb, tile, d), lambda i, j: (i, j, 0))
    st = pl.BlockSpec((nb, d, RW_HEAD), lambda i, j: (i, 0, 0))
    vec = _resident((1, d))
    st_out = pl.BlockSpec((None, nb, d, RW_HEAD), lambda i, j: (layer, i, 0, 0))
    args = [r, wl, k, v, a, b, g, s0r, p["r_k"].reshape(1, d), p["lnx_w"].reshape(1, d), p["lnx_b"].reshape(1, d)]
    in_specs = [seq] * 7 + [st, vec, vec, vec]
    aliases = {}
    if stacked is not None:
        aliases = {len(args): 1}
        args.append(stacked)
        in_specs.append(pl.BlockSpec(memory_space=pl.ANY))
    return pl.pallas_call(
        functools.partial(_rwkv_body, chunk=c, tile=tile, nb=nb, zero_init=zero_init),
        grid=(bsz // nb, t // tile),
        in_specs=in_specs,
        out_specs=[seq, st_out],
        out_shape=[jax.ShapeDtypeStruct((bsz, t, d), ACT), jax.ShapeDtypeStruct((n_layers, bsz, d, RW_HEAD), F32)],
        scratch_shapes=[pltpu.VMEM((nb * (d // RW_GROUP), RW_GROUP, RW_GROUP), F32),
                        pltpu.VMEM((nb, tile, d), F32)],
        input_output_aliases=aliases,
        compiler_params=_params(),
        name="rwkv_scan",
    )(*args)


def _hgrn_pre_body(x_ref, w_ref, llb_ref, l1m_ref, omlb_ref, q_out, k_out, g_out, i_out, gate_out, *, rl):
    d = D_MODEL
    subs = _sub_blocks(rl.rows)
    ys = [_dot(x_ref[rs, :].astype(BF16), w_ref[...]) for rs in subs]
    for rs, y in reversed(list(zip(subs, ys))):
        qz, f = y[:, 0:d], y[:, d:2 * d]
        i_out[rs, :] = y[:, 2 * d:3 * d].astype(i_out.dtype)
        gate_out[rs, :] = y[:, 3 * d:4 * d].astype(gate_out.dtype)
        q_out[rs, :] = (qz * jax.nn.sigmoid(qz) * HG_HEAD ** -0.5).astype(q_out.dtype)
        la = llb_ref[...]
        lb = l1m_ref[...] - _softplus(-f)
        log_g = jnp.maximum(la, lb) + jnp.log(1.0 + jnp.exp(-jnp.abs(la - lb)))
        k = omlb_ref[...] * jax.nn.sigmoid(-f)
        if not rl.long and rl.valid < rl.t:
            live = (_iota(f.shape, 0) & (rl.t - 1)) < rl.valid
            log_g, k = jnp.where(live, log_g, 0.0), jnp.where(live, k, 0.0)
        g_out[rs, :] = log_g
        k_out[rs, :] = k.astype(k_out.dtype)


def _hgrn_pre(x, lb, p, valid):
    bsz, t, d = x.shape
    rl = _Rows(bsz, t, valid, PRE_ROWS)
    seq = rl.seq_spec(d)
    vecs = [jnp.log(lb).reshape(1, d), jnp.log1p(-lb).reshape(1, d), (1.0 - lb).reshape(1, d)]
    weights = [jnp.concatenate([p["wq"], p["wf"], p["wi"], p["wg"]], axis=1)] + vecs
    outs = pl.pallas_call(
        functools.partial(_hgrn_pre_body, rl=rl),
        grid=rl.grid,
        in_specs=[seq] + [_resident(w.shape) for w in weights],
        out_specs=[seq] * 5,
        out_shape=[jax.ShapeDtypeStruct(rl.view + (d,), F32 if i == 2 else ACT) for i in range(5)],
        compiler_params=_params(),
        name="hgrn_pre",
    )(rl.flat(x), *weights)
    return [rl.unflat(o) for o in outs]


def _gla_pre_body(x_ref, w_ref, gk2_ref, gkb_ref, q_out, k_out, g_out, v_out, gate_out, *, rl):
    d, dkk = D_MODEL, GL_HEADS * GL_DK
    subs = _sub_blocks(rl.rows)
    ys = [_dot(x_ref[rs, :].astype(BF16), w_ref[...]) for rs in subs]
    for rs, y in zip(subs, ys):
        q_out[rs, :] = (y[:, 0:dkk] * GL_DK ** -0.5).astype(q_out.dtype)
        k = y[:, dkk:2 * dkk]
        v_out[rs, :] = y[:, 2 * dkk:2 * dkk + d].astype(v_out.dtype)
        gate_out[rs, :] = y[:, 2 * dkk + d:2 * dkk + 2 * d].astype(gate_out.dtype)
        gk = _dot(y[:, 2 * dkk + 2 * d:].astype(BF16), gk2_ref[...]) + gkb_ref[...]
        log_g = -_softplus(-gk) * (1.0 / GL_GATE_NORM)
        if not rl.long and rl.valid < rl.t:
            live = (_iota(k.shape, 0) & (rl.t - 1)) < rl.valid
            log_g, k = jnp.where(live, log_g, 0.0), jnp.where(live, k, 0.0)
        g_out[rs, :] = log_g
        k_out[rs, :] = k.astype(k_out.dtype)


def _gla_pre(x, p, valid):
    bsz, t, d = x.shape
    dkk = GL_HEADS * GL_DK
    rl = _Rows(bsz, t, valid, PRE_ROWS)
    rank = p["gk1"].shape[1]
    gk1 = jnp.pad(p["gk1"], ((0, 0), (0, LANES - rank)))
    gk2 = jnp.pad(p["gk2"], ((0, LANES - rank), (0, 0)))
    weights = [jnp.concatenate([p["wq"], p["wk"], p["wv"], p["wg"], gk1], axis=1), gk2, p["gk_b"].reshape(1, dkk)]
    kseq, vseq = rl.seq_spec(dkk), rl.seq_spec(d)
    kshape, vshape = jax.ShapeDtypeStruct(rl.view + (dkk,), ACT), jax.ShapeDtypeStruct(rl.view + (d,), ACT)
    outs = pl.pallas_call(
        functools.partial(_gla_pre_body, rl=rl),
        grid=rl.grid,
        in_specs=[vseq] + [_resident(w.shape) for w in weights],
        out_specs=[kseq, kseq, kseq, vseq, vseq],
        out_shape=[kshape, kshape, jax.ShapeDtypeStruct(rl.view + (dkk,), F32), vshape, vshape],
        compiler_params=_params(),
        name="gla_pre",
    )(rl.flat(x), *weights)
    return [rl.unflat(o) for o in outs]


def _gla_body(q_ref, k_ref, g_ref, v_ref, gate_ref, s0_ref, gain_ref, o_ref, st_ref, s_scr, l_scr,
              *, chunk, tile, nb, heads, dk, dv, zero_init):
    c = chunk
    ref_row = (c - 1) // 2
    t_idx = pl.program_id(1)
    incl = _iota((c, c), 1) <= _iota((c, c), 0)
    chains = [(n, h) for n in range(nb) for h in range(heads)]

    @pl.when(t_idx == 0)
    def _init():
        for i, (n, h) in enumerate(chains):
            s_scr[i] = jnp.zeros((dv, dk), F32) if zero_init else s0_ref[n, h].T

    tri = _chunk_tri(tile, c)
    for n in range(nb):
        l_scr[n] = _dot_hilo_rhs(tri, g_ref[n])

    def chunk_step(ci, carry):
        rows = pl.ds(pl.multiple_of(ci * c, c), c)
        ld = []
        for i, (n, h) in enumerate(chains):
            kl = slice(h * dk, (h + 1) * dk)
            q, k = q_ref[n, rows, kl].astype(F32), k_ref[n, rows, kl].astype(F32)
            bc = l_scr[n, rows, kl]
            b_ref = bc[ref_row:ref_row + 1, :]
            b_last = bc[c - 1:c, :]
            ld.append(dict(qi=(q * jnp.exp(bc - b_ref)).astype(BF16), ki=(k * jnp.exp(b_ref - bc)).astype(BF16),
                           qd=(q * jnp.exp(bc)).astype(BF16), kd=(k * jnp.exp(b_last - bc)).astype(BF16),
                           v=v_ref[n, rows, h * dv:(h + 1) * dv].astype(BF16), w_last=jnp.exp(b_last),
                           st=s_scr[i]))
        for x in ld:
            x["att"] = _dot_nt(x["qi"], x["ki"])
            x["os"] = _dot_nt(x["qd"], x["st"].astype(BF16))
            x["ds"] = _dot_tn(x["v"], x["kd"])
        for i, (x, (n, h)) in enumerate(zip(ld, chains)):
            vl = slice(h * dv, (h + 1) * dv)
            s_scr[i] = x["st"] * x["w_last"] + x["ds"]
            o = x["os"] + _dot(jnp.where(incl, x["att"], 0.0).astype(BF16), x["v"])
            gate = gate_ref[n, rows, vl].astype(F32)
            on = o * lax.rsqrt(jnp.mean(o * o, -1, keepdims=True) + RMS_EPS) * gain_ref[...]
            o_ref[n, rows, vl] = (on * (gate * jax.nn.sigmoid(gate))).astype(o_ref.dtype)
        return carry

    lax.fori_loop(0, tile // c, chunk_step, 0)

    @pl.when(t_idx == pl.num_programs(1) - 1)
    def _final():
        for i, (n, h) in enumerate(chains):
            st_ref[n, h] = s_scr[i].T


def _gla_scan(q, k, g, v, gate, s0, gain, heads, dk, dv, zero_init):
    bsz, t, _ = q.shape
    c = min(CHUNK, t)
    tile = min(t, SCAN_ROWS)
    nb = GLA_BATCH_PER_STEP
    assert t % tile == 0 and tile % c == 0 and c % SUBLANES == 0 and bsz % nb == 0
    kseq = pl.BlockSpec((nb, tile, heads * dk), lambda i, j: (i, j, 0))
    vseq = pl.BlockSpec((nb, tile, heads * dv), lambda i, j: (i, j, 0))
    st = pl.BlockSpec((nb, heads, dk, dv), lambda i, j: (i, 0, 0, 0))
    return pl.pallas_call(
        functools.partial(_gla_body, chunk=c, tile=tile, nb=nb, heads=heads, dk=dk, dv=dv, zero_init=zero_init),
        grid=(bsz // nb, t // tile),
        in_specs=[kseq, kseq, kseq, vseq, vseq, st, _resident((1, dv))],
        out_specs=[vseq, st],
        out_shape=[jax.ShapeDtypeStruct((bsz, t, heads * dv), ACT),
                   jax.ShapeDtypeStruct((bsz, heads, dk, dv), F32)],
        scratch_shapes=[pltpu.VMEM((nb * heads, dv, dk), F32), pltpu.VMEM((nb, tile, heads * dk), F32)],
        compiler_params=_params(),
        name="gla_scan",
    )(q, k, g, v, gate, s0, gain.reshape(1, dv))


def _cf_body(x_ref, o_ref, fill_ref, wo_ref, l1w_ref, l1b_ref, wu_ref, wg_ref, wd_ref, cw_ref, cb_ref,
             l2w_ref, l2b_ref, out_ref, z_ref, h_scr, carry_scr, *, rl):
    n_state = CONV_W - 1
    t_idx = pl.program_id(1)
    subs = _sub_blocks(rl.rows)
    sub = subs[0].stop
    if rl.long:
        @pl.when(t_idx == 0)
        def _load_state():
            carry_scr[...] = fill_ref[...]

    x1s = [_ln_rows(DN_ALPHA * x_ref[rs, :] + _dot(o_ref[rs, :], wo_ref[...]), l1w_ref[...], l1b_ref[...])
           for rs in subs]
    for rs, x1 in zip(subs, x1s):
        x1b = x1.astype(BF16)
        for j in range(wu_ref.shape[1] // FFN_COLS):
            cs = slice(j * FFN_COLS, (j + 1) * FFN_COLS)
            zc = _dot(x1b, wg_ref[:, cs])
            uc = _dot(x1b, wu_ref[:, cs])
            carry_c = carry_scr.at[:, cs] if rl.long else None
            fill_c = None if rl.long else fill_ref.at[rs, cs]
            zp1 = _prev_rows(zc, 1, rl, t_idx, carry_c, fill_c, n_state)
            zp2 = _prev_rows(zc, 2, rl, t_idx, carry_c, fill_c, n_state)
            if rl.long:
                carry_scr[:, cs] = zc[sub - n_state:sub, :]
            else:
                z_ref[rs, cs] = zc
            cw = cw_ref[:, cs]
            pre = cb_ref[:, cs] + cw[0:1] * zp2 + cw[1:2] * zp1 + cw[2:3] * zc
            h_scr[rs, cs] = (pre * jax.nn.sigmoid(pre) * uc).astype(BF16)
        out_ref[rs, :] = _ln_rows(DN_ALPHA * x1 + _dot(h_scr[rs, :], wd_ref[...]), l2w_ref[...], l2b_ref[...])
    if rl.long:
        @pl.when(t_idx == pl.num_programs(1) - 1)
        def _store_state():
            z_ref[...] = carry_scr[...]


def _proj_ffn(x, o, buf, wo, ln1, p, ln2, valid):
    bsz, t, d = x.shape
    f = p["wu"].shape[1]
    n_state = CONV_W - 1
    rl = _Rows(bsz, t, valid, FFN_ROWS if t >= FFN_ROWS else FFN_ROWS // 2)
    assert f % FFN_COLS == 0 and valid >= n_state
    seq = rl.seq_spec(d)
    vec = lambda n: _resident((1, n))
    if rl.long:
        z_spec, z_shape = pl.BlockSpec((None, n_state, f), lambda i, j: (i, 0, 0)), (bsz, n_state, f)
    else:
        z_spec, z_shape = rl.seq_spec(f), rl.view + (f,)
    out, z = pl.pallas_call(
        functools.partial(_cf_body, rl=rl),
        grid=rl.grid,
        in_specs=[seq, seq, rl.state_spec(n_state, f),
                  _resident((d, d)), vec(d), vec(d), _resident((d, f)), _resident((d, f)), _resident((f, d)),
                  _resident((CONV_W, f)), vec(f), vec(d), vec(d)],
        out_specs=[seq, z_spec],
        out_shape=[jax.ShapeDtypeStruct(rl.view + (d,), F32), jax.ShapeDtypeStruct(z_shape, F32)],
        scratch_shapes=[pltpu.VMEM((rl.rows, f), BF16), pltpu.VMEM((n_state, f), F32)],
        compiler_params=_params(),
        name="proj_ffn",
    )(rl.flat(x), rl.flat(o), rl.state_rows(buf), wo, ln1[0].reshape(1, d), ln1[1].reshape(1, d),
      p["wu"], p["wg"], p["wd"], p["conv_w"], p["conv_b"].reshape(1, f), ln2[0].reshape(1, d), ln2[1].reshape(1, d))
    if rl.long:
        return out, z
    return rl.unflat(out), rl.unflat(z)[:, valid - n_state:valid]


def _run_trunk(x, valid, st_rw, st_shift, st_hg, st_gl, st_conv, rw, rw_vmix, hg, lower_bounds, gl, ffn, ln, zero_init):
    new_shift, new_hg, new_gl, new_conv = [], [], [], []
    v_first = rw_states = None
    for i in range(DEPTH):
        j = i // N_MIXERS
        kind = i % N_MIXERS
        if kind == 0:
            p = {n: q[j] for n, q in rw.items()}
            vm = None if j == 0 else tuple(q[j - 1] for q in rw_vmix)
            r, wl, k, v, a, b, g = _rwkv_pre(x, st_shift[j], v_first, vm, p, valid)
            if vm is None:
                v_first = v
            o, rw_states = _rwkv_scan(r, wl, k, v, a, b, g, st_rw[j], p, zero_init, j, st_rw.shape[0], rw_states)
            new_shift.append(x[:, valid - 1])
        elif kind == 1:
            p = {n: q[j] for n, q in hg.items()}
            q_, k, log_g, val, gate = _hgrn_pre(x, lower_bounds[i], p, valid)
            o, s_t = _gla_scan(q_, k, log_g, val, gate, st_hg[j], p["norm_w"], HG_HEADS, HG_HEAD, HG_HEAD, zero_init)
            new_hg.append(s_t)
        else:
            p = {n: q[j] for n, q in gl.items()}
            q_, k, log_g, val, gate = _gla_pre(x, p, valid)
            o, s_t = _gla_scan(q_, k, log_g, val, gate, st_gl[j], p["norm_w"], GL_HEADS, GL_DK, GL_DV, zero_init)
            new_gl.append(s_t)
        x, buf = _proj_ffn(x, o, st_conv[i], p["wo"], (ln["ln1_w"][i], ln["ln1_b"][i]),
                           {n: q[i] for n, q in ffn.items()}, (ln["ln2_w"][i], ln["ln2_b"][i]), valid)
        new_conv.append(buf)
    return (x[:, :valid], rw_states.reshape(st_rw.shape), jnp.stack(new_shift), jnp.stack(new_hg), jnp.stack(new_gl),
            jnp.stack(new_conv))


def kernel(x_prompt, x_sample, state_rwkv, state_rwkv_shift, state_hgrn, state_gla, state_ffn_conv, rw_mix, rw_wr, rw_wk, rw_wv, rw_wo, rw_w0, rw_w1, rw_w2, rw_a0, rw_a1, rw_a2, rw_g1, rw_g2, rw_k_k, rw_k_a, rw_r_k, rw_lnx_w, rw_lnx_b, rw_v0, rw_v1, rw_v2, hg_wq, hg_wf, hg_wi, hg_wg, hg_wo, hg_norm_w, hg_lb_param, gl_wq, gl_wk, gl_wv, gl_wg, gl_gk1, gl_gk2, gl_gk_b, gl_wo, gl_norm_w, ffn_wu, ffn_wg, ffn_conv_w, ffn_conv_b, ffn_wd, ln1_w, ln1_b, ln2_w, ln2_b):
    bf = lambda w: w.astype(BF16)
    rw = dict(mix=rw_mix, wr=bf(rw_wr), wk=bf(rw_wk), wv=bf(rw_wv), wo=bf(rw_wo), w0=rw_w0, w1=bf(rw_w1),
              w2=bf(rw_w2), a0=rw_a0, a1=bf(rw_a1), a2=bf(rw_a2), g1=bf(rw_g1), g2=bf(rw_g2), k_k=rw_k_k,
              k_a=rw_k_a, r_k=rw_r_k, lnx_w=rw_lnx_w, lnx_b=rw_lnx_b)
    rw_vmix = (rw_v0, bf(rw_v1), bf(rw_v2))
    hg = dict(wq=bf(hg_wq), wf=bf(hg_wf), wi=bf(hg_wi), wg=bf(hg_wg), wo=bf(hg_wo), norm_w=hg_norm_w)
    gl = dict(wq=bf(gl_wq), wk=bf(gl_wk), wv=bf(gl_wv), wg=bf(gl_wg), gk1=bf(gl_gk1), gk2=bf(gl_gk2),
              gk_b=gl_gk_b, wo=bf(gl_wo), norm_w=gl_norm_w)
    ffn = dict(wu=bf(ffn_wu), wg=bf(ffn_wg), conv_w=ffn_conv_w, conv_b=ffn_conv_b, wd=bf(ffn_wd))
    ln = dict(ln1_w=ln1_w, ln1_b=ln1_b, ln2_w=ln2_w, ln2_b=ln2_b)
    lb_soft = jax.nn.softmax(hg_lb_param, axis=0)
    lower_bounds = jnp.cumsum(lb_soft, axis=0) - lb_soft[0]

    nb = x_prompt.shape[0]
    zeros = lambda s: jnp.zeros((s.shape[0], nb) + s.shape[2:], s.dtype)
    p_out = _run_trunk(x_prompt, x_prompt.shape[1], zeros(state_rwkv), zeros(state_rwkv_shift), zeros(state_hgrn),
                       zeros(state_gla), zeros(state_ffn_conv), rw, rw_vmix, hg, lower_bounds, gl, ffn, ln, True)
    t_s = x_sample.shape[1]
    t_pad = max(SUBLANES, 1 << (t_s - 1).bit_length())
    xs = jnp.pad(x_sample, ((0, 0), (0, t_pad - t_s), (0, 0)))
    s_out = _run_trunk(xs, t_s, state_rwkv, state_rwkv_shift, state_hgrn, state_gla, state_ffn_conv,
                       rw, rw_vmix, hg, lower_bounds, gl, ffn, ln, False)
    return (p_out[0], s_out[0]) + tuple(p_out[1:]) + tuple(s_out[1:])
```

```python
import functools
import math

import jax
import jax.numpy as jnp
from jax import lax
from jax.experimental import pallas as pl
from jax.experimental.pallas import tpu as pltpu

D_MODEL = 1024
DEPTH = 4
N_MIXERS = 3
RW_HEAD = 64
RW_HEADS = D_MODEL // RW_HEAD
RW_HEAD_SHIFT = RW_HEAD.bit_length() - 1
RW_LNX_EPS = 64e-5
HG_HEAD = 128
HG_HEADS = D_MODEL // HG_HEAD
GL_HEADS = 4
GL_DK = (D_MODEL // 2) // GL_HEADS
GL_DV = D_MODEL // GL_HEADS
GL_GATE_NORM = 16.0
CONV_W = 3
LN_EPS = 1e-5
RMS_EPS = 1e-5
DN_ALPHA = (2 * DEPTH) ** 0.25

LANES = 128
SUBLANES = 8
RW_GROUP = 256
RW_GROUP_HEADS = RW_GROUP // RW_HEAD
CHUNK = 32
RW_CHUNK = 64
RW_BATCH_PER_STEP = 2
RW_BATCH_PER_STEP_SHORT = 4
GLA_BATCH_PER_STEP = 4
SCAN_ROWS = 256
PRE_ROWS = 512
FFN_ROWS = 512
SUB_ROWS = 256
FFN_COLS = 256
VMEM_LIMIT = 56 * 1024 * 1024

BF16 = jnp.bfloat16
F32 = jnp.float32
ACT = BF16


def _dot(a, b):
    return jnp.dot(a, b, preferred_element_type=F32)


def _dot_nt(a, b):
    return lax.dot_general(a, b, (((1,), (1,)), ((), ())), preferred_element_type=F32)


def _dot_tn(a, b):
    return lax.dot_general(a, b, (((0,), (0,)), ((), ())), preferred_element_type=F32)


def _dot_hilo_rhs(sel, x):
    hi = x.astype(BF16)
    lo = (x - hi.astype(F32)).astype(BF16)
    return _dot(sel, hi) + _dot(sel, lo)


def _head_sums(x, ones_bd):
    return _dot(x.astype(BF16), ones_bd)


def _iota(shape, dim):
    return lax.broadcasted_iota(jnp.int32, shape, dim)


def _chunk_tri(n, c):
    row, col = _iota((n, n), 0), _iota((n, n), 1)
    sh = int(math.log2(c))
    keep = (col <= row) & ((row >> sh) == (col >> sh))
    return jnp.where(keep, 1.0, 0.0).astype(BF16)


def _softplus(t):
    return jnp.maximum(t, 0.0) + jnp.log(1.0 + jnp.exp(-jnp.abs(t)))


def _ln_rows(x, w, b):
    mu = jnp.mean(x, -1, keepdims=True)
    xc = x - mu
    var = jnp.mean(xc * xc, -1, keepdims=True)
    return xc * lax.rsqrt(var + LN_EPS) * w + b


def _resident(shape):
    return pl.BlockSpec(shape, lambda i, j: (0,) * len(shape), pipeline_mode=pl.Buffered(1))


def _sub_blocks(rows):
    sub = min(rows, SUB_ROWS)
    assert rows % sub == 0
    return [slice(s * sub, (s + 1) * sub) for s in range(rows // sub)]


def _params():
    return pltpu.CompilerParams(dimension_semantics=("arbitrary", "arbitrary"), vmem_limit_bytes=VMEM_LIMIT)


class _Rows:
    def __init__(self, bsz, t, valid, max_rows):
        self.bsz, self.t, self.valid = bsz, t, valid
        self.long = t >= max_rows
        if self.long:
            assert t % max_rows == 0 and valid == t
            self.rows, self.grid, self.view = max_rows, (bsz, t // max_rows), (bsz, t)
        else:
            total = bsz * t
            self.rows = min(total, max_rows)
            assert t & (t - 1) == 0 and self.rows % t == 0 and total % self.rows == 0
            self.grid, self.view = (1, total // self.rows), (1, total)

    def flat(self, x):
        return x.reshape(self.view + x.shape[2:])

    def unflat(self, x):
        return x.reshape((self.bsz, self.t) + x.shape[2:])

    def seq_spec(self, width):
        return pl.BlockSpec((None, self.rows, width), lambda i, j: (i, j, 0))

    def state_rows(self, state):
        if self.long:
            return state
        return self.flat(jnp.pad(state, ((0, 0), (0, self.t - state.shape[1]), (0, 0))))

    def state_spec(self, n, width):
        if self.long:
            return pl.BlockSpec((None, n, width), lambda i, j: (i, 0, 0))
        return self.seq_spec(width)


def _prev_rows(x, shift, rl, t_idx, carry_scr, fill_ref, n_prev):
    rows, width = x.shape
    row = _iota((rows, width), 0)
    rolled = pltpu.roll(x, shift, 0)
    if rl.long:
        out = rolled
        for i in range(shift):
            src = n_prev - shift + i
            out = jnp.where(row == i, carry_scr[src:src + 1, :], out)
        return out
    fill = fill_ref[...]
    if n_prev != shift:
        fill = pltpu.roll(fill, rows - (n_prev - shift), 0)
    return jnp.where((row & (rl.t - 1)) >= shift, rolled, fill)


def _rwkv_pre_body(*refs, rl, has_vmix):
    it = iter(refs)
    x_ref, fill_ref = next(it), next(it)
    vf_ref = next(it) if has_vmix else None
    mix_ref, wr_ref, wk_ref, wv_ref, w0_ref, w1_ref, w2_ref = (next(it) for _ in range(7))
    a0_ref, a1_ref, a2_ref, g1_ref, g2_ref, kk_ref, ka_ref = (next(it) for _ in range(7))
    v0_ref, v1_ref, v2_ref = (next(it) for _ in range(3)) if has_vmix else (None, None, None)
    r_out, wl_out, k_out, v_out, a_out, b_out, g_out, carry_scr = (next(it) for _ in range(8))

    t_idx = pl.program_id(1)
    if rl.long:
        @pl.when(t_idx == 0)
        def _load_state():
            carry_scr[0:1, :] = fill_ref[...]
    bd = (_iota((RW_GROUP, RW_GROUP), 0) >> RW_HEAD_SHIFT) == (_iota((RW_GROUP, RW_GROUP), 1) >> RW_HEAD_SHIFT)
    ones_bd = jnp.where(bd, 1.0, 0.0).astype(BF16)

    subs = _sub_blocks(rl.rows)
    stage1 = []
    for rs in subs:
        rows = rs.stop - rs.start
        x = x_ref[rs, :]
        x_prev = _prev_rows(x, 1, rl, t_idx, carry_scr, None if rl.long else fill_ref.at[rs, :], 1)
        if rl.long:
            carry_scr[0:1, :] = x[rows - 1:rows, :]
        xx = x_prev - x
        xr, xw, xk, xv, xa, xg = [(x + xx * mix_ref[j:j + 1, :]).astype(BF16) for j in range(6)]
        stage1.append(dict(r=_dot(xr, wr_ref[...]), k=_dot(xk, wk_ref[...]), v=_dot(xv, wv_ref[...]),
                           lw=_dot(xw, w1_ref[...]), la=_dot(xa, a1_ref[...]), lg=_dot(xg, g1_ref[...]),
                           lv=_dot(xv, v1_ref[...]) if has_vmix else None))
    for rs, s1 in zip(subs, stage1):
        rows = rs.stop - rs.start
        r, k, v = s1["r"], s1["k"], s1["v"]
        lora_w = _dot(jnp.tanh(s1["lw"]).astype(BF16), w2_ref[...])
        wl = -jnp.exp(-_softplus(-(w0_ref[...] + lora_w)) - 0.5)
        a = jax.nn.sigmoid(a0_ref[...] + _dot(s1["la"].astype(BF16), a2_ref[...]))
        if has_vmix:
            gate = jax.nn.sigmoid(v0_ref[...] + _dot(s1["lv"].astype(BF16), v2_ref[...]))
            v = v + (vf_ref[rs, :].astype(F32) - v) * gate
        g_out[rs, :] = _dot(jax.nn.sigmoid(s1["lg"]).astype(BF16), g2_ref[...]).astype(g_out.dtype)

        kk = k * kk_ref[...]
        k = k * (1.0 + (a - 1.0) * ka_ref[...])
        if not rl.long and rl.valid < rl.t:
            live = (_iota((rows, D_MODEL), 0) & (rl.t - 1)) < rl.valid
            wl, k, v, kk = (jnp.where(live, z, 0.0) for z in (wl, k, v, kk))
        r_out[rs, :] = r.astype(r_out.dtype)
        wl_out[rs, :] = wl
        k_out[rs, :] = k.astype(k_out.dtype)
        v_out[rs, :] = v.astype(v_out.dtype)
        for g in range(D_MODEL // RW_GROUP):
            lanes = slice(g * RW_GROUP, (g + 1) * RW_GROUP)
            kkg = kk[:, lanes]
            kkn = kkg / jnp.maximum(jnp.sqrt(_head_sums(kkg * kkg, ones_bd)), 1e-12)
            a_out[rs, lanes] = (-kkn).astype(a_out.dtype)
            b_out[rs, lanes] = (kkn * a[:, lanes]).astype(b_out.dtype)


def _rwkv_pre(x, shift, v_first, vmix, p, valid):
    bsz, t, d = x.shape
    rl = _Rows(bsz, t, valid, PRE_ROWS)
    has_vmix = vmix is not None
    seq = rl.seq_spec(d)
    vec = lambda z: z.reshape(1, d)
    args = [rl.flat(x), rl.state_rows(shift[:, None, :])]
    specs = [seq, rl.state_spec(1, d)]
    if has_vmix:
        args.append(rl.flat(v_first))
        specs.append(seq)
    weights = [p["mix"], p["wr"], p["wk"], p["wv"], vec(p["w0"]), p["w1"], p["w2"], vec(p["a0"]), p["a1"], p["a2"],
               p["g1"], p["g2"], vec(p["k_k"]), vec(p["k_a"])]
    if has_vmix:
        weights += [vec(vmix[0]), vmix[1], vmix[2]]
    args += weights
    specs += [_resident(w.shape) for w in weights]
    outs = pl.pallas_call(
        functools.partial(_rwkv_pre_body, rl=rl, has_vmix=has_vmix),
        grid=rl.grid,
        in_specs=specs,
        out_specs=[seq] * 7,
        out_shape=[jax.ShapeDtypeStruct(rl.view + (d,), F32 if i == 1 else ACT) for i in range(7)],
        scratch_shapes=[pltpu.VMEM((SUBLANES, d), F32)],
        compiler_params=_params(),
        name="rwkv_pre",
    )(*args)
    return [rl.unflat(o) for o in outs]


def _rwkv_body(r_ref, wl_ref, k_ref, v_ref, a_ref, b_ref, g_ref, s0_ref, rk_ref, lw_ref, lb_ref, *rest,
               chunk, tile, nb, zero_init):
    y_ref, st_ref, s_scr, l_scr = rest[-4:]
    c = chunk
    gh = RW_GROUP_HEADS
    n_groups = D_MODEL // RW_GROUP
    t_idx = pl.program_id(1)
    log2c = int(math.log2(c))

    bd_r, bd_c = _iota((RW_GROUP, RW_GROUP), 0), _iota((RW_GROUP, RW_GROUP), 1)
    bd256 = (bd_r >> RW_HEAD_SHIFT) == (bd_c >> RW_HEAD_SHIFT)
    ones_bd = jnp.where(bd256, 1.0, 0.0).astype(BF16)
    hm = (_iota((gh * c, RW_GROUP), 0) >> log2c) == (_iota((gh * c, RW_GROUP), 1) >> RW_HEAD_SHIFT)
    hm_b = jnp.where(hm, 1.0, 0.0).astype(BF16)
    bdm = (_iota((gh * c, gh * c), 0) >> log2c) == (_iota((gh * c, gh * c), 1) >> log2c)
    bdm_b = jnp.where(bdm, 1.0, 0.0).astype(BF16)
    trow = _iota((c, gh * c), 0)
    tcol = _iota((c, gh * c), 1) & (c - 1)
    strict = tcol < trow
    incl = tcol <= trow
    chains = [(n, g) for n in range(nb) for g in range(n_groups)]

    def head_block(i, n, g, h):
        blk = slice(h * RW_HEAD, (h + 1) * RW_HEAD)
        return (i, blk, blk), (n, slice(g * RW_GROUP + h * RW_HEAD, g * RW_GROUP + (h + 1) * RW_HEAD), slice(None))

    @pl.when(t_idx == 0)
    def _init():
        for i, (n, g) in enumerate(chains):
            s_scr[i] = jnp.zeros((RW_GROUP, RW_GROUP), F32)
            if not zero_init:
                for h in range(gh):
                    dst, src = head_block(i, n, g, h)
                    s_scr[dst] = s0_ref[src]

    tri = _chunk_tri(tile, c)
    for n in range(nb):
        l_scr[n] = _dot_hilo_rhs(tri, wl_ref[n])

    def stack(x_b):
        return jnp.concatenate([x_b] * gh, axis=0) * hm_b

    def bd_stack(m_b):
        return jnp.concatenate([m_b] * gh, axis=0) * bdm_b

    def chunk_step(ci, carry):
        rows = pl.ds(pl.multiple_of(ci * c, c), c)
        ld = []
        for n, g in chains:
            lanes = slice(g * RW_GROUP, (g + 1) * RW_GROUP)
            f32 = lambda ref: ref[n, rows, lanes].astype(F32)
            ld.append(dict(r=f32(r_ref), wl=wl_ref[n, rows, lanes], k=f32(k_ref), v=v_ref[n, rows, lanes].astype(BF16),
                           a=f32(a_ref), b=f32(b_ref), lc=l_scr[n, rows, lanes]))
        for i, x in enumerate(ld):
            lc = x["lc"]
            x["lend"] = lc[c - 1:c, :]
            w_inv = jnp.exp(-lc)
            at = x["a"] * jnp.exp(lc - x["wl"])
            rt = x["r"] * jnp.exp(lc)
            x["ar"] = jnp.concatenate([at, rt], axis=0).astype(BF16)
            x["bts"] = stack((x["b"] * w_inv).astype(BF16))
            x["kts"] = stack((x["k"] * w_inv).astype(BF16))
            x["vs"] = stack(x["v"])
            x["sg"] = s_scr[i]
        for x in ld:
            x["sb"] = _dot_nt(x["ar"], x["bts"])
            x["sk"] = _dot_nt(x["ar"], x["kts"])
            x["x2"] = _dot_nt(x["ar"], x["sg"].astype(BF16))
        for x, (n, g) in zip(ld, chains):
            lanes = slice(g * RW_GROUP, (g + 1) * RW_GROUP)
            x["bonus"] = _head_sums(x["r"] * x["k"] * rk_ref[:, lanes], ones_bd) * x["v"].astype(F32)
        for x in ld:
            x["p"] = jnp.where(strict, x["sb"][:c], 0.0).astype(BF16)
            x["arb"] = jnp.where(incl, x["sb"][c:], 0.0).astype(BF16)
            kv = _dot(jnp.where(jnp.concatenate([strict, incl], axis=0), x["sk"], 0.0).astype(BF16), x["vs"])
            x["u"] = x["x2"][:c] + kv[:c]
            x["ykv"] = x["x2"][c:] + kv[c:]
        for it in range(log2c):
            for x in ld:
                x["u"] = x["u"] + _dot(x["p"], stack(x["u"].astype(BF16)))
            if it < log2c - 1:
                for x in ld:
                    x["p"] = _dot(x["p"], bd_stack(x["p"])).astype(BF16)
        for x in ld:
            x["y"] = x["ykv"] + _dot(x["arb"], stack(x["u"].astype(BF16)))
        for i, x in enumerate(ld):
            w_end = jnp.exp(x["lend"] - x["lc"])
            uv = jnp.concatenate([x["u"].astype(BF16), x["v"]], axis=0)
            bk = jnp.concatenate([x["b"] * w_end, x["k"] * w_end], axis=0).astype(BF16)
            s_scr[i] = x["sg"] * jnp.exp(x["lend"]) + jnp.where(bd256, _dot_tn(uv, bk), 0.0)
        inv_n = 1.0 / RW_HEAD
        for x in ld:
            x["yc"] = x["y"] - _head_sums(x["y"], ones_bd) * inv_n
        for x, (n, g) in zip(ld, chains):
            lanes = slice(g * RW_GROUP, (g + 1) * RW_GROUP)
            var = _head_sums(x["yc"] * x["yc"], ones_bd) * inv_n
            yn = x["yc"] * lax.rsqrt(var + RW_LNX_EPS) * lw_ref[:, lanes] + lb_ref[:, lanes]
            y_ref[n, rows, lanes] = ((yn + x["bonus"]) * g_ref[n, rows, lanes].astype(F32)).astype(y_ref.dtype)
        return carry

    lax.fori_loop(0, tile // c, chunk_step, 0)

    @pl.when(t_idx == pl.num_programs(1) - 1)
    def _final():
        for i, (n, g) in enumerate(chains):
            for h in range(gh):
                src, dst = head_block(i, n, g, h)
                st_ref[dst] = s_scr[src]


def _rwkv_scan(r, wl, k, v, a, b, g, s0, p, zero_init, layer, n_layers, stacked):
    bsz, t, d = r.shape
    c = min(RW_CHUNK, t)
    tile = min(t, SCAN_ROWS)
    nb = RW_BATCH_PER_STEP if t > c else RW_BATCH_PER_STEP_SHORT
    assert t % tile == 0 and tile % c == 0 and c % SUBLANES == 0 and bsz % nb == 0
    s0r = s0.reshape(bsz, d, RW_HEAD)
    seq = pl.BlockSpec((nb, tile, d), lambda i, j: (i, j, 0))
    st = pl.BlockSpec((nb, d, RW_HEAD), lambda i, j: (i, 0, 0))
    vec = _resident((1, d))
    st_out = pl.BlockSpec((None, nb, d, RW_HEAD), lambda i, j: (layer, i, 0, 0))
    args = [r, wl, k, v, a, b, g, s0r, p["r_k"].reshape(1, d), p["lnx_w"].reshape(1, d), p["lnx_b"].reshape(1, d)]
    in_specs = [seq] * 7 + [st, vec, vec, vec]
    aliases = {}
    if stacked is not None:
        aliases = {len(args): 1}
        args.append(stacked)
        in_specs.append(pl.BlockSpec(memory_space=pl.ANY))
    return pl.pallas_call(
        functools.partial(_rwkv_body, chunk=c, tile=tile, nb=nb, zero_init=zero_init),
        grid=(bsz // nb, t // tile),
        in_specs=in_specs,
        out_specs=[seq, st_out],
        out_shape=[jax.ShapeDtypeStruct((bsz, t, d), ACT), jax.ShapeDtypeStruct((n_layers, bsz, d, RW_HEAD), F32)],
        scratch_shapes=[pltpu.VMEM((nb * (d // RW_GROUP), RW_GROUP, RW_GROUP), F32),
                        pltpu.VMEM((nb, tile, d), F32)],
        input_output_aliases=aliases,
        compiler_params=_params(),
        name="rwkv_scan",
    )(*args)


def _hgrn_pre_body(x_ref, w_ref, llb_ref, l1m_ref, omlb_ref, q_out, k_out, g_out, i_out, gate_out, *, rl):
    d = D_MODEL
    subs = _sub_blocks(rl.rows)
    ys = [_dot(x_ref[rs, :].astype(BF16), w_ref[...]) for rs in subs]
    for rs, y in reversed(list(zip(subs, ys))):
        qz, f = y[:, 0:d], y[:, d:2 * d]
        i_out[rs, :] = y[:, 2 * d:3 * d].astype(i_out.dtype)
        gate_out[rs, :] = y[:, 3 * d:4 * d].astype(gate_out.dtype)
        q_out[rs, :] = (qz * jax.nn.sigmoid(qz) * HG_HEAD ** -0.5).astype(q_out.dtype)
        la = llb_ref[...]
        lb = l1m_ref[...] - _softplus(-f)
        log_g = jnp.maximum(la, lb) + jnp.log(1.0 + jnp.exp(-jnp.abs(la - lb)))
        k = omlb_ref[...] * jax.nn.sigmoid(-f)
        if not rl.long and rl.valid < rl.t:
            live = (_iota(f.shape, 0) & (rl.t - 1)) < rl.valid
            log_g, k = jnp.where(live, log_g, 0.0), jnp.where(live, k, 0.0)
        g_out[rs, :] = log_g
        k_out[rs, :] = k.astype(k_out.dtype)


def _hgrn_pre(x, lb, p, valid):
    bsz, t, d = x.shape
    rl = _Rows(bsz, t, valid, PRE_ROWS)
    seq = rl.seq_spec(d)
    vecs = [jnp.log(lb).reshape(1, d), jnp.log1p(-lb).reshape(1, d), (1.0 - lb).reshape(1, d)]
    weights = [jnp.concatenate([p["wq"], p["wf"], p["wi"], p["wg"]], axis=1)] + vecs
    outs = pl.pallas_call(
        functools.partial(_hgrn_pre_body, rl=rl),
        grid=rl.grid,
        in_specs=[seq] + [_resident(w.shape) for w in weights],
        out_specs=[seq] * 5,
        out_shape=[jax.ShapeDtypeStruct(rl.view + (d,), F32 if i == 2 else ACT) for i in range(5)],
        compiler_params=_params(),
        name="hgrn_pre",
    )(rl.flat(x), *weights)
    return [rl.unflat(o) for o in outs]


def _gla_pre_body(x_ref, w_ref, gk2_ref, gkb_ref, q_out, k_out, g_out, v_out, gate_out, *, rl):
    d, dkk = D_MODEL, GL_HEADS * GL_DK
    subs = _sub_blocks(rl.rows)
    ys = [_dot(x_ref[rs, :].astype(BF16), w_ref[...]) for rs in subs]
    for rs, y in zip(subs, ys):
        q_out[rs, :] = (y[:, 0:dkk] * GL_DK ** -0.5).astype(q_out.dtype)
        k = y[:, dkk:2 * dkk]
        v_out[rs, :] = y[:, 2 * dkk:2 * dkk + d].astype(v_out.dtype)
        gate_out[rs, :] = y[:, 2 * dkk + d:2 * dkk + 2 * d].astype(gate_out.dtype)
        gk = _dot(y[:, 2 * dkk + 2 * d:].astype(BF16), gk2_ref[...]) + gkb_ref[...]
        log_g = -_softplus(-gk) * (1.0 / GL_GATE_NORM)
        if not rl.long and rl.valid < rl.t:
            live = (_iota(k.shape, 0) & (rl.t - 1)) < rl.valid
            log_g, k = jnp.where(live, log_g, 0.0), jnp.where(live, k, 0.0)
        g_out[rs, :] = log_g
        k_out[rs, :] = k.astype(k_out.dtype)


def _gla_pre(x, p, valid):
    bsz, t, d = x.shape
    dkk = GL_HEADS * GL_DK
    rl = _Rows(bsz, t, valid, PRE_ROWS)
    rank = p["gk1"].shape[1]
    gk1 = jnp.pad(p["gk1"], ((0, 0), (0, LANES - rank)))
    gk2 = jnp.pad(p["gk2"], ((0, LANES - rank), (0, 0)))
    weights = [jnp.concatenate([p["wq"], p["wk"], p["wv"], p["wg"], gk1], axis=1), gk2, p["gk_b"].reshape(1, dkk)]
    kseq, vseq = rl.seq_spec(dkk), rl.seq_spec(d)
    kshape, vshape = jax.ShapeDtypeStruct(rl.view + (dkk,), ACT), jax.ShapeDtypeStruct(rl.view + (d,), ACT)
    outs = pl.pallas_call(
        functools.partial(_gla_pre_body, rl=rl),
        grid=rl.grid,
        in_specs=[vseq] + [_resident(w.shape) for w in weights],
        out_specs=[kseq, kseq, kseq, vseq, vseq],
        out_shape=[kshape, kshape, jax.ShapeDtypeStruct(rl.view + (dkk,), F32), vshape, vshape],
        compiler_params=_params(),
        name="gla_pre",
    )(rl.flat(x), *weights)
    return [rl.unflat(o) for o in outs]


def _gla_body(q_ref, k_ref, g_ref, v_ref, gate_ref, s0_ref, gain_ref, o_ref, st_ref, s_scr, l_scr,
              *, chunk, tile, nb, heads, dk, dv, zero_init):
    c = chunk
    ref_row = (c - 1) // 2
    t_idx = pl.program_id(1)
    incl = _iota((c, c), 1) <= _iota((c, c), 0)
    chains = [(n, h) for n in range(nb) for h in range(heads)]

    @pl.when(t_idx == 0)
    def _init():
        for i, (n, h) in enumerate(chains):
            s_scr[i] = jnp.zeros((dv, dk), F32) if zero_init else s0_ref[n, h].T

    tri = _chunk_tri(tile, c)
    for n in range(nb):
        l_scr[n] = _dot_hilo_rhs(tri, g_ref[n])

    def chunk_step(ci, carry):
        rows = pl.ds(pl.multiple_of(ci * c, c), c)
        ld = []
        for i, (n, h) in enumerate(chains):
            kl = slice(h * dk, (h + 1) * dk)
            q, k = q_ref[n, rows, kl].astype(F32), k_ref[n, rows, kl].astype(F32)
            bc = l_scr[n, rows, kl]
            b_ref = bc[ref_row:ref_row + 1, :]
            b_last = bc[c - 1:c, :]
            ld.append(dict(qi=(q * jnp.exp(bc - b_ref)).astype(BF16), ki=(k * jnp.exp(b_ref - bc)).astype(BF16),
                           qd=(q * jnp.exp(bc)).astype(BF16), kd=(k * jnp.exp(b_last - bc)).astype(BF16),
                           v=v_ref[n, rows, h * dv:(h + 1) * dv].astype(BF16), w_last=jnp.exp(b_last),
                           st=s_scr[i]))
        for x in ld:
            x["att"] = _dot_nt(x["qi"], x["ki"])
            x["os"] = _dot_nt(x["qd"], x["st"].astype(BF16))
            x["ds"] = _dot_tn(x["v"], x["kd"])
        for i, (x, (n, h)) in enumerate(zip(ld, chains)):
            vl = slice(h * dv, (h + 1) * dv)
            s_scr[i] = x["st"] * x["w_last"] + x["ds"]
            o = x["os"] + _dot(jnp.where(incl, x["att"], 0.0).astype(BF16), x["v"])
            gate = gate_ref[n, rows, vl].astype(F32)
            on = o * lax.rsqrt(jnp.mean(o * o, -1, keepdims=True) + RMS_EPS) * gain_ref[...]
            o_ref[n, rows, vl] = (on * (gate * jax.nn.sigmoid(gate))).astype(o_ref.dtype)
        return carry

    lax.fori_loop(0, tile // c, chunk_step, 0)

    @pl.when(t_idx == pl.num_programs(1) - 1)
    def _final():
        for i, (n, h) in enumerate(chains):
            st_ref[n, h] = s_scr[i].T


def _gla_scan(q, k, g, v, gate, s0, gain, heads, dk, dv, zero_init):
    bsz, t, _ = q.shape
    c = min(CHUNK, t)
    tile = min(t, SCAN_ROWS)
    nb = GLA_BATCH_PER_STEP
    assert t % tile == 0 and tile % c == 0 and c % SUBLANES == 0 and bsz % nb == 0
    kseq = pl.BlockSpec((nb, tile, heads * dk), lambda i, j: (i, j, 0))
    vseq = pl.BlockSpec((nb, tile, heads * dv), lambda i, j: (i, j, 0))
    st = pl.BlockSpec((nb, heads, dk, dv), lambda i, j: (i, 0, 0, 0))
    return pl.pallas_call(
        functools.partial(_gla_body, chunk=c, tile=tile, nb=nb, heads=heads, dk=dk, dv=dv, zero_init=zero_init),
        grid=(bsz // nb, t // tile),
        in_specs=[kseq, kseq, kseq, vseq, vseq, st, _resident((1, dv))],
        out_specs=[vseq, pl.BlockSpec((None, nb, heads, dk, dv), lambda i, j: (0, i, 0, 0, 0))],
        out_shape=[jax.ShapeDtypeStruct((bsz, t, heads * dv), ACT),
                   jax.ShapeDtypeStruct((1, bsz, heads, dk, dv), F32)],
        scratch_shapes=[pltpu.VMEM((nb * heads, dv, dk), F32), pltpu.VMEM((nb, tile, heads * dk), F32)],
        compiler_params=_params(),
        name="gla_scan",
    )(q, k, g, v, gate, s0, gain.reshape(1, dv))


def _cf_body(x_ref, o_ref, fill_ref, wo_ref, l1w_ref, l1b_ref, wu_ref, wg_ref, wd_ref, cw_ref, cb_ref,
             l2w_ref, l2b_ref, out_ref, z_ref, h_scr, carry_scr, *, rl):
    n_state = CONV_W - 1
    t_idx = pl.program_id(1)
    subs = _sub_blocks(rl.rows)
    sub = subs[0].stop
    if rl.long:
        @pl.when(t_idx == 0)
        def _load_state():
            carry_scr[...] = fill_ref[...]

    x1s = [_ln_rows(DN_ALPHA * x_ref[rs, :] + _dot(o_ref[rs, :], wo_ref[...]), l1w_ref[...], l1b_ref[...])
           for rs in subs]
    for rs, x1 in zip(subs, x1s):
        x1b = x1.astype(BF16)
        for j in range(wu_ref.shape[1] // FFN_COLS):
            cs = slice(j * FFN_COLS, (j + 1) * FFN_COLS)
            zc = _dot(x1b, wg_ref[:, cs])
            uc = _dot(x1b, wu_ref[:, cs])
            carry_c = carry_scr.at[:, cs] if rl.long else None
            fill_c = None if rl.long else fill_ref.at[rs, cs]
            zp1 = _prev_rows(zc, 1, rl, t_idx, carry_c, fill_c, n_state)
            zp2 = _prev_rows(zc, 2, rl, t_idx, carry_c, fill_c, n_state)
            if rl.long:
                carry_scr[:, cs] = zc[sub - n_state:sub, :]
            else:
                z_ref[rs, cs] = zc
            cw = cw_ref[:, cs]
            pre = cb_ref[:, cs] + cw[0:1] * zp2 + cw[1:2] * zp1 + cw[2:3] * zc
            h_scr[rs, cs] = (pre * jax.nn.sigmoid(pre) * uc).astype(BF16)
        out_ref[rs, :] = _ln_rows(DN_ALPHA * x1 + _dot(h_scr[rs, :], wd_ref[...]), l2w_ref[...], l2b_ref[...])
    if rl.long:
        @pl.when(t_idx == pl.num_programs(1) - 1)
        def _store_state():
            z_ref[...] = carry_scr[...]


def _proj_ffn(x, o, buf, wo, ln1, p, ln2, valid):
    bsz, t, d = x.shape
    f = p["wu"].shape[1]
    n_state = CONV_W - 1
    rl = _Rows(bsz, t, valid, FFN_ROWS if t >= FFN_ROWS else FFN_ROWS // 2)
    assert f % FFN_COLS == 0 and valid >= n_state
    seq = rl.seq_spec(d)
    vec = lambda n: _resident((1, n))
    if rl.long:
        z_spec, z_shape = pl.BlockSpec((None, n_state, f), lambda i, j: (i, 0, 0)), (bsz, n_state, f)
    else:
        z_spec, z_shape = rl.seq_spec(f), rl.view + (f,)
    out, z = pl.pallas_call(
        functools.partial(_cf_body, rl=rl),
        grid=rl.grid,
        in_specs=[seq, seq, rl.state_spec(n_state, f),
                  _resident((d, d)), vec(d), vec(d), _resident((d, f)), _resident((d, f)), _resident((f, d)),
                  _resident((CONV_W, f)), vec(f), vec(d), vec(d)],
        out_specs=[seq, z_spec],
        out_shape=[jax.ShapeDtypeStruct(rl.view + (d,), F32), jax.ShapeDtypeStruct(z_shape, F32)],
        scratch_shapes=[pltpu.VMEM((rl.rows, f), BF16), pltpu.VMEM((n_state, f), F32)],
        compiler_params=_params(),
        name="proj_ffn",
    )(rl.flat(x), rl.flat(o), rl.state_rows(buf), wo, ln1[0].reshape(1, d), ln1[1].reshape(1, d),
      p["wu"], p["wg"], p["wd"], p["conv_w"], p["conv_b"].reshape(1, f), ln2[0].reshape(1, d), ln2[1].reshape(1, d))
    if rl.long:
        return out, z
    return rl.unflat(out), rl.unflat(z)[:, valid - n_state:valid]


def _cat_layers(states):
    return states[0] if len(states) == 1 else jnp.concatenate(states, axis=0)


def _run_trunk(x, valid, st_rw, st_shift, st_hg, st_gl, st_conv, rw, rw_vmix, hg, lower_bounds, gl, ffn, ln, zero_init):
    new_shift, new_hg, new_gl, new_conv = [], [], [], []
    v_first = rw_states = None
    for i in range(DEPTH):
        j = i // N_MIXERS
        kind = i % N_MIXERS
        if kind == 0:
            p = {n: q[j] for n, q in rw.items()}
            vm = None if j == 0 else tuple(q[j - 1] for q in rw_vmix)
            r, wl, k, v, a, b, g = _rwkv_pre(x, st_shift[j], v_first, vm, p, valid)
            if vm is None:
                v_first = v
            o, rw_states = _rwkv_scan(r, wl, k, v, a, b, g, st_rw[j], p, zero_init, j, st_rw.shape[0], rw_states)
            new_shift.append(x[:, valid - 1])
        elif kind == 1:
            p = {n: q[j] for n, q in hg.items()}
            q_, k, log_g, val, gate = _hgrn_pre(x, lower_bounds[i], p, valid)
            o, s_t = _gla_scan(q_, k, log_g, val, gate, st_hg[j], p["norm_w"], HG_HEADS, HG_HEAD, HG_HEAD, zero_init)
            new_hg.append(s_t)
        else:
            p = {n: q[j] for n, q in gl.items()}
            q_, k, log_g, val, gate = _gla_pre(x, p, valid)
            o, s_t = _gla_scan(q_, k, log_g, val, gate, st_gl[j], p["norm_w"], GL_HEADS, GL_DK, GL_DV, zero_init)
            new_gl.append(s_t)
        x, buf = _proj_ffn(x, o, st_conv[i], p["wo"], (ln["ln1_w"][i], ln["ln1_b"][i]),
                           {n: q[i] for n, q in ffn.items()}, (ln["ln2_w"][i], ln["ln2_b"][i]), valid)
        new_conv.append(buf)
    return (x[:, :valid], rw_states.reshape(st_rw.shape), jnp.stack(new_shift), _cat_layers(new_hg), _cat_layers(new_gl),
            jnp.stack(new_conv))


def kernel(x_prompt, x_sample, state_rwkv, state_rwkv_shift, state_hgrn, state_gla, state_ffn_conv, rw_mix, rw_wr, rw_wk, rw_wv, rw_wo, rw_w0, rw_w1, rw_w2, rw_a0, rw_a1, rw_a2, rw_g1, rw_g2, rw_k_k, rw_k_a, rw_r_k, rw_lnx_w, rw_lnx_b, rw_v0, rw_v1, rw_v2, hg_wq, hg_wf, hg_wi, hg_wg, hg_wo, hg_norm_w, hg_lb_param, gl_wq, gl_wk, gl_wv, gl_wg, gl_gk1, gl_gk2, gl_gk_b, gl_wo, gl_norm_w, ffn_wu, ffn_wg, ffn_conv_w, ffn_conv_b, ffn_wd, ln1_w, ln1_b, ln2_w, ln2_b):
    bf = lambda w: w.astype(BF16)
    rw = dict(mix=rw_mix, wr=bf(rw_wr), wk=bf(rw_wk), wv=bf(rw_wv), wo=bf(rw_wo), w0=rw_w0, w1=bf(rw_w1),
              w2=bf(rw_w2), a0=rw_a0, a1=bf(rw_a1), a2=bf(rw_a2), g1=bf(rw_g1), g2=bf(rw_g2), k_k=rw_k_k,
              k_a=rw_k_a, r_k=rw_r_k, lnx_w=rw_lnx_w, lnx_b=rw_lnx_b)
    rw_vmix = (rw_v0, bf(rw_v1), bf(rw_v2))
    hg = dict(wq=bf(hg_wq), wf=bf(hg_wf), wi=bf(hg_wi), wg=bf(hg_wg), wo=bf(hg_wo), norm_w=hg_norm_w)
    gl = dict(wq=bf(gl_wq), wk=bf(gl_wk), wv=bf(gl_wv), wg=bf(gl_wg), gk1=bf(gl_gk1), gk2=bf(gl_gk2),
              gk_b=gl_gk_b, wo=bf(gl_wo), norm_w=gl_norm_w)
    ffn = dict(wu=bf(ffn_wu), wg=bf(ffn_wg), conv_w=ffn_conv_w, conv_b=ffn_conv_b, wd=bf(ffn_wd))
    ln = dict(ln1_w=ln1_w, ln1_b=ln1_b, ln2_w=ln2_w, ln2_b=ln2_b)
    lb_soft = jax.nn.softmax(hg_lb_param, axis=0)
    lower_bounds = jnp.cumsum(lb_soft, axis=0) - lb_soft[0]

    nb = x_prompt.shape[0]
    zeros = lambda s: jnp.zeros((s.shape[0], nb) + s.shape[2:], s.dtype)
    p_out = _run_trunk(x_prompt, x_prompt.shape[1], zeros(state_rwkv), zeros(state_rwkv_shift), zeros(state_hgrn),
                       zeros(state_gla), zeros(state_ffn_conv), rw, rw_vmix, hg, lower_bounds, gl, ffn, ln, True)
    t_s = x_sample.shape[1]
    t_pad = max(SUBLANES, 1 << (t_s - 1).bit_length())
    xs = jnp.pad(x_sample, ((0, 0), (0, t_pad - t_s), (0, 0)))
    s_out = _run_trunk(xs, t_s, state_rwkv, state_rwkv_shift, state_hgrn, state_gla, state_ffn_conv,
                       rw, rw_vmix, hg, lower_bounds, gl, ffn, ln, False)
    return (p_out[0], s_out[0]) + tuple(p_out[1:]) + tuple(s_out[1:])
```
